```python
import math, functools
import jax, jax.numpy as jnp
from jax import lax
import numpy as np

D_MODEL = 1024
BATCH = 32
SEQ = 256
DEPTH = 4
DEC_BATCH = 8
DEC_SEQ = 2048
PAST_LEN = 256

GRID_W = 64
N_MIXERS = 4
N_LAYERS_A = (DEPTH + 3) // 4
N_LAYERS_B = (DEPTH + 2) // 4
N_LAYERS_C = (DEPTH + 1) // 4
N_LAYERS_D = DEPTH // 4
Q_BLOCK = 128
ROPE_THETA = 10000.0
NORM_EPS = 1e-6

GDN_HEADS = 8
GDN_DK = 128
GDN_DV = 128
GDN_KDIM = GDN_HEADS * GDN_DK
GDN_VDIM = GDN_HEADS * GDN_DV
GDN_CHUNK = 64
CONV_W = 3
DIFF_HEADS = 8
DIFF_DH = 64
DIFF_DV = 2 * DIFF_DH
GQA_HEADS = 8
GQA_KV = 2
GQA_GROUP = GQA_HEADS // GQA_KV
GQA_HD = 128
SWA_HEADS = 16
SWA_KV = 4
SWA_GROUP = SWA_HEADS // SWA_KV
SWA_HD = 64
WINDOW = 128
N_EXPERTS = 16
EXPERT_FF = 1024
EC_CAPACITY_FACTOR = 2

kernel_name = 'hybrid_diffusion_prefix_trunk_step'


def rms_norm(x, g):
    xf = x.astype(jnp.float32)
    y = xf * lax.rsqrt(jnp.mean(xf * xf, axis=-1, keepdims=True) + NORM_EPS)
    return (y * g.astype(jnp.float32)).astype(x.dtype)


def l2_normalize(x):
    xf = x.astype(jnp.float32)
    return (xf * lax.rsqrt(jnp.sum(xf * xf, axis=-1, keepdims=True) + NORM_EPS)).astype(x.dtype)


def axial_rope(n_tokens, dim):
    rows = n_tokens // GRID_W
    row = jnp.broadcast_to(jnp.arange(rows, dtype=jnp.float32)[:, None], (rows, GRID_W)).reshape(-1)
    col = jnp.broadcast_to(jnp.arange(GRID_W, dtype=jnp.float32)[None, :], (rows, GRID_W)).reshape(-1)
    n_freq = dim // 4
    inv = ROPE_THETA ** (-jnp.arange(n_freq, dtype=jnp.float32) / n_freq)
    ang = jnp.concatenate([row[:, None] * inv, col[:, None] * inv], axis=-1)
    return jnp.cos(ang), jnp.sin(ang)


def apply_rope(x, cos, sin):
    shape = (1, x.shape[1]) + (1,) * (x.ndim - 3) + (cos.shape[-1],)
    cos, sin = cos.reshape(shape), sin.reshape(shape)
    x1, x2 = jnp.split(x.astype(jnp.float32), 2, axis=-1)
    return jnp.concatenate([x1 * cos - x2 * sin, x2 * cos + x1 * sin], axis=-1).astype(x.dtype)


def sweep_query_blocks(block_fn, q):
    b, t = q.shape[:2]
    nb = t // Q_BLOCK
    qb = jnp.moveaxis(q.reshape((b, nb, Q_BLOCK) + q.shape[2:]), 1, 0)
    out = jnp.moveaxis(lax.map(block_fn, (jnp.arange(nb), qb)), 0, 1)
    return out.reshape((b, t) + out.shape[3:])


def centred_depthwise_conv(x, w):
    pad = CONV_W // 2
    return lax.conv_general_dilated(x, w[:, None, :].astype(x.dtype), window_strides=(1,),
                                    padding=[(pad, pad)], dimension_numbers=('NWC', 'WIO', 'NWC'),
                                    feature_group_count=x.shape[-1])


def chunk_gated_delta(q, k, v, g, beta, s0):
    b, t, h, _ = q.shape
    dv = v.shape[-1]
    n = t // GDN_CHUNK
    f32 = jnp.float32

    def chunks(a):
        a = a.astype(f32).reshape((b, n, GDN_CHUNK, h) + a.shape[3:])
        return jnp.moveaxis(jnp.moveaxis(a, 1, 0), 3, 2)

    qc, kc, vc, bc = chunks(q), chunks(k), chunks(v), chunks(beta)
    gc = jnp.cumsum(chunks(g), axis=-1)
    idx = jnp.arange(GDN_CHUNK)
    causal = idx[:, None] >= idx[None, :]
    strict = idx[:, None] > idx[None, :]
    decay = jnp.exp(jnp.where(causal, gc[..., :, None] - gc[..., None, :], -jnp.inf))
    kb = kc * bc[..., None]
    lower = jnp.where(strict, jnp.einsum('nbhcd,nbhsd->nbhcs', kb, kc) * decay, 0.0)
    tri = jnp.eye(GDN_CHUNK, dtype=f32) + lower
    solve = functools.partial(lax.linalg.triangular_solve, left_side=True, lower=True, unit_diagonal=True)
    u = solve(tri, vc * bc[..., None])
    w = solve(tri, kb * jnp.exp(gc)[..., None])
    g_last = gc[..., -1]
    q_dec = qc * jnp.exp(gc)[..., None]
    k_dec = kc * jnp.exp(g_last[..., None] - gc)[..., None]
    intra = jnp.einsum('nbhcd,nbhsd->nbhcs', qc, kc) * decay

    def step(state, xs):
        q_i, k_i, u_i, w_i, a_i, gl_i = xs
        v_new = u_i - jnp.einsum('bhcd,bhde->bhce', w_i, state)
        o_i = jnp.einsum('bhcd,bhde->bhce', q_i, state) + jnp.einsum('bhcs,bhse->bhce', a_i, v_new)
        state = state * jnp.exp(gl_i)[..., None, None] + jnp.einsum('bhcd,bhce->bhde', k_i, v_new)
        return state, o_i

    s_final, o = lax.scan(step, s0.astype(f32), (q_dec, k_dec, u, w, intra, g_last))
    o = jnp.moveaxis(jnp.moveaxis(o, 2, 3), 0, 1).reshape(b, t, h, dv)
    return o.astype(q.dtype), s_final.astype(v.dtype)


def gdn_mixer(h, s0, w_in, conv_w, w_ab, a_log, dt_bias, norm_g, w_out):
    b, t, _ = h.shape
    proj = h @ w_in
    qkv = jax.nn.silu(centred_depthwise_conv(proj[..., :2 * GDN_KDIM + GDN_VDIM], conv_w))
    z = proj[..., 2 * GDN_KDIM + GDN_VDIM:].reshape(b, t, GDN_HEADS, GDN_DV)
    q = l2_normalize(qkv[..., :GDN_KDIM].reshape(b, t, GDN_HEADS, GDN_DK)) * (GDN_DK ** -0.5)
    k = l2_normalize(qkv[..., GDN_KDIM:2 * GDN_KDIM].reshape(b, t, GDN_HEADS, GDN_DK))
    v = qkv[..., 2 * GDN_KDIM:].reshape(b, t, GDN_HEADS, GDN_DV)
    ab = (h @ w_ab).astype(jnp.float32).reshape(b, t, 2, 2, GDN_HEADS)
    g = -jnp.exp(a_log.astype(jnp.float32)) * jax.nn.softplus(ab[:, :, :, 0] + dt_bias.astype(jnp.float32))
    beta = jax.nn.sigmoid(ab[:, :, :, 1])
    o_f, s_f = chunk_gated_delta(q, k, v, g[:, :, 0], beta[:, :, 0], s0[:, 0])
    flip = lambda a: jnp.flip(a, axis=1)
    o_b, s_b = chunk_gated_delta(flip(q), flip(k), flip(v), flip(g[:, :, 1]), flip(beta[:, :, 1]), s0[:, 1])
    o = rms_norm(o_f + flip(o_b), norm_g) * jax.nn.silu(z)
    return o.reshape(b, t, GDN_VDIM) @ w_out, jnp.stack([s_f, s_b], axis=1)


def diff_mixer(h, layer_idx, cache, w_in, lam_vecs, subln_g, w_out):
    b, t, _ = h.shape
    nq = DIFF_HEADS * 2 * DIFF_DH
    proj = h @ w_in
    q = proj[..., :nq].reshape(b, t, DIFF_HEADS, 2, DIFF_DH)
    k = proj[..., nq:2 * nq].reshape(b, t, DIFF_HEADS, 2, DIFF_DH)
    v = proj[..., 2 * nq:].reshape(b, t, DIFF_HEADS, DIFF_DV)
    lam_init = 0.8 - 0.6 * math.exp(-0.3 * layer_idx)
    lv = lam_vecs.astype(jnp.float32)
    lam = jnp.exp(jnp.sum(lv[0] * lv[1])) - jnp.exp(jnp.sum(lv[2] * lv[3])) + lam_init
    if cache is None:
        keys, vals = k, v
        new_cache = (k.reshape(b, t, DIFF_HEADS, 2 * DIFF_DH), v)
    else:
        cos, sin = axial_rope(t, DIFF_DH)
        q, k = apply_rope(q, cos, sin), apply_rope(k, cos, sin)
        k_ctx, v_ctx = cache
        keys = jnp.concatenate([k, k_ctx.reshape(b, -1, DIFF_HEADS, 2, DIFF_DH)], axis=1)
        vals = jnp.concatenate([v, v_ctx], axis=1)
        new_cache = None
    scale = DIFF_DH ** -0.5

    def block(args):
        _, qb = args
        s = jnp.einsum('bqhmd,bkhmd->bhmqk', qb, keys).astype(jnp.float32) * scale
        p = jax.nn.softmax(s, axis=-1)
        attn = p[:, :, 0] - lam * p[:, :, 1]
        return jnp.einsum('bhqk,bkhe->bqhe', attn.astype(vals.dtype), vals)

    o = rms_norm(sweep_query_blocks(block, q), subln_g) * (1.0 - lam_init)
    return o.reshape(b, t, DIFF_HEADS * DIFF_DV) @ w_out, new_cache


def gqa_mixer(h, cache, w_in, q_norm_g, k_norm_g, w_out):
    b, t, _ = h.shape
    nq, nkv = GQA_HEADS * GQA_HD, GQA_KV * GQA_HD
    proj = h @ w_in
    q = rms_norm(proj[..., :nq].reshape(b, t, GQA_KV, GQA_GROUP, GQA_HD), q_norm_g)
    k = rms_norm(proj[..., nq:nq + nkv].reshape(b, t, GQA_KV, GQA_HD), k_norm_g)
    v = proj[..., nq + nkv:].reshape(b, t, GQA_KV, GQA_HD)
    if cache is None:
        keys, vals, new_cache = k, v, (k, v)
    else:
        cos, sin = axial_rope(t, GQA_HD)
        q, k = apply_rope(q, cos, sin), apply_rope(k, cos, sin)
        keys = jnp.concatenate([k, cache[0]], axis=1)
        vals = jnp.concatenate([v, cache[1]], axis=1)
        new_cache = None
    scale = GQA_HD ** -0.5

    def block(args):
        _, qb = args
        s = jnp.einsum('bqngd,bknd->bngqk', qb, keys).astype(jnp.float32) * scale
        p = jax.nn.softmax(s, axis=-1).astype(vals.dtype)
        return jnp.einsum('bngqk,bknd->bqngd', p, vals)

    o = sweep_query_blocks(block, q)
    return o.reshape(b, t, nq) @ w_out, new_cache


def swa_mixer(h, cache, w_in, sinks, w_out):
    b, t, _ = h.shape
    nq, nkv = SWA_HEADS * SWA_HD, SWA_KV * SWA_HD
    proj = h @ w_in
    q = proj[..., :nq].reshape(b, t, SWA_KV, SWA_GROUP, SWA_HD)
    k = proj[..., nq:nq + nkv].reshape(b, t, SWA_KV, SWA_HD)
    v = proj[..., nq + nkv:].reshape(b, t, SWA_KV, SWA_HD)
    sink = sinks.astype(jnp.float32).reshape(SWA_KV, SWA_GROUP)[None, :, :, None, None]
    sink_col = lambda s: jnp.broadcast_to(sink, s.shape[:-1] + (1,))
    scale = SWA_HD ** -0.5
    if cache is None:
        def block(args):
            _, qb = args
            s = jnp.einsum('bqngd,bknd->bngqk', qb, k).astype(jnp.float32) * scale
            p = jax.nn.softmax(jnp.concatenate([s, sink_col(s)], axis=-1), axis=-1)[..., :-1]
            return jnp.einsum('bngqk,bknd->bqngd', p.astype(v.dtype), v)
        o = sweep_query_blocks(block, q)
        return o.reshape(b, t, nq) @ w_out, (k, v)
    cos, sin = axial_rope(t, SWA_HD)
    q, k = apply_rope(q, cos, sin), apply_rope(k, cos, sin)
    k_ctx, v_ctx = cache
    pad = ((0, 0), (WINDOW, WINDOW), (0, 0), (0, 0))
    kp, vp = jnp.pad(k, pad), jnp.pad(v, pad)
    band = Q_BLOCK + 2 * WINDOW
    qi = jnp.arange(Q_BLOCK)[:, None]
    kj = jnp.arange(band)[None, :]
    in_window = (kj >= qi) & (kj <= qi + 2 * WINDOW)

    def block(args):
        i, qb = args
        start = i * Q_BLOCK
        kb = lax.dynamic_slice_in_dim(kp, start, band, axis=1)
        vb = lax.dynamic_slice_in_dim(vp, start, band, axis=1)
        kpos = start - WINDOW + kj
        valid = in_window & (kpos >= 0) & (kpos < t)
        s_band = jnp.where(valid, jnp.einsum('bqngd,bknd->bngqk', qb, kb).astype(jnp.float32) * scale, -jnp.inf)
        s_ctx = jnp.einsum('bqngd,bknd->bngqk', qb, k_ctx).astype(jnp.float32) * scale
        p = jax.nn.softmax(jnp.concatenate([s_band, s_ctx, sink_col(s_band)], axis=-1), axis=-1).astype(v.dtype)
        return (jnp.einsum('bngqk,bknd->bqngd', p[..., :band], vb)
                + jnp.einsum('bngqk,bknd->bqngd', p[..., band:-1], v_ctx))

    o = sweep_query_blocks(block, q)
    return o.reshape(b, t, nq) @ w_out, None


def expert_choice_ffn(h, w_router, w_gate, w_up, w_down):
    b, t, d = h.shape
    n = b * t
    capacity = EC_CAPACITY_FACTOR * n // N_EXPERTS
    xt = h.reshape(n, d)
    affinity = jax.nn.softmax((xt @ w_router).astype(jnp.float32), axis=-1)
    gate, idx = lax.top_k(affinity.T, capacity)
    xe = xt[idx]
    hid = jax.nn.silu(jnp.einsum('ecd,edf->ecf', xe, w_gate)) * jnp.einsum('ecd,edf->ecf', xe, w_up)
    ye = jnp.einsum('ecf,efd->ecd', hid, w_down) * gate[..., None].astype(h.dtype)
    return jnp.zeros_like(xt).at[idx.reshape(-1)].add(ye.reshape(-1, d)).reshape(b, t, d)


def run_trunk(x, cond, caches, p):
    new = {'gdn': [], 'diff_k': [], 'diff_v': [], 'gqa_k': [], 'gqa_v': [], 'swa_k': [], 'swa_v': []}
    for layer in range(DEPTH):
        kind, slot = layer % N_MIXERS, layer // N_MIXERS
        mod = (jax.nn.silu(cond) @ p['mod_w'][layer] + p['mod_b'][layer])[:, None, :]
        sh1, sc1, g1, sh2, sc2, g2 = jnp.split(mod, 6, axis=-1)
        h = rms_norm(x, p['norm_mix_g'][layer]) * (1.0 + sc1) + sh1
        if kind == 0:
            if caches is None:
                s0 = jnp.zeros((x.shape[0], 2, GDN_HEADS, GDN_DK, GDN_DV), x.dtype)
            else:
                s0 = caches['state_gdn'][:, slot]
            out, s_fin = gdn_mixer(h, s0, p['gdn_w_in'][slot], p['gdn_conv_w'][slot], p['gdn_w_ab'][slot],
                                   p['gdn_a_log'][slot], p['gdn_dt_bias'][slot], p['gdn_norm_g'][slot],
                                   p['gdn_w_out'][slot])
            if caches is None:
                new['gdn'].append(s_fin)
        elif kind == 1:
            cache = None if caches is None else (caches['cache_diff_k'][:, slot], caches['cache_diff_v'][:, slot])
            out, kv = diff_mixer(h, layer, cache, p['diff_w_in'][slot], p['diff_lambda'][slot],
                                 p['diff_subln_g'][slot], p['diff_w_out'][slot])
            if kv is not None:
                new['diff_k'].append(kv[0])
                new['diff_v'].append(kv[1])
        elif kind == 2:
            cache = None if caches is None else (caches['cache_gqa_k'][:, slot], caches['cache_gqa_v'][:, slot])
            out, kv = gqa_mixer(h, cache, p['gqa_w_in'][slot], p['gqa_q_norm_g'][slot],
                                p['gqa_k_norm_g'][slot], p['gqa_w_out'][slot])
            if kv is not None:
                new['gqa_k'].append(kv[0])
                new['gqa_v'].append(kv[1])
        else:
            cache = None if caches is None else (caches['cache_swa_k'][:, slot], caches['cache_swa_v'][:, slot])
            out, kv = swa_mixer(h, cache, p['swa_w_in'][slot], p['swa_sinks'][slot], p['swa_w_out'][slot])
            if kv is not None:
                new['swa_k'].append(kv[0])
                new['swa_v'].append(kv[1])
        x = x + g1 * out
        h = rms_norm(x, p['norm_ffn_g'][layer]) * (1.0 + sc2) + sh2
        x = x + g2 * expert_choice_ffn(h, p['moe_router'][layer], p['moe_w_gate'][layer],
                                       p['moe_w_up'][layer], p['moe_w_down'][layer])
    return rms_norm(x, p['final_g']), new


def setup_inputs(seed: int = 0) -> dict:
    key = jax.random.key(seed)
    ks = iter(jax.random.split(key, 64))
    f32 = jnp.float32
    D = D_MODEL

    def normal(shape, scale):
        return scale * jax.random.normal(next(ks), shape, f32)

    def gain(shape):
        return 1.0 + normal(shape, 0.05)

    dt = jnp.exp(jax.random.uniform(next(ks), (N_LAYERS_A, 2, GDN_HEADS), f32, math.log(1e-3), math.log(1e-1)))
    return {
        'x_prompt': normal((BATCH, SEQ, D), 1.0),
        'x_sample': normal((DEC_BATCH, DEC_SEQ, D), 1.0),
        'state_gdn': normal((DEC_BATCH, N_LAYERS_A, 2, GDN_HEADS, GDN_DK, GDN_DV), 0.1),
        'cache_diff_k': normal((DEC_BATCH, N_LAYERS_B, PAST_LEN, DIFF_HEADS, 2 * DIFF_DH), 1.0),
        'cache_diff_v': normal((DEC_BATCH, N_LAYERS_B, PAST_LEN, DIFF_HEADS, DIFF_DV), 1.0),
        'cache_gqa_k': normal((DEC_BATCH, N_LAYERS_C, PAST_LEN, GQA_KV, GQA_HD), 1.0),
        'cache_gqa_v': normal((DEC_BATCH, N_LAYERS_C, PAST_LEN, GQA_KV, GQA_HD), 1.0),
        'cache_swa_k': normal((DEC_BATCH, N_LAYERS_D, PAST_LEN, SWA_KV, SWA_HD), 1.0),
        'cache_swa_v': normal((DEC_BATCH, N_LAYERS_D, PAST_LEN, SWA_KV, SWA_HD), 1.0),
        'c': normal((DEC_BATCH, D), 1.0),
        'c_ctx': normal((D,), 1.0),
        'mod_w': normal((DEPTH, D, 6 * D), 0.5 * D ** -0.5),
        'mod_b': normal((DEPTH, 6 * D), 0.02),
        'norm_mix_g': gain((DEPTH, D)),
        'norm_ffn_g': gain((DEPTH, D)),
        'final_g': gain((D,)),
        'gdn_w_in': normal((N_LAYERS_A, D, 2 * GDN_KDIM + 2 * GDN_VDIM), D ** -0.5),
        'gdn_conv_w': normal((N_LAYERS_A, CONV_W, 2 * GDN_KDIM + GDN_VDIM), CONV_W ** -0.5),
        'gdn_w_ab': normal((N_LAYERS_A, D, 4 * GDN_HEADS), D ** -0.5),
        'gdn_a_log': jnp.log(jax.random.uniform(next(ks), (N_LAYERS_A, 2, GDN_HEADS), f32, 1.0, 16.0)),
        'gdn_dt_bias': dt + jnp.log(-jnp.expm1(-dt)),
        'gdn_norm_g': gain((N_LAYERS_A, GDN_DV)),
        'gdn_w_out': normal((N_LAYERS_A, GDN_VDIM, D), GDN_VDIM ** -0.5),
        'diff_w_in': normal((N_LAYERS_B, D, 3 * DIFF_HEADS * DIFF_DV), D ** -0.5),
        'diff_lambda': normal((N_LAYERS_B, 4, DIFF_DH), 0.1),
        'diff_subln_g': gain((N_LAYERS_B, DIFF_DV)),
        'diff_w_out': normal((N_LAYERS_B, DIFF_HEADS * DIFF_DV, D), (DIFF_HEADS * DIFF_DV) ** -0.5),
        'gqa_w_in': normal((N_LAYERS_C, D, (GQA_HEADS + 2 * GQA_KV) * GQA_HD), D ** -0.5),
        'gqa_q_norm_g': gain((N_LAYERS_C, GQA_HD)),
        'gqa_k_norm_g': gain((N_LAYERS_C, GQA_HD)),
        'gqa_w_out': normal((N_LAYERS_C, GQA_HEADS * GQA_HD, D), (GQA_HEADS * GQA_HD) ** -0.5),
        'swa_w_in': normal((N_LAYERS_D, D, (SWA_HEADS + 2 * SWA_KV) * SWA_HD), D ** -0.5),
        'swa_sinks': normal((N_LAYERS_D, SWA_HEADS), 1.0),
        'swa_w_out': normal((N_LAYERS_D, SWA_HEADS * SWA_HD, D), (SWA_HEADS * SWA_HD) ** -0.5),
        'moe_router': normal((DEPTH, D, N_EXPERTS), D ** -0.5),
        'moe_w_gate': normal((DEPTH, N_EXPERTS, D, EXPERT_FF), D ** -0.5),
        'moe_w_up': normal((DEPTH, N_EXPERTS, D, EXPERT_FF), D ** -0.5),
        'moe_w_down': normal((DEPTH, N_EXPERTS, EXPERT_FF, D), EXPERT_FF ** -0.5),
    }


def reference(x_prompt, x_sample, state_gdn, cache_diff_k, cache_diff_v, cache_gqa_k, cache_gqa_v,
              cache_swa_k, cache_swa_v, c, c_ctx, mod_w, mod_b, norm_mix_g, norm_ffn_g, final_g,
              gdn_w_in, gdn_conv_w, gdn_w_ab, gdn_a_log, gdn_dt_bias, gdn_norm_g, gdn_w_out,
              diff_w_in, diff_lambda, diff_subln_g, diff_w_out,
              gqa_w_in, gqa_q_norm_g, gqa_k_norm_g, gqa_w_out,
              swa_w_in, swa_sinks, swa_w_out,
              moe_router, moe_w_gate, moe_w_up, moe_w_down):
    p = dict(mod_w=mod_w, mod_b=mod_b, norm_mix_g=norm_mix_g, norm_ffn_g=norm_ffn_g, final_g=final_g,
             gdn_w_in=gdn_w_in, gdn_conv_w=gdn_conv_w, gdn_w_ab=gdn_w_ab, gdn_a_log=gdn_a_log,
             gdn_dt_bias=gdn_dt_bias, gdn_norm_g=gdn_norm_g, gdn_w_out=gdn_w_out,
             diff_w_in=diff_w_in, diff_lambda=diff_lambda, diff_subln_g=diff_subln_g, diff_w_out=diff_w_out,
             gqa_w_in=gqa_w_in, gqa_q_norm_g=gqa_q_norm_g, gqa_k_norm_g=gqa_k_norm_g, gqa_w_out=gqa_w_out,
             swa_w_in=swa_w_in, swa_sinks=swa_sinks, swa_w_out=swa_w_out,
             moe_router=moe_router, moe_w_gate=moe_w_gate, moe_w_up=moe_w_up, moe_w_down=moe_w_down)
    y_prompt, st = run_trunk(x_prompt, c_ctx[None, :], None, p)
    caches = dict(state_gdn=state_gdn, cache_diff_k=cache_diff_k, cache_diff_v=cache_diff_v,
                  cache_gqa_k=cache_gqa_k, cache_gqa_v=cache_gqa_v,
                  cache_swa_k=cache_swa_k, cache_swa_v=cache_swa_v)
    y_sample, _ = run_trunk(x_sample, c, caches, p)
    stack = lambda xs: jnp.stack(xs, axis=1)
    return (y_prompt, y_sample, stack(st['gdn']), stack(st['diff_k']), stack(st['diff_v']),
            stack(st['gqa_k']), stack(st['gqa_v']), stack(st['swa_k']), stack(st['swa_v']))
```

```python
import functools
import math

import jax
import jax.numpy as jnp
from jax import lax
from jax.experimental import pallas as pl
from jax.experimental.pallas import tpu as pltpu

F32 = jnp.float32
BF16 = jnp.bfloat16

D_MODEL = 1024
GRID_W = 64
ROPE_THETA = 10000.0
NORM_EPS = 1e-6
LANES = 128
GDN_HEADS = 8
GDN_DK = 128
GDN_CHUNK = 64
GDN_SUPER = 256
DIFF_HEADS = 8
DIFF_DH = 64
GQA_HEADS = 8
GQA_KV = 2
GQA_HD = 128
SWA_HEADS = 16
SWA_KV = 4
SWA_HD = 64
WINDOW = 128
N_EXPERTS = 16
EC_CAPACITY_FACTOR = 2
VMEM_LIMIT_BYTES = 56 * 1024 * 1024


def _params(*sem):
    return pltpu.CompilerParams(dimension_semantics=sem, vmem_limit_bytes=VMEM_LIMIT_BYTES)


def _silu(x):
    return x / (1.0 + jnp.exp(-x))


def _rms(x, g):
    return x * lax.rsqrt(jnp.mean(x * x, axis=-1, keepdims=True) + NORM_EPS) * g


def _dot(a, b):
    return jnp.dot(a, b, preferred_element_type=F32)


def _dot_nt(a, b):
    return lax.dot_general(a, b, (((1,), (1,)), ((), ())), preferred_element_type=F32)


def _split_bf16(x, parts):
    out = []
    for _ in range(parts - 1):
        hi = x.astype(BF16)
        out.append(hi)
        x = x - hi.astype(F32)
    out.append(x.astype(BF16))
    return out


def _mod_kernel(c_ref, w_ref, b_ref, o_ref):
    a = _silu(c_ref[...]).astype(BF16)
    o_ref[0] = _dot(a, w_ref[0].astype(BF16)) + b_ref[0]


def _mod_proj(cond, mod_w, mod_b):
    depth, _, width = mod_w.shape
    rows = cond.shape[0]
    tn = 1536
    return pl.pallas_call(
        _mod_kernel,
        grid=(depth, width // tn),
        in_specs=[pl.BlockSpec((rows, D_MODEL), lambda l, j: (0, 0)),
                  pl.BlockSpec((1, D_MODEL, tn), lambda l, j: (l, 0, j)),
                  pl.BlockSpec((1, 1, tn), lambda l, j: (l, 0, j))],
        out_specs=pl.BlockSpec((1, rows, tn), lambda l, j: (l, 0, j)),
        out_shape=jax.ShapeDtypeStruct((depth, rows, width), F32),
        compiler_params=_params("parallel", "parallel"),
        name="mod_proj",
    )(cond, mod_w, mod_b.reshape(depth, 1, width))


def _norm_mod(x, g, mod_ref, sh, sc):
    return _rms(x, g) * (1.0 + mod_ref[0, sc:sc + 1, :]) + mod_ref[0, sh:sh + 1, :]


def _nmm_kernel(x_ref, g_ref, mod_ref, w_ref, o_ref, *, sh, sc, chunk):
    h = _norm_mod(x_ref[...], g_ref[...], mod_ref, sh, sc).astype(BF16)
    width = w_ref.shape[1]
    for c0 in range(0, width, chunk):
        c1 = min(c0 + chunk, width)
        o_ref[:, c0:c1] = _dot(h, w_ref[:, c0:c1])


def _nmm(x, g, mod, w, rows_per_mod, sh, sc, tm=256):
    m = x.shape[0]
    width = w.shape[1]
    return pl.pallas_call(
        functools.partial(_nmm_kernel, sh=sh, sc=sc, chunk=512),
        grid=(m // tm,),
        in_specs=[pl.BlockSpec((tm, D_MODEL), lambda i: (i, 0)),
                  pl.BlockSpec((1, D_MODEL), lambda i: (0, 0)),
                  pl.BlockSpec((1, 6, D_MODEL), lambda i: ((i * tm) // rows_per_mod, 0, 0)),
                  pl.BlockSpec((D_MODEL, width), lambda i: (0, 0))],
        out_specs=pl.BlockSpec((tm, width), lambda i: (i, 0)),
        out_shape=jax.ShapeDtypeStruct((m, width), F32),
        compiler_params=_params("parallel"),
        name="norm_mod_matmul",
    )(x, g.reshape(1, D_MODEL), mod, w)


def _mmr_kernel(a_ref, w_ref, r_ref, mod_ref, o_ref, *, gi):
    acc = _dot(a_ref[...].astype(BF16), w_ref[...])
    o_ref[...] = r_ref[...] + mod_ref[0, gi:gi + 1, :] * acc


def _mm_resid(a, w, res, mod, rows_per_mod, gi, tm=512):
    m = a.shape[0]
    return pl.pallas_call(
        functools.partial(_mmr_kernel, gi=gi),
        grid=(m // tm,),
        in_specs=[pl.BlockSpec((tm, D_MODEL), lambda i: (i, 0)),
                  pl.BlockSpec((D_MODEL, D_MODEL), lambda i: (0, 0)),
                  pl.BlockSpec((tm, D_MODEL), lambda i: (i, 0)),
                  pl.BlockSpec((1, 6, D_MODEL), lambda i: ((i * tm) // rows_per_mod, 0, 0))],
        out_specs=pl.BlockSpec((tm, D_MODEL), lambda i: (i, 0)),
        out_shape=jax.ShapeDtypeStruct((m, D_MODEL), F32),
        compiler_params=_params("parallel"),
        name="out_proj_resid",
    )(a, w, res, mod)


def _rg_kernel(x_ref, y_ref, mod_ref, fg_ref, o_ref, *, gi, final):
    x = x_ref[...] + mod_ref[0, gi:gi + 1, :] * y_ref[...]
    if final:
        x = _rms(x, fg_ref[...])
    o_ref[...] = x


def _resid_gate(x, y, mod, rows_per_mod, gi, final_g, final, tm=512):
    m = x.shape[0]
    return pl.pallas_call(
        functools.partial(_rg_kernel, gi=gi, final=final),
        grid=(m // tm,),
        in_specs=[pl.BlockSpec((tm, D_MODEL), lambda i: (i, 0)),
                  pl.BlockSpec((tm, D_MODEL), lambda i: (i, 0)),
                  pl.BlockSpec((1, 6, D_MODEL), lambda i: ((i * tm) // rows_per_mod, 0, 0)),
                  pl.BlockSpec((1, D_MODEL), lambda i: (0, 0))],
        out_specs=pl.BlockSpec((tm, D_MODEL), lambda i: (i, 0)),
        out_shape=jax.ShapeDtypeStruct((m, D_MODEL), F32),
        compiler_params=_params("parallel"),
        name="resid_gate",
    )(x, y, mod, final_g.reshape(1, D_MODEL))


def _rope_tables(n_tokens, dim):
    rows = n_tokens // GRID_W
    row = jnp.broadcast_to(jnp.arange(rows, dtype=F32)[:, None], (rows, GRID_W)).reshape(-1)
    col = jnp.broadcast_to(jnp.arange(GRID_W, dtype=F32)[None, :], (rows, GRID_W)).reshape(-1)
    n_freq = dim // 4
    inv = ROPE_THETA ** (-jnp.arange(n_freq, dtype=F32) / n_freq)
    ang = jnp.concatenate([row[:, None] * inv, col[:, None] * inv], axis=-1)
    cos, sin = jnp.cos(ang), jnp.sin(ang)
    zero = jnp.zeros_like(sin)
    reps = LANES // dim
    cos_t = jnp.tile(jnp.concatenate([cos, cos], axis=-1), (1, reps))
    sin_a = jnp.tile(jnp.concatenate([-sin, zero], axis=-1), (1, reps))
    sin_b = jnp.tile(jnp.concatenate([zero, sin], axis=-1), (1, reps))
    return cos_t, sin_a, sin_b


def _rope(x, cos, sin_a, sin_b, dim):
    half = dim // 2
    return x * cos + pltpu.roll(x, LANES - half, 1) * sin_a + pltpu.roll(x, half, 1) * sin_b


def _prep_kernel(*refs, kind, rope):
    if kind == "gqa":
        p_ref, cos_ref, sa_ref, sb_ref, qg_ref, kg_ref, q_out, k_out, v_out, kf_out = refs
    else:
        p_ref, cos_ref, sa_ref, sb_ref, q_out, k_out, v_out = refs
    cos, sa, sb = cos_ref[...], sa_ref[...], sb_ref[...]

    def blk(j):
        return p_ref[:, j * LANES:(j + 1) * LANES]

    if kind == "gqa":
        scale = GQA_HD ** -0.5
        for j in range(GQA_HEADS):
            y = _rms(blk(j), qg_ref[...])
            if rope:
                y = _rope(y, cos, sa, sb, GQA_HD)
            q_out[:, j * LANES:(j + 1) * LANES] = (y * scale).astype(BF16)
        for j in range(GQA_KV):
            y = _rms(blk(GQA_HEADS + j), kg_ref[...])
            kf_out[:, j * LANES:(j + 1) * LANES] = y
            if rope:
                y = _rope(y, cos, sa, sb, GQA_HD)
            k_out[:, j * LANES:(j + 1) * LANES] = y.astype(BF16)
        v0 = (GQA_HEADS + GQA_KV) * LANES
        v_out[...] = p_ref[:, v0:v0 + GQA_KV * LANES].astype(BF16)
    elif kind == "diff":
        scale = DIFF_DH ** -0.5
        for j in range(DIFF_HEADS):
            y = blk(j)
            if rope:
                y = _rope(y, cos, sa, sb, DIFF_DH)
            q_out[:, j * LANES:(j + 1) * LANES] = (y * scale).astype(BF16)
            y = blk(DIFF_HEADS + j)
            if rope:
                y = _rope(y, cos, sa, sb, DIFF_DH)
            k_out[:, j * LANES:(j + 1) * LANES] = y.astype(BF16)
        v_out[...] = p_ref[:, 2 * DIFF_HEADS * LANES:3 * DIFF_HEADS * LANES].astype(BF16)
    else:
        scale = SWA_HD ** -0.5
        low = lax.broadcasted_iota(jnp.int32, cos.shape, 1) < SWA_HD

        def pad_pair(y, out, j):
            out[:, (2 * j) * LANES:(2 * j + 1) * LANES] = jnp.where(low, y, 0.0).astype(BF16)
            out[:, (2 * j + 1) * LANES:(2 * j + 2) * LANES] = jnp.where(
                low, pltpu.roll(y, SWA_HD, 1), 0.0).astype(BF16)

        nq = SWA_HEADS // 2
        for j in range(nq):
            y = blk(j)
            if rope:
                y = _rope(y, cos, sa, sb, SWA_HD)
            pad_pair(y * scale, q_out, j)
        for j in range(SWA_KV // 2):
            y = blk(nq + j)
            if rope:
                y = _rope(y, cos, sa, sb, SWA_HD)
            pad_pair(y, k_out, j)
        v0 = (nq + SWA_KV // 2) * LANES
        v_out[...] = p_ref[:, v0:v0 + (SWA_KV // 2) * LANES].astype(BF16)


def _attn_prep(proj, kind, seq, rope, gains=None, tm=256):
    m, width = proj.shape
    dim = GQA_HD if kind == "gqa" else DIFF_DH
    cos, sa, sb = _rope_tables(seq, dim)
    nt = seq // tm
    if kind == "gqa":
        wq, wk, wv = GQA_HEADS * LANES, GQA_KV * LANES, GQA_KV * LANES
    elif kind == "diff":
        wq, wk, wv = DIFF_HEADS * LANES, DIFF_HEADS * LANES, DIFF_HEADS * LANES
    else:
        wq, wk, wv = SWA_HEADS * LANES, SWA_KV * LANES, (SWA_KV // 2) * LANES
    tab = pl.BlockSpec((tm, LANES), lambda i: (i % nt, 0))
    in_specs = [pl.BlockSpec((tm, width), lambda i: (i, 0)), tab, tab, tab]
    args = [proj, cos, sa, sb]
    out_shapes = [jax.ShapeDtypeStruct((m, wq), BF16), jax.ShapeDtypeStruct((m, wk), BF16),
                  jax.ShapeDtypeStruct((m, wv), BF16)]
    out_specs = [pl.BlockSpec((tm, wq), lambda i: (i, 0)), pl.BlockSpec((tm, wk), lambda i: (i, 0)),
                 pl.BlockSpec((tm, wv), lambda i: (i, 0))]
    if kind == "gqa":
        gspec = pl.BlockSpec((1, LANES), lambda i: (0, 0))
        in_specs += [gspec, gspec]
        args += [gains[0].reshape(1, LANES), gains[1].reshape(1, LANES)]
        out_shapes.append(jax.ShapeDtypeStruct((m, wk), F32))
        out_specs.append(pl.BlockSpec((tm, wk), lambda i: (i, 0)))
    return pl.pallas_call(
        functools.partial(_prep_kernel, kind=kind, rope=rope),
        grid=(m // tm,),
        in_specs=in_specs,
        out_specs=out_specs,
        out_shape=out_shapes,
        compiler_params=_params("parallel"),
        name="attn_prep_" + kind,
    )(*args)


def _pick_half(o_a, o_b, upper):
    low = lax.broadcasted_iota(jnp.int32, o_a.shape, 1) < SWA_HD
    if upper:
        return jnp.where(low, pltpu.roll(o_a, SWA_HD, 1), o_b)
    return jnp.where(low, o_a, pltpu.roll(o_b, SWA_HD, 1))


def _attn_kernel(*refs, mode, lam_init):
    if mode == "gqa":
        q_ref, k_ref, v_ref, o_ref = refs
    elif mode == "diff":
        q_ref, k_ref, v_ref, lam_ref, g_ref, o_ref = refs
    else:
        q_ref, k_ref, v_ref, sink_ref, o_ref = refs
    v = v_ref[...]
    if mode == "gqa":
        k = k_ref[...]
        for j in range(GQA_HEADS // GQA_KV):
            s = _dot_nt(q_ref[:, j * LANES:(j + 1) * LANES], k)
            e = jnp.exp(s - jnp.max(s, axis=-1, keepdims=True))
            o = _dot(e.astype(BF16), v) / jnp.sum(e, axis=-1, keepdims=True)
            o_ref[:, j * LANES:(j + 1) * LANES] = o.astype(BF16)
    elif mode == "diff":
        k = k_ref[...]
        q = q_ref[...]
        low = lax.broadcasted_iota(jnp.int32, q.shape, 1) < DIFF_DH
        lv = lam_ref[...]
        lam = (jnp.exp(jnp.sum(lv[0:1] * lv[1:2], axis=-1, keepdims=True))
               - jnp.exp(jnp.sum(lv[2:3] * lv[3:4], axis=-1, keepdims=True)) + lam_init)
        s0 = _dot_nt(jnp.where(low, q, 0.0).astype(BF16), k)
        s1 = _dot_nt(jnp.where(low, 0.0, q).astype(BF16), k)
        e0 = jnp.exp(s0 - jnp.max(s0, axis=-1, keepdims=True))
        e1 = jnp.exp(s1 - jnp.max(s1, axis=-1, keepdims=True))
        a = e0 * (1.0 / jnp.sum(e0, axis=-1, keepdims=True)) - e1 * (lam / jnp.sum(e1, axis=-1, keepdims=True))
        o = _dot(a.astype(BF16), v)
        o_ref[...] = (_rms(o, g_ref[...]) * (1.0 - lam_init)).astype(BF16)
    else:
        heads = SWA_HEADS // (SWA_KV // 2)
        group = SWA_HEADS // SWA_KV
        outs = []
        for j in range(heads):
            a = j // group
            s = _dot_nt(q_ref[:, j * LANES:(j + 1) * LANES], k_ref[:, a * LANES:(a + 1) * LANES])
            sink = sink_ref[0, :, j:j + 1]
            mx = jnp.maximum(jnp.max(s, axis=-1, keepdims=True), sink)
            e = jnp.exp(s - mx)
            den = jnp.sum(e, axis=-1, keepdims=True) + jnp.exp(sink - mx)
            outs.append(_dot(e.astype(BF16), v) / den)
        for jj in range(heads // 2):
            a = (2 * jj) // group
            o_ref[:, jj * LANES:(jj + 1) * LANES] = _pick_half(outs[2 * jj], outs[2 * jj + 1], a == 1).astype(BF16)


def _attention(q, k, v, mode, batch, seq, n_keys, extra=(), lam_init=0.0, tq=256):
    nq = seq // tq
    if mode == "gqa":
        groups, wq, wk, wv, wo = GQA_KV, (GQA_HEADS // GQA_KV) * LANES, LANES, LANES, (GQA_HEADS // GQA_KV) * LANES
    elif mode == "diff":
        groups, wq, wk, wv, wo = DIFF_HEADS, LANES, LANES, LANES, LANES
    else:
        groups, wq, wk, wv, wo = SWA_KV // 2, SWA_HEADS // 2 * LANES, 2 * LANES, LANES, 4 * LANES
    in_specs = [pl.BlockSpec((tq, wq), lambda b, g, i: (b * nq + i, g)),
                pl.BlockSpec((n_keys, wk), lambda b, g, i: (b, g)),
                pl.BlockSpec((n_keys, wv), lambda b, g, i: (b, g))]
    args = [q, k, v]
    if mode == "diff":
        lam_vecs, subln_g = extra
        in_specs += [pl.BlockSpec((4, DIFF_DH), lambda b, g, i: (0, 0)),
                     pl.BlockSpec((1, LANES), lambda b, g, i: (0, 0))]
        args += [lam_vecs, subln_g.reshape(1, LANES)]
    elif mode == "swa":
        (sinks,) = extra
        in_specs.append(pl.BlockSpec((1, 1, SWA_HEADS // 2), lambda b, g, i: (g, 0, 0)))
        args.append(sinks.reshape(2, 1, SWA_HEADS // 2))
    return pl.pallas_call(
        functools.partial(_attn_kernel, mode=mode, lam_init=lam_init),
        grid=(batch, groups, nq),
        in_specs=in_specs,
        out_specs=pl.BlockSpec((tq, wo), lambda b, g, i: (b * nq + i, g)),
        out_shape=jax.ShapeDtypeStruct((batch * seq, D_MODEL), BF16),
        compiler_params=_params("parallel", "parallel", "parallel"),
        name="attn_" + mode,
    )(*args)


def _band_kernel(q_ref, kp_ref, kc_ref, kn_ref, kx_ref, vp_ref, vc_ref, vn_ref, vx_ref, sink_ref, o_ref, *, nqb):
    i = pl.program_id(2)
    qi = lax.broadcasted_iota(jnp.int32, (WINDOW, WINDOW), 0)
    kj = lax.broadcasted_iota(jnp.int32, (WINDOW, WINDOW), 1)
    m_prev = (kj >= qi) & (i > 0)
    m_next = (kj <= qi) & (i < nqb - 1)
    heads = SWA_HEADS // (SWA_KV // 2)
    group = SWA_HEADS // SWA_KV
    outs = []
    for j in range(heads):
        a = j // group
        q = q_ref[:, j * LANES:(j + 1) * LANES]
        ks = slice(a * LANES, (a + 1) * LANES)
        s_p = jnp.where(m_prev, _dot_nt(q, kp_ref[:, ks]), -jnp.inf)
        s_c = _dot_nt(q, kc_ref[:, ks])
        s_n = jnp.where(m_next, _dot_nt(q, kn_ref[:, ks]), -jnp.inf)
        s_x = _dot_nt(q, kx_ref[:, ks])
        sink = sink_ref[0, :, j:j + 1]
        mx = jnp.maximum(jnp.maximum(jnp.max(s_p, axis=-1, keepdims=True), jnp.max(s_c, axis=-1, keepdims=True)),
                         jnp.maximum(jnp.max(s_n, axis=-1, keepdims=True), jnp.max(s_x, axis=-1, keepdims=True)))
        mx = jnp.maximum(mx, sink)
        e_p, e_c, e_n, e_x = jnp.exp(s_p - mx), jnp.exp(s_c - mx), jnp.exp(s_n - mx), jnp.exp(s_x - mx)
        den = (jnp.sum(e_p, axis=-1, keepdims=True) + jnp.sum(e_c, axis=-1, keepdims=True)
               + jnp.sum(e_n, axis=-1, keepdims=True) + jnp.sum(e_x, axis=-1, keepdims=True) + jnp.exp(sink - mx))
        o = (_dot(e_p.astype(BF16), vp_ref[...]) + _dot(e_c.astype(BF16), vc_ref[...])
             + _dot(e_n.astype(BF16), vn_ref[...]) + _dot(e_x.astype(BF16), vx_ref[...]))
        outs.append(o / den)
    for jj in range(heads // 2):
        a = (2 * jj) // group
        o_ref[:, jj * LANES:(jj + 1) * LANES] = _pick_half(outs[2 * jj], outs[2 * jj + 1], a == 1).astype(BF16)


def _band_attention(q, k, v, sinks, batch, seq, past):
    nqb = seq // WINDOW
    nkb = (seq + past) // WINDOW
    pairs = SWA_KV // 2
    xb = past // WINDOW
    assert past % WINDOW == 0 and seq % past == 0

    def band(shift, width):
        return pl.BlockSpec((WINDOW, width),
                            lambda b, g, i: (b * nkb + jnp.clip(i + shift, 0, nqb - 1), g))

    def ctx(width):
        return pl.BlockSpec((past, width), lambda b, g, i: (b * (nkb // xb) + seq // past, g))

    return pl.pallas_call(
        functools.partial(_band_kernel, nqb=nqb),
        grid=(batch, pairs, nqb),
        in_specs=[pl.BlockSpec((WINDOW, SWA_HEADS // 2 * LANES), lambda b, g, i: (b * nqb + i, g)),
                  band(-1, 2 * LANES), band(0, 2 * LANES), band(1, 2 * LANES), ctx(2 * LANES),
                  band(-1, LANES), band(0, LANES), band(1, LANES), ctx(LANES),
                  pl.BlockSpec((1, 1, SWA_HEADS // 2), lambda b, g, i: (g, 0, 0))],
        out_specs=pl.BlockSpec((WINDOW, 4 * LANES), lambda b, g, i: (b * nqb + i, g)),
        out_shape=jax.ShapeDtypeStruct((batch * seq, D_MODEL), BF16),
        compiler_params=_params("parallel", "parallel", "parallel"),
        name="attn_band",
    )(q, k, k, k, k, v, v, v, v, sinks.reshape(2, 1, SWA_HEADS // 2))


def _gdn_prep_kernel(p_ref, w_ref, o_ref):
    x = p_ref[0]
    t = x.shape[0]
    w = w_ref[...]
    row = lax.broadcasted_iota(jnp.int32, x.shape, 0)
    xm = jnp.where(row == 0, 0.0, pltpu.roll(x, 1, 0))
    xp = jnp.where(row == t - 1, 0.0, pltpu.roll(x, t - 1, 0))
    y = _silu(xm * w[0:1] + x * w[1:2] + xp * w[2:3])
    rs = lax.rsqrt(jnp.sum(y * y, axis=-1, keepdims=True) + NORM_EPS)
    j = pl.program_id(1)
    is_q = (j < GDN_HEADS).astype(F32)
    normed = (j < 2 * GDN_HEADS).astype(F32)
    scale = is_q * (GDN_DK ** -0.5) + (1.0 - is_q)
    o_ref[0] = y * (rs * (scale * normed) + (1.0 - normed))


def _gdn_prep(proj3, conv_w):
    batch, seq, _ = proj3.shape
    nblk = 3 * GDN_HEADS
    return pl.pallas_call(
        _gdn_prep_kernel,
        grid=(batch, nblk),
        in_specs=[pl.BlockSpec((1, seq, LANES), lambda b, j: (b, 0, j)),
                  pl.BlockSpec((3, LANES), lambda b, j: (0, j))],
        out_specs=pl.BlockSpec((1, seq, LANES), lambda b, j: (b, 0, j)),
        out_shape=jax.ShapeDtypeStruct((batch, seq, nblk * LANES), F32),
        compiler_params=_params("parallel", "parallel"),
        name="gdn_conv",
    )(proj3, conv_w)


def _gdn_gate_kernel(p_ref, al_ref, dtb_ref, o_ref, t_ref):
    ab = p_ref[...]
    lane = lax.broadcasted_iota(jnp.int32, ab.shape, 1)
    is_g = (lane < 4 * GDN_HEADS) & ((lane & GDN_HEADS) == 0)
    is_fwd = lane < 2 * GDN_HEADS
    x = ab + dtb_ref[...]
    softplus = jnp.maximum(x, 0.0) + jnp.log(1.0 + jnp.exp(-jnp.abs(x)))
    g = jnp.where(is_g, -jnp.exp(al_ref[...]) * softplus, 0.0)
    beta = 1.0 / (1.0 + jnp.exp(-ab))
    ri = lax.broadcasted_iota(jnp.int32, (GDN_SUPER, GDN_SUPER), 0)
    ci = lax.broadcasted_iota(jnp.int32, (GDN_SUPER, GDN_SUPER), 1)
    same = (ri // GDN_CHUNK) == (ci // GDN_CHUNK)
    lower = jnp.where(same & (ri >= ci), 1.0, 0.0).astype(BF16)
    upper = jnp.where(same & (ri <= ci), 1.0, 0.0).astype(BF16)
    parts = _split_bf16(g, 3)
    pre = _dot(lower, parts[0]) + _dot(lower, parts[1]) + _dot(lower, parts[2])
    suf = _dot(upper, parts[0]) + _dot(upper, parts[1]) + _dot(upper, parts[2])
    o_ref[...] = jnp.where(is_g, jnp.where(is_fwd, pre, suf), beta)
    t_ref[...] = pre + suf - g


def _gdn_gates(proj, a_log, dt_bias):
    m, width = proj.shape
    col_blk = (width - LANES) // LANES
    pad = jnp.zeros((2, GDN_HEADS), F32)

    def row(v):
        return jnp.pad(jnp.stack([v.astype(F32), pad], axis=1).reshape(1, 4 * GDN_HEADS),
                       ((0, 0), (0, LANES - 4 * GDN_HEADS)))

    spec = pl.BlockSpec((GDN_SUPER, LANES), lambda i: (i, 0))
    vec = pl.BlockSpec((1, LANES), lambda i: (0, 0))
    return pl.pallas_call(
        _gdn_gate_kernel,
        grid=(m // GDN_SUPER,),
        in_specs=[pl.BlockSpec((GDN_SUPER, LANES), lambda i: (i, col_blk)), vec, vec],
        out_specs=[spec, spec],
        out_shape=[jax.ShapeDtypeStruct((m, LANES), F32), jax.ShapeDtypeStruct((m, LANES), F32)],
        compiler_params=_params("parallel"),
        name="gdn_gates",
    )(proj, row(a_log), row(dt_bias))


def _gdn_kernel(q_ref, k_ref, v_ref, z_ref, col_ref, row_ref, s0_ref, ng_ref, o_ref, sf_ref, of_s, ob_s, *, nsc):
    n = GDN_SUPER
    ri = lax.broadcasted_iota(jnp.int32, (n, n), 0)
    ci = lax.broadcasted_iota(jnp.int32, (n, n), 1)
    b64 = (ri >> 6) == (ci >> 6)
    b32 = (ri >> 5) == (ci >> 5)
    b16 = (ri >> 4) == (ci >> 4)
    eye = jnp.where(ri == ci, 1.0, 0.0)
    causal = (b64 & (ri >= ci), b64 & (ri <= ci))
    strict = (b64 & (ri > ci), b64 & (ri < ci))
    nchunk = n // GDN_CHUNK

    def mm(a, b):
        return _dot(a.astype(BF16), b.astype(BF16))

    def half(d, sc, state):
        q, k, v = q_ref[0, sc], k_ref[0, sc], v_ref[0, sc]
        col, row = col_ref[0, sc], row_ref[0, sc]
        gcol, bcol, tcol = col[:, 3 * d:3 * d + 1], col[:, 3 * d + 1:3 * d + 2], col[:, 3 * d + 2:3 * d + 3]
        grow, trow = row[3 * d:3 * d + 1, :], row[3 * d + 2:3 * d + 3, :]
        decay = jnp.exp(jnp.where(causal[d], gcol - grow, -jnp.inf))
        k16 = k.astype(BF16)
        kb = k * bcol
        lmat = jnp.where(strict[d], _dot_nt(kb.astype(BF16), k16) * decay, 0.0)
        intra = (_dot_nt(q.astype(BF16), k16) * decay).astype(BF16)
        l0 = jnp.where(b16, lmat, 0.0)
        inv = eye - l0
        pw = mm(l0, l0)
        inv = inv + mm(inv, pw)
        pw = mm(pw, pw)
        inv = inv + mm(inv, pw)
        pw = mm(pw, pw)
        inv = inv + mm(inv, pw)
        c1 = jnp.where(b32 & (~b16), lmat, 0.0)
        inv = inv - mm(mm(inv, c1), inv)
        c2 = jnp.where(b32, 0.0, lmat)
        inv = inv - mm(mm(inv, c2), inv)
        eg = jnp.exp(gcol)
        uw = mm(inv, jnp.concatenate([v * bcol, kb * eg], axis=1))
        u, w = uw[:, :LANES], uw[:, LANES:].astype(BF16)
        qd = (q * eg).astype(BF16)
        kdt = jnp.transpose(k * jnp.exp(tcol - gcol)).astype(BF16)
        outs = [None] * nchunk
        for j in (range(nchunk) if d == 0 else range(nchunk - 1, -1, -1)):
            r = slice(j * GDN_CHUNK, (j + 1) * GDN_CHUNK)
            s16 = state.astype(BF16)
            vn = u[r] - _dot(w[r], s16)
            vn16 = vn.astype(BF16)
            outs[j] = _dot(qd[r], s16) + _dot(intra[r, r], vn16)
            state = state * jnp.exp(trow[:, j * GDN_CHUNK:j * GDN_CHUNK + 1]) + _dot(kdt[:, r], vn16)
        return jnp.concatenate(outs, axis=0), state

    def body(i, carry):
        sf, sb = carry
        o_f, sf = half(0, i, sf)
        o_b, sb = half(1, nsc - 1 - i, sb)
        of_s[i] = o_f
        ob_s[nsc - 1 - i] = o_b
        return sf, sb

    init = (s0_ref[0, 0, 0], s0_ref[0, 1, 0])
    sf, sb = body(0, init) if nsc == 1 else lax.fori_loop(0, nsc, body, init)
    sf_ref[0, 0, 0] = sf
    sf_ref[0, 1, 0] = sb
    for sc in range(nsc):
        o = _rms(of_s[sc] + ob_s[sc], ng_ref[...])
        o_ref[0, sc] = (o * _silu(z_ref[0, sc])).astype(BF16)


def _gdn_scan(qkv, proj, cols, rows, s0, norm_g, batch, seq):
    nsc = seq // GDN_SUPER
    h = GDN_HEADS
    qkv4 = qkv.reshape(batch, nsc, GDN_SUPER, 3 * h * LANES)
    proj4 = proj.reshape(batch, nsc, GDN_SUPER, proj.shape[-1])

    def tile(off):
        return pl.BlockSpec((1, nsc, GDN_SUPER, LANES), lambda b, j: (b, 0, 0, off + j))

    st = pl.BlockSpec((1, 2, 1, GDN_DK, LANES), lambda b, j: (b, 0, j, 0, 0))
    return pl.pallas_call(
        functools.partial(_gdn_kernel, nsc=nsc),
        grid=(batch, h),
        in_specs=[tile(0), tile(h), tile(2 * h), tile(3 * h),
                  pl.BlockSpec((1, nsc, GDN_SUPER, 8), lambda b, j: (j, b, 0, 0)),
                  pl.BlockSpec((1, nsc, 8, GDN_SUPER), lambda b, j: (j, b, 0, 0)),
                  st, pl.BlockSpec((1, LANES), lambda b, j: (0, 0))],
        out_specs=[tile(0), st],
        out_shape=[jax.ShapeDtypeStruct((batch, nsc, GDN_SUPER, h * LANES), BF16),
                   jax.ShapeDtypeStruct((batch, 2, h, GDN_DK, LANES), F32)],
        scratch_shapes=[pltpu.VMEM((nsc, GDN_SUPER, LANES), F32), pltpu.VMEM((nsc, GDN_SUPER, LANES), F32)],
        compiler_params=_params("parallel", "parallel"),
        name="gdn_scan",
    )(qkv4, qkv4, qkv4, proj4, cols, rows, s0, norm_g.reshape(1, LANES))


def _gdn_mixer(x, g, mod, rows_per_mod, batch, seq, s0, w_cat, conv_w, a_log, dt_bias, norm_g):
    proj = _nmm(x, g, mod, w_cat, rows_per_mod, 0, 1)
    qkv = _gdn_prep(proj.reshape(batch, seq, proj.shape[-1]), conv_w)
    gates, tot = _gdn_gates(proj, a_log, dt_bias)
    h = GDN_HEADS
    n = batch * seq
    per = jnp.stack([gates[:, 0:h], gates[:, h:2 * h], tot[:, 0:h],
                     gates[:, 2 * h:3 * h], gates[:, 3 * h:4 * h], tot[:, 2 * h:3 * h],
                     jnp.zeros((n, h), F32), jnp.zeros((n, h), F32)], axis=-1)
    per = per.reshape(n // GDN_SUPER, GDN_SUPER, h, 8)
    cols = jnp.transpose(per, (2, 0, 1, 3))
    rows = jnp.transpose(per, (2, 0, 3, 1))
    o, s_fin = _gdn_scan(qkv, proj, cols, rows, s0, norm_g, batch, seq)
    return o.reshape(n, D_MODEL), s_fin


def _moe_pre_kernel(x_ref, g_ref, mod_ref, wr_ref, h_ref, aff_ref):
    h = _norm_mod(x_ref[...], g_ref[...], mod_ref, 3, 4)
    h_ref[...] = h.astype(BF16)
    h1, h2 = _split_bf16(h, 2)
    logits = _dot(h1, wr_ref[0]) + (_dot(h1, wr_ref[1]) + _dot(h2, wr_ref[0]))
    e = jnp.exp(logits - jnp.max(logits, axis=-1, keepdims=True))
    aff_ref[...] = e / jnp.sum(e, axis=-1, keepdims=True)


def _moe_pre(x, g, mod, rows_per_mod, w_router, tm=512):
    m = x.shape[0]
    w1 = w_router.astype(BF16)
    w2 = (w_router - w1.astype(F32)).astype(BF16)
    return pl.pallas_call(
        _moe_pre_kernel,
        grid=(m // tm,),
        in_specs=[pl.BlockSpec((tm, D_MODEL), lambda i: (i, 0)),
                  pl.BlockSpec((1, D_MODEL), lambda i: (0, 0)),
                  pl.BlockSpec((1, 6, D_MODEL), lambda i: ((i * tm) // rows_per_mod, 0, 0)),
                  pl.BlockSpec((2, D_MODEL, N_EXPERTS), lambda i: (0, 0, 0))],
        out_specs=[pl.BlockSpec((tm, D_MODEL), lambda i: (i, 0)),
                   pl.BlockSpec((tm, N_EXPERTS), lambda i: (i, 0))],
        out_shape=[jax.ShapeDtypeStruct((m, D_MODEL), BF16), jax.ShapeDtypeStruct((m, N_EXPERTS), F32)],
        compiler_params=_params("parallel"),
        name="moe_router",
    )(x, g.reshape(1, D_MODEL), mod, jnp.stack([w1, w2]))


def _ffn_kernel(x_ref, gate_ref, wg_ref, wu_ref, wd_ref, o_ref, wg_s, wu_s, wd_s):
    @pl.when(pl.program_id(1) == 0)
    def _():
        wg_s[...] = wg_ref[0].astype(BF16)
        wu_s[...] = wu_ref[0].astype(BF16)
        wd_s[...] = wd_ref[0].astype(BF16)

    x = x_ref[0]
    ff = wg_s.shape[1]
    half = ff // 2
    acc = None
    for c0 in (0, half):
        a = _dot(x, wg_s[:, c0:c0 + half])
        b = _dot(x, wu_s[:, c0:c0 + half])
        part = _dot((_silu(a) * b).astype(BF16), wd_s[c0:c0 + half, :])
        acc = part if acc is None else acc + part
    o_ref[0] = acc * gate_ref[0]


def _expert_ffn(xe, gate, w_gate, w_up, w_down, tc=512):
    n_exp, cap, _ = xe.shape
    ff = w_gate.shape[-1]
    tc = min(tc, cap)
    wspec_in = pl.BlockSpec((1, D_MODEL, ff), lambda e, t: (e, 0, 0))
    return pl.pallas_call(
        _ffn_kernel,
        grid=(n_exp, cap // tc),
        in_specs=[pl.BlockSpec((1, tc, D_MODEL), lambda e, t: (e, t, 0)),
                  pl.BlockSpec((1, tc, 1), lambda e, t: (e, t, 0)),
                  wspec_in, wspec_in,
                  pl.BlockSpec((1, ff, D_MODEL), lambda e, t: (e, 0, 0))],
        out_specs=pl.BlockSpec((1, tc, D_MODEL), lambda e, t: (e, t, 0)),
        out_shape=jax.ShapeDtypeStruct((n_exp, cap, D_MODEL), F32),
        scratch_shapes=[pltpu.VMEM((D_MODEL, ff), BF16), pltpu.VMEM((D_MODEL, ff), BF16),
                        pltpu.VMEM((ff, D_MODEL), BF16)],
        compiler_params=_params("parallel", "arbitrary"),
        name="expert_ffn",
    )(xe, gate.reshape(n_exp, cap, 1), w_gate, w_up, w_down)


def _expert_choice(x, g, mod, rows_per_mod, w_router, w_gate, w_up, w_down):
    n = x.shape[0]
    cap = EC_CAPACITY_FACTOR * n // N_EXPERTS
    h, aff = _moe_pre(x, g, mod, rows_per_mod, w_router)
    gate, idx = lax.top_k(aff.T, cap)
    ye = _expert_ffn(h[idx], gate, w_gate, w_up, w_down)
    return jnp.zeros((n, D_MODEL), F32).at[idx.reshape(-1)].add(ye.reshape(-1, D_MODEL))


def _trunk(x, mods, rows_per_mod, batch, seq, caches, p):
    depth = p["mod_w"].shape[0]
    n = batch * seq
    new = {}
    for layer in range(depth):
        kind, slot = layer % 4, layer // 4
        mod = mods[layer]
        g_mix = p["norm_mix_g"][layer]
        if kind == 0:
            w_ab = jnp.pad(p["gdn_w_ab"][slot], ((0, 0), (0, LANES - 4 * GDN_HEADS)))
            w_cat = jnp.concatenate([p["gdn_w_in"][slot], w_ab], axis=1).astype(BF16)
            if caches is None:
                s0 = jnp.zeros((batch, 2, GDN_HEADS, GDN_DK, LANES), F32)
            else:
                s0 = caches["state_gdn"][:, slot]
            o, s_fin = _gdn_mixer(x, g_mix, mod, rows_per_mod, batch, seq, s0, w_cat, p["gdn_conv_w"][slot],
                                  p["gdn_a_log"][slot], p["gdn_dt_bias"][slot], p["gdn_norm_g"][slot])
            new.setdefault("gdn", []).append(s_fin)
            w_out = p["gdn_w_out"][slot]
        elif kind == 1:
            proj = _nmm(x, g_mix, mod, p["diff_w_in"][slot].astype(BF16), rows_per_mod, 0, 1)
            q, k, v = _attn_prep(proj, "diff", seq, caches is not None)
            nk = seq
            if caches is None:
                hw = DIFF_HEADS * LANES
                new.setdefault("diff_k", []).append(proj[:, hw:2 * hw].reshape(batch, seq, DIFF_HEADS, LANES))
                new.setdefault("diff_v", []).append(proj[:, 2 * hw:].reshape(batch, seq, DIFF_HEADS, LANES))
            else:
                kc = caches["cache_diff_k"][:, slot].reshape(batch, -1, DIFF_HEADS * LANES).astype(BF16)
                vc = caches["cache_diff_v"][:, slot].reshape(batch, -1, DIFF_HEADS * LANES).astype(BF16)
                nk = seq + kc.shape[1]
                k = jnp.concatenate([k.reshape(batch, seq, -1), kc], axis=1).reshape(batch * nk, -1)
                v = jnp.concatenate([v.reshape(batch, seq, -1), vc], axis=1).reshape(batch * nk, -1)
            lam_init = 0.8 - 0.6 * math.exp(-0.3 * layer)
            o = _attention(q, k, v, "diff", batch, seq, nk,
                           (p["diff_lambda"][slot], p["diff_subln_g"][slot]), lam_init)
            w_out = p["diff_w_out"][slot]
        elif kind == 2:
            proj = _nmm(x, g_mix, mod, p["gqa_w_in"][slot].astype(BF16), rows_per_mod, 0, 1)
            q, k, v, kf = _attn_prep(proj, "gqa", seq, caches is not None,
                                     (p["gqa_q_norm_g"][slot], p["gqa_k_norm_g"][slot]))
            nk = seq
            if caches is None:
                v0 = (GQA_HEADS + GQA_KV) * LANES
                new.setdefault("gqa_k", []).append(kf.reshape(batch, seq, GQA_KV, GQA_HD))
                new.setdefault("gqa_v", []).append(proj[:, v0:].reshape(batch, seq, GQA_KV, GQA_HD))
            else:
                kc = caches["cache_gqa_k"][:, slot].reshape(batch, -1, GQA_KV * GQA_HD).astype(BF16)
                vc = caches["cache_gqa_v"][:, slot].reshape(batch, -1, GQA_KV * GQA_HD).astype(BF16)
                nk = seq + kc.shape[1]
                k = jnp.concatenate([k.reshape(batch, seq, -1), kc], axis=1).reshape(batch * nk, -1)
                v = jnp.concatenate([v.reshape(batch, seq, -1), vc], axis=1).reshape(batch * nk, -1)
            o = _attention(q, k, v, "gqa", batch, seq, nk)
            w_out = p["gqa_w_out"][slot]
        else:
            proj = _nmm(x, g_mix, mod, p["swa_w_in"][slot].astype(BF16), rows_per_mod, 0, 1)
            q, k, v = _attn_prep(proj, "swa", seq, caches is not None)
            sinks = p["swa_sinks"][slot].astype(F32)
            if caches is None:
                k0 = SWA_HEADS * SWA_HD
                v0 = k0 + SWA_KV * SWA_HD
                new.setdefault("swa_k", []).append(proj[:, k0:v0].reshape(batch, seq, SWA_KV, SWA_HD))
                new.setdefault("swa_v", []).append(proj[:, v0:].reshape(batch, seq, SWA_KV, SWA_HD))
                o = _attention(q, k, v, "swa", batch, seq, seq, (sinks,))
            else:
                kc = jnp.pad(caches["cache_swa_k"][:, slot], ((0, 0), (0, 0), (0, 0), (0, LANES - SWA_HD)))
                past = kc.shape[1]
                kc = kc.reshape(batch, past, SWA_KV * LANES).astype(BF16)
                vc = caches["cache_swa_v"][:, slot].reshape(batch, past, SWA_KV * SWA_HD).astype(BF16)
                nk = seq + past
                k = jnp.concatenate([k.reshape(batch, seq, -1), kc], axis=1).reshape(batch * nk, -1)
                v = jnp.concatenate([v.reshape(batch, seq, -1), vc], axis=1).reshape(batch * nk, -1)
                o = _band_attention(q, k, v, sinks, batch, seq, past)
            w_out = p["swa_w_out"][slot]
        x = _mm_resid(o, w_out.astype(BF16), x, mod, rows_per_mod, 2)
        y = _expert_choice(x, p["norm_ffn_g"][layer], mod, rows_per_mod, p["moe_router"][layer],
                           p["moe_w_gate"][layer], p["moe_w_up"][layer], p["moe_w_down"][layer])
        x = _resid_gate(x, y, mod, rows_per_mod, 5, p["final_g"], layer == depth - 1)
    return x, new


def kernel(x_prompt, x_sample, state_gdn, cache_diff_k, cache_diff_v, cache_gqa_k, cache_gqa_v, cache_swa_k, cache_swa_v, c, c_ctx, mod_w, mod_b, norm_mix_g, norm_ffn_g, final_g, gdn_w_in, gdn_conv_w, gdn_w_ab, gdn_a_log, gdn_dt_bias, gdn_norm_g, gdn_w_out, diff_w_in, diff_lambda, diff_subln_g, diff_w_out, gqa_w_in, gqa_q_norm_g, gqa_k_norm_g, gqa_w_out, swa_w_in, swa_sinks, swa_w_out, moe_router, moe_w_gate, moe_w_up, moe_w_down):
    p = dict(mod_w=mod_w, mod_b=mod_b, norm_mix_g=norm_mix_g, norm_ffn_g=norm_ffn_g, final_g=final_g,
             gdn_w_in=gdn_w_in, gdn_conv_w=gdn_conv_w, gdn_w_ab=gdn_w_ab, gdn_a_log=gdn_a_log,
             gdn_dt_bias=gdn_dt_bias, gdn_norm_g=gdn_norm_g, gdn_w_out=gdn_w_out,
             diff_w_in=diff_w_in, diff_lambda=diff_lambda, diff_subln_g=diff_subln_g, diff_w_out=diff_w_out,
             gqa_w_in=gqa_w_in, gqa_q_norm_g=gqa_q_norm_g, gqa_k_norm_g=gqa_k_norm_g, gqa_w_out=gqa_w_out,
             swa_w_in=swa_w_in, swa_sinks=swa_sinks, swa_w_out=swa_w_out,
             moe_router=moe_router, moe_w_gate=moe_w_gate, moe_w_up=moe_w_up, moe_w_down=moe_w_down)
    batch, seq, _ = x_prompt.shape
    dec_batch, dec_seq, _ = x_sample.shape
    depth = mod_w.shape[0]
    rows = 8 * ((1 + dec_batch + 7) // 8)
    cond = jnp.zeros((rows, D_MODEL), F32).at[0].set(c_ctx).at[1:1 + dec_batch].set(c)
    mod_all = _mod_proj(cond, mod_w, mod_b).reshape(depth, rows, 6, D_MODEL)
    mods_ctx = [mod_all[l, 0:1] for l in range(depth)]
    mods_lat = [mod_all[l, 1:1 + dec_batch] for l in range(depth)]

    y_prompt, st = _trunk(x_prompt.reshape(batch * seq, D_MODEL), mods_ctx, batch * seq, batch, seq, None, p)
    caches = dict(state_gdn=state_gdn, cache_diff_k=cache_diff_k, cache_diff_v=cache_diff_v,
                  cache_gqa_k=cache_gqa_k, cache_gqa_v=cache_gqa_v,
                  cache_swa_k=cache_swa_k, cache_swa_v=cache_swa_v)
    y_sample, _ = _trunk(x_sample.reshape(dec_batch * dec_seq, D_MODEL), mods_lat, dec_seq, dec_batch, dec_seq,
                         caches, p)

    def stack(xs):
        return jnp.stack(xs, axis=1)

    return (y_prompt.reshape(batch, seq, D_MODEL), y_sample.reshape(dec_batch, dec_seq, D_MODEL),
            stack(st["gdn"]), stack(st["diff_k"]), stack(st["diff_v"]),
            stack(st["gqa_k"]), stack(st["gqa_v"]), stack(st["swa_k"]), stack(st["swa_v"]))
```

```python
import functools
import math

import jax
import jax.numpy as jnp
from jax import lax
from jax.experimental import pallas as pl
from jax.experimental.pallas import tpu as pltpu

F32 = jnp.float32
BF16 = jnp.bfloat16

D_MODEL = 1024
GRID_W = 64
ROPE_THETA = 10000.0
NORM_EPS = 1e-6
LANES = 128
GDN_HEADS = 8
GDN_DK = 128
GDN_CHUNK = 64
GDN_SUPER = 256
DIFF_HEADS = 8
DIFF_DH = 64
GQA_HEADS = 8
GQA_KV = 2
GQA_HD = 128
SWA_HEADS = 16
SWA_KV = 4
SWA_HD = 64
WINDOW = 128
N_EXPERTS = 16
EC_CAPACITY_FACTOR = 2
VMEM_LIMIT_BYTES = 56 * 1024 * 1024


def _params(*sem):
    return pltpu.CompilerParams(dimension_semantics=sem, vmem_limit_bytes=VMEM_LIMIT_BYTES)


def _silu(x):
    return x / (1.0 + jnp.exp(-x))


def _rms(x, g):
    return x * lax.rsqrt(jnp.mean(x * x, axis=-1, keepdims=True) + NORM_EPS) * g


def _dot(a, b):
    return jnp.dot(a, b, preferred_element_type=F32)


def _dot_nt(a, b):
    return lax.dot_general(a, b, (((1,), (1,)), ((), ())), preferred_element_type=F32)


def _split_bf16(x, parts):
    out = []
    for _ in range(parts - 1):
        hi = x.astype(BF16)
        out.append(hi)
        x = x - hi.astype(F32)
    out.append(x.astype(BF16))
    return out


def _mod_kernel(c_ref, w_ref, b_ref, o_ref):
    a = _silu(c_ref[...]).astype(BF16)
    o_ref[0] = _dot(a, w_ref[0].astype(BF16)) + b_ref[0]


def _mod_proj(cond, mod_w, mod_b):
    depth, _, width = mod_w.shape
    rows = cond.shape[0]
    tn = 1536
    return pl.pallas_call(
        _mod_kernel,
        grid=(depth, width // tn),
        in_specs=[pl.BlockSpec((rows, D_MODEL), lambda l, j: (0, 0)),
                  pl.BlockSpec((1, D_MODEL, tn), lambda l, j: (l, 0, j)),
                  pl.BlockSpec((1, 1, tn), lambda l, j: (l, 0, j))],
        out_specs=pl.BlockSpec((1, rows, tn), lambda l, j: (l, 0, j)),
        out_shape=jax.ShapeDtypeStruct((depth, rows, width), F32),
        compiler_params=_params("parallel", "parallel"),
        name="mod_proj",
    )(cond, mod_w, mod_b.reshape(depth, 1, width))


def _norm_mod(x, g, mod_ref, sh, sc):
    return _rms(x, g) * (1.0 + mod_ref[0, sc:sc + 1, :]) + mod_ref[0, sh:sh + 1, :]


def _nmm_kernel(x_ref, g_ref, mod_ref, w_ref, o_ref, *, sh, sc, chunk):
    h = _norm_mod(x_ref[...], g_ref[...], mod_ref, sh, sc).astype(BF16)
    width = w_ref.shape[1]
    for c0 in range(0, width, chunk):
        c1 = min(c0 + chunk, width)
        o_ref[:, c0:c1] = _dot(h, w_ref[:, c0:c1])


def _nmm(x, g, mod, w, rows_per_mod, sh, sc, tm=256):
    m = x.shape[0]
    width = w.shape[1]
    return pl.pallas_call(
        functools.partial(_nmm_kernel, sh=sh, sc=sc, chunk=512),
        grid=(m // tm,),
        in_specs=[pl.BlockSpec((tm, D_MODEL), lambda i: (i, 0)),
                  pl.BlockSpec((1, D_MODEL), lambda i: (0, 0)),
                  pl.BlockSpec((1, 6, D_MODEL), lambda i: ((i * tm) // rows_per_mod, 0, 0)),
                  pl.BlockSpec((D_MODEL, width), lambda i: (0, 0))],
        out_specs=pl.BlockSpec((tm, width), lambda i: (i, 0)),
        out_shape=jax.ShapeDtypeStruct((m, width), F32),
        compiler_params=_params("parallel"),
        name="norm_mod_matmul",
    )(x, g.reshape(1, D_MODEL), mod, w)


def _mmr_kernel(a_ref, w_ref, r_ref, mod_ref, o_ref, *, gi):
    acc = _dot(a_ref[...].astype(BF16), w_ref[...])
    o_ref[...] = r_ref[...] + mod_ref[0, gi:gi + 1, :] * acc


def _mm_resid(a, w, res, mod, rows_per_mod, gi, tm=512):
    m = a.shape[0]
    return pl.pallas_call(
        functools.partial(_mmr_kernel, gi=gi),
        grid=(m // tm,),
        in_specs=[pl.BlockSpec((tm, D_MODEL), lambda i: (i, 0)),
                  pl.BlockSpec((D_MODEL, D_MODEL), lambda i: (0, 0)),
                  pl.BlockSpec((tm, D_MODEL), lambda i: (i, 0)),
                  pl.BlockSpec((1, 6, D_MODEL), lambda i: ((i * tm) // rows_per_mod, 0, 0))],
        out_specs=pl.BlockSpec((tm, D_MODEL), lambda i: (i, 0)),
        out_shape=jax.ShapeDtypeStruct((m, D_MODEL), F32),
        compiler_params=_params("parallel"),
        name="out_proj_resid",
    )(a, w, res, mod)


def _rope_tables(n_tokens, dim):
    rows = n_tokens // GRID_W
    row = jnp.broadcast_to(jnp.arange(rows, dtype=F32)[:, None], (rows, GRID_W)).reshape(-1)
    col = jnp.broadcast_to(jnp.arange(GRID_W, dtype=F32)[None, :], (rows, GRID_W)).reshape(-1)
    n_freq = dim // 4
    inv = ROPE_THETA ** (-jnp.arange(n_freq, dtype=F32) / n_freq)
    ang = jnp.concatenate([row[:, None] * inv, col[:, None] * inv], axis=-1)
    cos, sin = jnp.cos(ang), jnp.sin(ang)
    zero = jnp.zeros_like(sin)
    reps = LANES // dim
    cos_t = jnp.tile(jnp.concatenate([cos, cos], axis=-1), (1, reps))
    sin_a = jnp.tile(jnp.concatenate([-sin, zero], axis=-1), (1, reps))
    sin_b = jnp.tile(jnp.concatenate([zero, sin], axis=-1), (1, reps))
    return cos_t, sin_a, sin_b


def _rope(x, cos, sin_a, sin_b, dim):
    half = dim // 2
    return x * cos + pltpu.roll(x, LANES - half, 1) * sin_a + pltpu.roll(x, half, 1) * sin_b


def _prep_kernel(*refs, kind, rope):
    if kind == "gqa":
        p_ref, cos_ref, sa_ref, sb_ref, qg_ref, kg_ref, q_out, k_out, v_out, kf_out = refs
    else:
        p_ref, cos_ref, sa_ref, sb_ref, q_out, k_out, v_out = refs
    cos, sa, sb = cos_ref[...], sa_ref[...], sb_ref[...]

    def blk(j):
        return p_ref[:, j * LANES:(j + 1) * LANES]

    if kind == "gqa":
        scale = GQA_HD ** -0.5
        for j in range(GQA_HEADS):
            y = _rms(blk(j), qg_ref[...])
            if rope:
                y = _rope(y, cos, sa, sb, GQA_HD)
            q_out[:, j * LANES:(j + 1) * LANES] = (y * scale).astype(BF16)
        for j in range(GQA_KV):
            y = _rms(blk(GQA_HEADS + j), kg_ref[...])
            kf_out[:, j * LANES:(j + 1) * LANES] = y
            if rope:
                y = _rope(y, cos, sa, sb, GQA_HD)
            k_out[:, j * LANES:(j + 1) * LANES] = y.astype(BF16)
        v0 = (GQA_HEADS + GQA_KV) * LANES
        v_out[...] = p_ref[:, v0:v0 + GQA_KV * LANES].astype(BF16)
    elif kind == "diff":
        scale = DIFF_DH ** -0.5
        for j in range(DIFF_HEADS):
            y = blk(j)
            if rope:
                y = _rope(y, cos, sa, sb, DIFF_DH)
            q_out[:, j * LANES:(j + 1) * LANES] = (y * scale).astype(BF16)
            y = blk(DIFF_HEADS + j)
            if rope:
                y = _rope(y, cos, sa, sb, DIFF_DH)
            k_out[:, j * LANES:(j + 1) * LANES] = y.astype(BF16)
        v_out[...] = p_ref[:, 2 * DIFF_HEADS * LANES:3 * DIFF_HEADS * LANES].astype(BF16)
    else:
        scale = SWA_HD ** -0.5
        low = lax.broadcasted_iota(jnp.int32, cos.shape, 1) < SWA_HD

        def pad_pair(y, out, j):
            out[:, (2 * j) * LANES:(2 * j + 1) * LANES] = jnp.where(low, y, 0.0).astype(BF16)
            out[:, (2 * j + 1) * LANES:(2 * j + 2) * LANES] = jnp.where(
                low, pltpu.roll(y, SWA_HD, 1), 0.0).astype(BF16)

        nq = SWA_HEADS // 2
        for j in range(nq):
            y = blk(j)
            if rope:
                y = _rope(y, cos, sa, sb, SWA_HD)
            pad_pair(y * scale, q_out, j)
        for j in range(SWA_KV // 2):
            y = blk(nq + j)
            if rope:
                y = _rope(y, cos, sa, sb, SWA_HD)
            pad_pair(y, k_out, j)
        v0 = (nq + SWA_KV // 2) * LANES
        v_out[...] = p_ref[:, v0:v0 + (SWA_KV // 2) * LANES].astype(BF16)


def _attn_prep(proj, kind, seq, rope, gains=None, tm=256):
    m, width = proj.shape
    dim = GQA_HD if kind == "gqa" else DIFF_DH
    cos, sa, sb = _rope_tables(seq, dim)
    nt = seq // tm
    if kind == "gqa":
        wq, wk, wv = GQA_HEADS * LANES, GQA_KV * LANES, GQA_KV * LANES
    elif kind == "diff":
        wq, wk, wv = DIFF_HEADS * LANES, DIFF_HEADS * LANES, DIFF_HEADS * LANES
    else:
        wq, wk, wv = SWA_HEADS * LANES, SWA_KV * LANES, (SWA_KV // 2) * LANES
    tab = pl.BlockSpec((tm, LANES), lambda i: (i % nt, 0))
    in_specs = [pl.BlockSpec((tm, width), lambda i: (i, 0)), tab, tab, tab]
    args = [proj, cos, sa, sb]
    out_shapes = [jax.ShapeDtypeStruct((m, wq), BF16), jax.ShapeDtypeStruct((m, wk), BF16),
                  jax.ShapeDtypeStruct((m, wv), BF16)]
    out_specs = [pl.BlockSpec((tm, wq), lambda i: (i, 0)), pl.BlockSpec((tm, wk), lambda i: (i, 0)),
                 pl.BlockSpec((tm, wv), lambda i: (i, 0))]
    if kind == "gqa":
        gspec = pl.BlockSpec((1, LANES), lambda i: (0, 0))
        in_specs += [gspec, gspec]
        args += [gains[0].reshape(1, LANES), gains[1].reshape(1, LANES)]
        out_shapes.append(jax.ShapeDtypeStruct((m, wk), F32))
        out_specs.append(pl.BlockSpec((tm, wk), lambda i: (i, 0)))
    return pl.pallas_call(
        functools.partial(_prep_kernel, kind=kind, rope=rope),
        grid=(m // tm,),
        in_specs=in_specs,
        out_specs=out_specs,
        out_shape=out_shapes,
        compiler_params=_params("parallel"),
        name="attn_prep_" + kind,
    )(*args)


def _pick_half(o_a, o_b, upper):
    low = lax.broadcasted_iota(jnp.int32, o_a.shape, 1) < SWA_HD
    if upper:
        return jnp.where(low, pltpu.roll(o_a, SWA_HD, 1), o_b)
    return jnp.where(low, o_a, pltpu.roll(o_b, SWA_HD, 1))


def _attn_kernel(*refs, mode, lam_init):
    if mode == "gqa":
        q_ref, k_ref, v_ref, o_ref = refs
    elif mode == "diff":
        q_ref, k_ref, v_ref, lam_ref, g_ref, o_ref = refs
    else:
        q_ref, k_ref, v_ref, sink_ref, o_ref = refs
    v = v_ref[...]
    if mode == "gqa":
        k = k_ref[...]
        for j in range(GQA_HEADS // GQA_KV):
            s = _dot_nt(q_ref[:, j * LANES:(j + 1) * LANES], k)
            e = jnp.exp(s - jnp.max(s, axis=-1, keepdims=True))
            o = _dot(e.astype(BF16), v) / jnp.sum(e, axis=-1, keepdims=True)
            o_ref[:, j * LANES:(j + 1) * LANES] = o.astype(BF16)
    elif mode == "diff":
        k = k_ref[...]
        q = q_ref[...]
        low = lax.broadcasted_iota(jnp.int32, q.shape, 1) < DIFF_DH
        lv = lam_ref[...]
        lam = (jnp.exp(jnp.sum(lv[0:1] * lv[1:2], axis=-1, keepdims=True))
               - jnp.exp(jnp.sum(lv[2:3] * lv[3:4], axis=-1, keepdims=True)) + lam_init)
        s0 = _dot_nt(jnp.where(low, q, 0.0).astype(BF16), k)
        s1 = _dot_nt(jnp.where(low, 0.0, q).astype(BF16), k)
        e0 = jnp.exp(s0 - jnp.max(s0, axis=-1, keepdims=True))
        e1 = jnp.exp(s1 - jnp.max(s1, axis=-1, keepdims=True))
        a = e0 * (1.0 / jnp.sum(e0, axis=-1, keepdims=True)) - e1 * (lam / jnp.sum(e1, axis=-1, keepdims=True))
        o = _dot(a.astype(BF16), v)
        o_ref[...] = (_rms(o, g_ref[...]) * (1.0 - lam_init)).astype(BF16)
    else:
        heads = SWA_HEADS // (SWA_KV // 2)
        group = SWA_HEADS // SWA_KV
        outs = []
        for j in range(heads):
            a = j // group
            s = _dot_nt(q_ref[:, j * LANES:(j + 1) * LANES], k_ref[:, a * LANES:(a + 1) * LANES])
            sink = sink_ref[0, :, j:j + 1]
            mx = jnp.maximum(jnp.max(s, axis=-1, keepdims=True), sink)
            e = jnp.exp(s - mx)
            den = jnp.sum(e, axis=-1, keepdims=True) + jnp.exp(sink - mx)
            outs.append(_dot(e.astype(BF16), v) / den)
        for jj in range(heads // 2):
            a = (2 * jj) // group
            o_ref[:, jj * LANES:(jj + 1) * LANES] = _pick_half(outs[2 * jj], outs[2 * jj + 1], a == 1).astype(BF16)


def _attention(q, k, v, mode, batch, seq, n_keys, extra=(), lam_init=0.0, tq=256):
    nq = seq // tq
    if mode == "gqa":
        groups, wq, wk, wv, wo = GQA_KV, (GQA_HEADS // GQA_KV) * LANES, LANES, LANES, (GQA_HEADS // GQA_KV) * LANES
    elif mode == "diff":
        groups, wq, wk, wv, wo = DIFF_HEADS, LANES, LANES, LANES, LANES
    else:
        groups, wq, wk, wv, wo = SWA_KV // 2, SWA_HEADS // 2 * LANES, 2 * LANES, LANES, 4 * LANES
    in_specs = [pl.BlockSpec((tq, wq), lambda b, g, i: (b * nq + i, g)),
                pl.BlockSpec((n_keys, wk), lambda b, g, i: (b, g)),
                pl.BlockSpec((n_keys, wv), lambda b, g, i: (b, g))]
    args = [q, k, v]
    if mode == "diff":
        lam_vecs, subln_g = extra
        in_specs += [pl.BlockSpec((4, DIFF_DH), lambda b, g, i: (0, 0)),
                     pl.BlockSpec((1, LANES), lambda b, g, i: (0, 0))]
        args += [lam_vecs, subln_g.reshape(1, LANES)]
    elif mode == "swa":
        (sinks,) = extra
        in_specs.append(pl.BlockSpec((1, 1, SWA_HEADS // 2), lambda b, g, i: (g, 0, 0)))
        args.append(sinks.reshape(2, 1, SWA_HEADS // 2))
    return pl.pallas_call(
        functools.partial(_attn_kernel, mode=mode, lam_init=lam_init),
        grid=(batch, groups, nq),
        in_specs=in_specs,
        out_specs=pl.BlockSpec((tq, wo), lambda b, g, i: (b * nq + i, g)),
        out_shape=jax.ShapeDtypeStruct((batch * seq, D_MODEL), BF16),
        compiler_params=_params("parallel", "parallel", "parallel"),
        name="attn_" + mode,
    )(*args)


def _band_kernel(q_ref, kp_ref, kc_ref, kn_ref, kx_ref, vp_ref, vc_ref, vn_ref, vx_ref, sink_ref, o_ref, *, nqb):
    i = pl.program_id(2)
    qi = lax.broadcasted_iota(jnp.int32, (WINDOW, WINDOW), 0)
    kj = lax.broadcasted_iota(jnp.int32, (WINDOW, WINDOW), 1)
    m_prev = (kj >= qi) & (i > 0)
    m_next = (kj <= qi) & (i < nqb - 1)
    heads = SWA_HEADS // (SWA_KV // 2)
    group = SWA_HEADS // SWA_KV
    outs = []
    for j in range(heads):
        a = j // group
        q = q_ref[:, j * LANES:(j + 1) * LANES]
        ks = slice(a * LANES, (a + 1) * LANES)
        s_p = jnp.where(m_prev, _dot_nt(q, kp_ref[:, ks]), -jnp.inf)
        s_c = _dot_nt(q, kc_ref[:, ks])
        s_n = jnp.where(m_next, _dot_nt(q, kn_ref[:, ks]), -jnp.inf)
        s_x = _dot_nt(q, kx_ref[:, ks])
        sink = sink_ref[0, :, j:j + 1]
        mx = jnp.maximum(jnp.maximum(jnp.max(s_p, axis=-1, keepdims=True), jnp.max(s_c, axis=-1, keepdims=True)),
                         jnp.maximum(jnp.max(s_n, axis=-1, keepdims=True), jnp.max(s_x, axis=-1, keepdims=True)))
        mx = jnp.maximum(mx, sink)
        e_p, e_c, e_n, e_x = jnp.exp(s_p - mx), jnp.exp(s_c - mx), jnp.exp(s_n - mx), jnp.exp(s_x - mx)
        den = (jnp.sum(e_p, axis=-1, keepdims=True) + jnp.sum(e_c, axis=-1, keepdims=True)
               + jnp.sum(e_n, axis=-1, keepdims=True) + jnp.sum(e_x, axis=-1, keepdims=True) + jnp.exp(sink - mx))
        o = (_dot(e_p.astype(BF16), vp_ref[...]) + _dot(e_c.astype(BF16), vc_ref[...])
             + _dot(e_n.astype(BF16), vn_ref[...]) + _dot(e_x.astype(BF16), vx_ref[...]))
        outs.append(o / den)
    for jj in range(heads // 2):
        a = (2 * jj) // group
        o_ref[:, jj * LANES:(jj + 1) * LANES] = _pick_half(outs[2 * jj], outs[2 * jj + 1], a == 1).astype(BF16)


def _band_attention(q, k, v, sinks, batch, seq, past):
    nqb = seq // WINDOW
    nkb = (seq + past) // WINDOW
    pairs = SWA_KV // 2
    xb = past // WINDOW
    assert past % WINDOW == 0 and seq % past == 0

    def band(shift, width):
        return pl.BlockSpec((WINDOW, width),
                            lambda b, g, i: (b * nkb + jnp.clip(i + shift, 0, nqb - 1), g))

    def ctx(width):
        return pl.BlockSpec((past, width), lambda b, g, i: (b * (nkb // xb) + seq // past, g))

    return pl.pallas_call(
        functools.partial(_band_kernel, nqb=nqb),
        grid=(batch, pairs, nqb),
        in_specs=[pl.BlockSpec((WINDOW, SWA_HEADS // 2 * LANES), lambda b, g, i: (b * nqb + i, g)),
                  band(-1, 2 * LANES), band(0, 2 * LANES), band(1, 2 * LANES), ctx(2 * LANES),
                  band(-1, LANES), band(0, LANES), band(1, LANES), ctx(LANES),
                  pl.BlockSpec((1, 1, SWA_HEADS // 2), lambda b, g, i: (g, 0, 0))],
        out_specs=pl.BlockSpec((WINDOW, 4 * LANES), lambda b, g, i: (b * nqb + i, g)),
        out_shape=jax.ShapeDtypeStruct((batch * seq, D_MODEL), BF16),
        compiler_params=_params("parallel", "parallel", "parallel"),
        name="attn_band",
    )(q, k, k, k, k, v, v, v, v, sinks.reshape(2, 1, SWA_HEADS // 2))


def _gdn_prep_kernel(p_ref, w_ref, o_ref):
    x = p_ref[0]
    t = x.shape[0]
    w = w_ref[...]
    row = lax.broadcasted_iota(jnp.int32, x.shape, 0)
    xm = jnp.where(row == 0, 0.0, pltpu.roll(x, 1, 0))
    xp = jnp.where(row == t - 1, 0.0, pltpu.roll(x, t - 1, 0))
    y = _silu(xm * w[0:1] + x * w[1:2] + xp * w[2:3])
    rs = lax.rsqrt(jnp.sum(y * y, axis=-1, keepdims=True) + NORM_EPS)
    j = pl.program_id(1)
    is_q = (j < GDN_HEADS).astype(F32)
    normed = (j < 2 * GDN_HEADS).astype(F32)
    scale = is_q * (GDN_DK ** -0.5) + (1.0 - is_q)
    o_ref[0] = y * (rs * (scale * normed) + (1.0 - normed))


def _gdn_prep(proj3, conv_w):
    batch, seq, _ = proj3.shape
    nblk = 3 * GDN_HEADS
    return pl.pallas_call(
        _gdn_prep_kernel,
        grid=(batch, nblk),
        in_specs=[pl.BlockSpec((1, seq, LANES), lambda b, j: (b, 0, j)),
                  pl.BlockSpec((3, LANES), lambda b, j: (0, j))],
        out_specs=pl.BlockSpec((1, seq, LANES), lambda b, j: (b, 0, j)),
        out_shape=jax.ShapeDtypeStruct((batch, seq, nblk * LANES), F32),
        compiler_params=_params("parallel", "parallel"),
        name="gdn_conv",
    )(proj3, conv_w)


def _gdn_gate_kernel(p_ref, al_ref, dtb_ref, o_ref, t_ref):
    ab = p_ref[...]
    lane = lax.broadcasted_iota(jnp.int32, ab.shape, 1)
    is_g = (lane < 4 * GDN_HEADS) & ((lane & GDN_HEADS) == 0)
    is_fwd = lane < 2 * GDN_HEADS
    x = ab + dtb_ref[...]
    softplus = jnp.maximum(x, 0.0) + jnp.log(1.0 + jnp.exp(-jnp.abs(x)))
    g = jnp.where(is_g, -jnp.exp(al_ref[...]) * softplus, 0.0)
    beta = 1.0 / (1.0 + jnp.exp(-ab))
    ri = lax.broadcasted_iota(jnp.int32, (GDN_SUPER, GDN_SUPER), 0)
    ci = lax.broadcasted_iota(jnp.int32, (GDN_SUPER, GDN_SUPER), 1)
    same = (ri // GDN_CHUNK) == (ci // GDN_CHUNK)
    lower = jnp.where(same & (ri >= ci), 1.0, 0.0).astype(BF16)
    upper = jnp.where(same & (ri <= ci), 1.0, 0.0).astype(BF16)
    parts = _split_bf16(g, 3)
    pre = _dot(lower, parts[0]) + _dot(lower, parts[1]) + _dot(lower, parts[2])
    suf = _dot(upper, parts[0]) + _dot(upper, parts[1]) + _dot(upper, parts[2])
    o_ref[...] = jnp.where(is_g, jnp.where(is_fwd, pre, suf), beta)
    t_ref[...] = pre + suf - g


def _gdn_gates(proj, a_log, dt_bias):
    m, width = proj.shape
    col_blk = (width - LANES) // LANES
    pad = jnp.zeros((2, GDN_HEADS), F32)

    def row(v):
        return jnp.pad(jnp.stack([v.astype(F32), pad], axis=1).reshape(1, 4 * GDN_HEADS),
                       ((0, 0), (0, LANES - 4 * GDN_HEADS)))

    spec = pl.BlockSpec((GDN_SUPER, LANES), lambda i: (i, 0))
    vec = pl.BlockSpec((1, LANES), lambda i: (0, 0))
    return pl.pallas_call(
        _gdn_gate_kernel,
        grid=(m // GDN_SUPER,),
        in_specs=[pl.BlockSpec((GDN_SUPER, LANES), lambda i: (i, col_blk)), vec, vec],
        out_specs=[spec, spec],
        out_shape=[jax.ShapeDtypeStruct((m, LANES), F32), jax.ShapeDtypeStruct((m, LANES), F32)],
        compiler_params=_params("parallel"),
        name="gdn_gates",
    )(proj, row(a_log), row(dt_bias))


def _gdn_kernel(q_ref, k_ref, v_ref, z_ref, col_ref, row_ref, s0_ref, ng_ref, o_ref, sf_ref, of_s, ob_s, *, nsc):
    n = GDN_SUPER
    ri = lax.broadcasted_iota(jnp.int32, (n, n), 0)
    ci = lax.broadcasted_iota(jnp.int32, (n, n), 1)
    b64 = (ri >> 6) == (ci >> 6)
    b32 = (ri >> 5) == (ci >> 5)
    b16 = (ri >> 4) == (ci >> 4)
    eye = jnp.where(ri == ci, 1.0, 0.0)
    causal = (b64 & (ri >= ci), b64 & (ri <= ci))
    strict = (b64 & (ri > ci), b64 & (ri < ci))
    nchunk = n // GDN_CHUNK

    def mm(a, b):
        return _dot(a.astype(BF16), b.astype(BF16))

    def half(d, sc, state):
        q, k, v = q_ref[0, sc], k_ref[0, sc], v_ref[0, sc]
        col, row = col_ref[0, sc], row_ref[0, sc]
        gcol, bcol, tcol = col[:, 3 * d:3 * d + 1], col[:, 3 * d + 1:3 * d + 2], col[:, 3 * d + 2:3 * d + 3]
        grow, trow = row[3 * d:3 * d + 1, :], row[3 * d + 2:3 * d + 3, :]
        decay = jnp.exp(jnp.where(causal[d], gcol - grow, -jnp.inf))
        k16 = k.astype(BF16)
        kb = k * bcol
        lmat = jnp.where(strict[d], _dot_nt(kb.astype(BF16), k16) * decay, 0.0)
        intra = (_dot_nt(q.astype(BF16), k16) * decay).astype(BF16)
        l0 = jnp.where(b16, lmat, 0.0)
        inv = eye - l0
        pw = mm(l0, l0)
        inv = inv + mm(inv, pw)
        pw = mm(pw, pw)
        inv = inv + mm(inv, pw)
        pw = mm(pw, pw)
        inv = inv + mm(inv, pw)
        c1 = jnp.where(b32 & (~b16), lmat, 0.0)
        inv = inv - mm(mm(inv, c1), inv)
        c2 = jnp.where(b32, 0.0, lmat)
        inv = inv - mm(mm(inv, c2), inv)
        eg = jnp.exp(gcol)
        uw = mm(inv, jnp.concatenate([v * bcol, kb * eg], axis=1))
        u, w = uw[:, :LANES], uw[:, LANES:].astype(BF16)
        qd = (q * eg).astype(BF16)
        kdt = jnp.transpose(k * jnp.exp(tcol - gcol)).astype(BF16)
        outs = [None] * nchunk
        for j in (range(nchunk) if d == 0 else range(nchunk - 1, -1, -1)):
            r = slice(j * GDN_CHUNK, (j + 1) * GDN_CHUNK)
            s16 = state.astype(BF16)
            vn = u[r] - _dot(w[r], s16)
            vn16 = vn.astype(BF16)
            outs[j] = _dot(qd[r], s16) + _dot(intra[r, r], vn16)
            state = state * jnp.exp(trow[:, j * GDN_CHUNK:j * GDN_CHUNK + 1]) + _dot(kdt[:, r], vn16)
        return jnp.concatenate(outs, axis=0), state

    def body(i, carry):
        sf, sb = carry
        o_f, sf = half(0, i, sf)
        o_b, sb = half(1, nsc - 1 - i, sb)
        of_s[i] = o_f
        ob_s[nsc - 1 - i] = o_b
        return sf, sb

    init = (s0_ref[0, 0, 0], s0_ref[0, 1, 0])
    sf, sb = body(0, init) if nsc == 1 else lax.fori_loop(0, nsc, body, init)
    sf_ref[0, 0, 0] = sf
    sf_ref[0, 1, 0] = sb
    for sc in range(nsc):
        o = _rms(of_s[sc] + ob_s[sc], ng_ref[...])
        o_ref[0, sc] = (o * _silu(z_ref[0, sc])).astype(BF16)


def _gdn_scan(qkv, proj, cols, rows, s0, norm_g, batch, seq):
    nsc = seq // GDN_SUPER
    h = GDN_HEADS
    qkv4 = qkv.reshape(batch, nsc, GDN_SUPER, 3 * h * LANES)
    proj4 = proj.reshape(batch, nsc, GDN_SUPER, proj.shape[-1])

    def tile(off):
        return pl.BlockSpec((1, nsc, GDN_SUPER, LANES), lambda b, j: (b, 0, 0, off + j))

    st = pl.BlockSpec((1, 2, 1, GDN_DK, LANES), lambda b, j: (b, 0, j, 0, 0))
    return pl.pallas_call(
        functools.partial(_gdn_kernel, nsc=nsc),
        grid=(batch, h),
        in_specs=[tile(0), tile(h), tile(2 * h), tile(3 * h),
                  pl.BlockSpec((1, nsc, GDN_SUPER, 8), lambda b, j: (j, b, 0, 0)),
                  pl.BlockSpec((1, nsc, 8, GDN_SUPER), lambda b, j: (j, b, 0, 0)),
                  st, pl.BlockSpec((1, LANES), lambda b, j: (0, 0))],
        out_specs=[tile(0), st],
        out_shape=[jax.ShapeDtypeStruct((batch, nsc, GDN_SUPER, h * LANES), BF16),
                   jax.ShapeDtypeStruct((batch, 2, h, GDN_DK, LANES), F32)],
        scratch_shapes=[pltpu.VMEM((nsc, GDN_SUPER, LANES), F32), pltpu.VMEM((nsc, GDN_SUPER, LANES), F32)],
        compiler_params=_params("parallel", "parallel"),
        name="gdn_scan",
    )(qkv4, qkv4, qkv4, proj4, cols, rows, s0, norm_g.reshape(1, LANES))


def _gdn_mixer(x, g, mod, rows_per_mod, batch, seq, s0, w_cat, conv_w, a_log, dt_bias, norm_g):
    proj = _nmm(x, g, mod, w_cat, rows_per_mod, 0, 1)
    qkv = _gdn_prep(proj.reshape(batch, seq, proj.shape[-1]), conv_w)
    gates, tot = _gdn_gates(proj, a_log, dt_bias)
    h = GDN_HEADS
    n = batch * seq
    per = jnp.stack([gates[:, 0:h], gates[:, h:2 * h], tot[:, 0:h],
                     gates[:, 2 * h:3 * h], gates[:, 3 * h:4 * h], tot[:, 2 * h:3 * h],
                     jnp.zeros((n, h), F32), jnp.zeros((n, h), F32)], axis=-1)
    per = per.reshape(n // GDN_SUPER, GDN_SUPER, h, 8)
    cols = jnp.transpose(per, (2, 0, 1, 3))
    rows = jnp.transpose(per, (2, 0, 3, 1))
    o, s_fin = _gdn_scan(qkv, proj, cols, rows, s0, norm_g, batch, seq)
    return o.reshape(n, D_MODEL), s_fin


def _moe_pre_kernel(x_ref, g_ref, mod_ref, wr_ref, h_ref, aff_ref):
    h = _norm_mod(x_ref[...], g_ref[...], mod_ref, 3, 4)
    h_ref[...] = h
    h1, h2 = _split_bf16(h, 2)
    logits = _dot(h1, wr_ref[0]) + (_dot(h1, wr_ref[1]) + _dot(h2, wr_ref[0]))
    e = jnp.exp(logits - jnp.max(logits, axis=-1, keepdims=True))
    aff_ref[...] = e / jnp.sum(e, axis=-1, keepdims=True)


def _moe_pre(x, g, mod, rows_per_mod, w_router, tm=512):
    m = x.shape[0]
    w1 = w_router.astype(BF16)
    w2 = (w_router - w1.astype(F32)).astype(BF16)
    return pl.pallas_call(
        _moe_pre_kernel,
        grid=(m // tm,),
        in_specs=[pl.BlockSpec((tm, D_MODEL), lambda i: (i, 0)),
                  pl.BlockSpec((1, D_MODEL), lambda i: (0, 0)),
                  pl.BlockSpec((1, 6, D_MODEL), lambda i: ((i * tm) // rows_per_mod, 0, 0)),
                  pl.BlockSpec((2, D_MODEL, N_EXPERTS), lambda i: (0, 0, 0))],
        out_specs=[pl.BlockSpec((tm, D_MODEL), lambda i: (i, 0)),
                   pl.BlockSpec((tm, N_EXPERTS), lambda i: (i, 0))],
        out_shape=[jax.ShapeDtypeStruct((m, D_MODEL), F32), jax.ShapeDtypeStruct((m, N_EXPERTS), F32)],
        compiler_params=_params("parallel"),
        name="moe_router",
    )(x, g.reshape(1, D_MODEL), mod, jnp.stack([w1, w2]))


def _ffn_kernel(pk_ref, h_hbm, gate_ref, wg_ref, wu_ref, wd_ref, z_hbm,
                xbuf, ybuf, wg_s, wu_s, wd_s, pk_s, gsem, ssem, psem, *, tc, steps_per_e, nsteps, n_tok):
    k = pl.program_id(0) * steps_per_e + pl.program_id(1)
    slot = k % 2
    other = 1 - slot

    def load_pk(step, sl):
        cp = pltpu.make_async_copy(pk_ref.at[step], pk_s.at[sl], psem)
        cp.start()
        cp.wait()

    def row_in(row, sl, i):
        return pltpu.make_async_copy(h_hbm.at[pl.ds(row, 1)], xbuf.at[sl, pl.ds(i, 1)], gsem.at[sl])

    def row_out(dst, sl, i):
        return pltpu.make_async_copy(ybuf.at[sl, pl.ds(i, 1)], z_hbm.at[pl.ds(dst, 1)], ssem.at[sl])

    def for_rows(fn):
        def body(i, c):
            fn(i)
            return c
        lax.fori_loop(0, tc, body, 0, unroll=8)

    def gather(sl):
        for_rows(lambda i: row_in(pk_s[sl, i] & 0xFFFF, sl, i).start())

    def scatter(sl):
        def one(i):
            v = pk_s[sl, i]
            row_out((v >> 16) * n_tok + (v & 0xFFFF), sl, i).start()
        for_rows(one)

    @pl.when(k == 0)
    def _():
        load_pk(0, 0)
        gather(0)

    @pl.when(k + 1 < nsteps)
    def _():
        load_pk(k + 1, other)
        gather(other)

    @pl.when(pl.program_id(1) == 0)
    def _():
        wg_s[...] = wg_ref[0].astype(BF16)
        wu_s[...] = wu_ref[0].astype(BF16)
        wd_s[...] = wd_ref[0].astype(BF16)

    for_rows(lambda i: row_in(0, slot, 0).wait())
    x = xbuf[slot].astype(BF16)
    half = wg_s.shape[1] // 2
    acc = None
    for c0 in (0, half):
        a = _dot(x, wg_s[:, c0:c0 + half])
        b = _dot(x, wu_s[:, c0:c0 + half])
        part = _dot((_silu(a) * b).astype(BF16), wd_s[c0:c0 + half, :])
        acc = part if acc is None else acc + part

    @pl.when(k >= 2)
    def _():
        for_rows(lambda i: row_out(0, slot, 0).wait())

    ybuf[slot] = acc * gate_ref[0]
    scatter(slot)

    @pl.when(k == nsteps - 1)
    def _():
        for_rows(lambda i: row_out(0, slot, 0).wait())
        if nsteps > 1:
            for_rows(lambda i: row_out(0, other, 0).wait())


def _expert_ffn(h, packed, gate, w_gate, w_up, w_down, max_rank, tc=512):
    n_tok = h.shape[0]
    n_exp, cap = packed.shape
    ff = w_gate.shape[-1]
    tc = min(tc, cap)
    steps_per_e = cap // tc
    nsteps = n_exp * steps_per_e
    wspec_in = pl.BlockSpec((1, D_MODEL, ff), lambda e, t: (e, 0, 0))
    z = pl.pallas_call(
        functools.partial(_ffn_kernel, tc=tc, steps_per_e=steps_per_e, nsteps=nsteps, n_tok=n_tok),
        grid=(n_exp, steps_per_e),
        in_specs=[pl.BlockSpec((nsteps, tc), lambda e, t: (0, 0)),
                  pl.BlockSpec(memory_space=pl.ANY),
                  pl.BlockSpec((1, tc, 1), lambda e, t: (e, t, 0)),
                  wspec_in, wspec_in,
                  pl.BlockSpec((1, ff, D_MODEL), lambda e, t: (e, 0, 0))],
        out_specs=pl.BlockSpec(memory_space=pl.ANY),
        out_shape=jax.ShapeDtypeStruct((max_rank * n_tok, D_MODEL), F32),
        scratch_shapes=[pltpu.VMEM((2, tc, D_MODEL), F32), pltpu.VMEM((2, tc, D_MODEL), F32),
                        pltpu.VMEM((D_MODEL, ff), BF16), pltpu.VMEM((D_MODEL, ff), BF16),
                        pltpu.VMEM((ff, D_MODEL), BF16),
                        pltpu.SMEM((2, tc), jnp.int32),
                        pltpu.SemaphoreType.DMA((2,)), pltpu.SemaphoreType.DMA((2,)),
                        pltpu.SemaphoreType.DMA(())],
        compiler_params=_params("arbitrary", "arbitrary"),
        name="expert_ffn",
    )(packed.reshape(nsteps, tc), h, gate.reshape(n_exp, cap, 1), w_gate, w_up, w_down)
    return z.reshape(max_rank, n_tok, D_MODEL)


def _combine_kernel(mc_ref, x_ref, cnt_ref, mod_ref, fg_ref, z_hbm, o_ref, zbuf, acc, sem, *, gi, final, tt):
    i = pl.program_id(0)
    mc = mc_ref[i]

    def plane(r, sl):
        return pltpu.make_async_copy(z_hbm.at[r, pl.ds(i * tt, tt)], zbuf.at[sl], sem.at[sl])

    @pl.when(mc > 0)
    def _():
        plane(0, 0).start()

    acc[...] = jnp.zeros_like(acc)
    cnt = cnt_ref[...]

    def body(r, c):
        sl = r % 2

        @pl.when(r + 1 < mc)
        def _():
            plane(r + 1, 1 - sl).start()

        plane(r, sl).wait()
        acc[...] += jnp.where(cnt > r, zbuf[sl], 0.0)
        return c

    lax.fori_loop(0, mc, body, 0)
    x = x_ref[...] + mod_ref[0, gi:gi + 1, :] * acc[...]
    if final:
        x = _rms(x, fg_ref[...])
    o_ref[...] = x


def _combine(x, z, cnt, maxcnt, mod, rows_per_mod, gi, final_g, final, tt=256):
    m = x.shape[0]
    return pl.pallas_call(
        functools.partial(_combine_kernel, gi=gi, final=final, tt=tt),
        grid_spec=pltpu.PrefetchScalarGridSpec(
            num_scalar_prefetch=1,
            grid=(m // tt,),
            in_specs=[pl.BlockSpec((tt, D_MODEL), lambda i, mc: (i, 0)),
                      pl.BlockSpec((tt, 1), lambda i, mc: (i, 0)),
                      pl.BlockSpec((1, 6, D_MODEL), lambda i, mc: ((i * tt) // rows_per_mod, 0, 0)),
                      pl.BlockSpec((1, D_MODEL), lambda i, mc: (0, 0)),
                      pl.BlockSpec(memory_space=pl.ANY)],
            out_specs=pl.BlockSpec((tt, D_MODEL), lambda i, mc: (i, 0)),
            scratch_shapes=[pltpu.VMEM((2, tt, D_MODEL), F32), pltpu.VMEM((tt, D_MODEL), F32),
                            pltpu.SemaphoreType.DMA((2,))]),
        out_shape=jax.ShapeDtypeStruct((m, D_MODEL), F32),
        compiler_params=_params("arbitrary"),
        name="moe_combine",
    )(maxcnt, x, cnt.reshape(m, 1), mod, final_g.reshape(1, D_MODEL), z)


def _expert_choice(x, g, mod, rows_per_mod, w_router, w_gate, w_up, w_down, gi, final_g, final):
    n = x.shape[0]
    cap = EC_CAPACITY_FACTOR * n // N_EXPERTS
    tt = min(256, n)
    h, aff = _moe_pre(x, g, mod, rows_per_mod, w_router)
    gate, idx = lax.top_k(aff.T, cap)
    mask = jnp.zeros((N_EXPERTS, n), jnp.int32).at[jnp.arange(N_EXPERTS)[:, None], idx].set(1)
    rank = jnp.cumsum(mask, axis=0) - mask
    packed = (jnp.take_along_axis(rank, idx, axis=1) << 16) | idx
    cnt = jnp.sum(mask, axis=0)
    maxcnt = jnp.max(cnt.reshape(n // tt, tt), axis=1)
    z = _expert_ffn(h, packed, gate, w_gate, w_up, w_down, N_EXPERTS)
    return _combine(x, z, cnt, maxcnt, mod, rows_per_mod, gi, final_g, final, tt)


def _trunk(x, mods, rows_per_mod, batch, seq, caches, p):
    depth = p["mod_w"].shape[0]
    n = batch * seq
    new = {}
    for layer in range(depth):
        kind, slot = layer % 4, layer // 4
        mod = mods[layer]
        g_mix = p["norm_mix_g"][layer]
        if kind == 0:
            w_ab = jnp.pad(p["gdn_w_ab"][slot], ((0, 0), (0, LANES - 4 * GDN_HEADS)))
            w_cat = jnp.concatenate([p["gdn_w_in"][slot], w_ab], axis=1).astype(BF16)
            if caches is None:
                s0 = jnp.zeros((batch, 2, GDN_HEADS, GDN_DK, LANES), F32)
            else:
                s0 = caches["state_gdn"][:, slot]
            o, s_fin = _gdn_mixer(x, g_mix, mod, rows_per_mod, batch, seq, s0, w_cat, p["gdn_conv_w"][slot],
                                  p["gdn_a_log"][slot], p["gdn_dt_bias"][slot], p["gdn_norm_g"][slot])
            new.setdefault("gdn", []).append(s_fin)
            w_out = p["gdn_w_out"][slot]
        elif kind == 1:
            proj = _nmm(x, g_mix, mod, p["diff_w_in"][slot].astype(BF16), rows_per_mod, 0, 1)
            q, k, v = _attn_prep(proj, "diff", seq, caches is not None)
            nk = seq
            if caches is None:
                hw = DIFF_HEADS * LANES
                new.setdefault("diff_k", []).append(proj[:, hw:2 * hw].reshape(batch, seq, DIFF_HEADS, LANES))
                new.setdefault("diff_v", []).append(proj[:, 2 * hw:].reshape(batch, seq, DIFF_HEADS, LANES))
            else:
                kc = caches["cache_diff_k"][:, slot].reshape(batch, -1, DIFF_HEADS * LANES).astype(BF16)
                vc = caches["cache_diff_v"][:, slot].reshape(batch, -1, DIFF_HEADS * LANES).astype(BF16)
                nk = seq + kc.shape[1]
                k = jnp.concatenate([k.reshape(batch, seq, -1), kc], axis=1).reshape(batch * nk, -1)
                v = jnp.concatenate([v.reshape(batch, seq, -1), vc], axis=1).reshape(batch * nk, -1)
            lam_init = 0.8 - 0.6 * math.exp(-0.3 * layer)
            o = _attention(q, k, v, "diff", batch, seq, nk,
                           (p["diff_lambda"][slot], p["diff_subln_g"][slot]), lam_init)
            w_out = p["diff_w_out"][slot]
        elif kind == 2:
            proj = _nmm(x, g_mix, mod, p["gqa_w_in"][slot].astype(BF16), rows_per_mod, 0, 1)
            q, k, v, kf = _attn_prep(proj, "gqa", seq, caches is not None,
                                     (p["gqa_q_norm_g"][slot], p["gqa_k_norm_g"][slot]))
            nk = seq
            if caches is None:
                v0 = (GQA_HEADS + GQA_KV) * LANES
                new.setdefault("gqa_k", []).append(kf.reshape(batch, seq, GQA_KV, GQA_HD))
                new.setdefault("gqa_v", []).append(proj[:, v0:].reshape(batch, seq, GQA_KV, GQA_HD))
            else:
                kc = caches["cache_gqa_k"][:, slot].reshape(batch, -1, GQA_KV * GQA_HD).astype(BF16)
                vc = caches["cache_gqa_v"][:, slot].reshape(batch, -1, GQA_KV * GQA_HD).astype(BF16)
                nk = seq + kc.shape[1]
                k = jnp.concatenate([k.reshape(batch, seq, -1), kc], axis=1).reshape(batch * nk, -1)
                v = jnp.concatenate([v.reshape(batch, seq, -1), vc], axis=1).reshape(batch * nk, -1)
            o = _attention(q, k, v, "gqa", batch, seq, nk)
            w_out = p["gqa_w_out"][slot]
        else:
            proj = _nmm(x, g_mix, mod, p["swa_w_in"][slot].astype(BF16), rows_per_mod, 0, 1)
            q, k, v = _attn_prep(proj, "swa", seq, caches is not None)
            sinks = p["swa_sinks"][slot].astype(F32)
            if caches is None:
                k0 = SWA_HEADS * SWA_HD
                v0 = k0 + SWA_KV * SWA_HD
                new.setdefault("swa_k", []).append(proj[:, k0:v0].reshape(batch, seq, SWA_KV, SWA_HD))
                new.setdefault("swa_v", []).append(proj[:, v0:].reshape(batch, seq, SWA_KV, SWA_HD))
                o = _attention(q, k, v, "swa", batch, seq, seq, (sinks,))
            else:
                kc = jnp.pad(caches["cache_swa_k"][:, slot], ((0, 0), (0, 0), (0, 0), (0, LANES - SWA_HD)))
                past = kc.shape[1]
                kc = kc.reshape(batch, past, SWA_KV * LANES).astype(BF16)
                vc = caches["cache_swa_v"][:, slot].reshape(batch, past, SWA_KV * SWA_HD).astype(BF16)
                nk = seq + past
                k = jnp.concatenate([k.reshape(batch, seq, -1), kc], axis=1).reshape(batch * nk, -1)
                v = jnp.concatenate([v.reshape(batch, seq, -1), vc], axis=1).reshape(batch * nk, -1)
                o = _band_attention(q, k, v, sinks, batch, seq, past)
            w_out = p["swa_w_out"][slot]
        x = _mm_resid(o, w_out.astype(BF16), x, mod, rows_per_mod, 2)
        x = _expert_choice(x, p["norm_ffn_g"][layer], mod, rows_per_mod, p["moe_router"][layer],
                           p["moe_w_gate"][layer], p["moe_w_up"][layer], p["moe_w_down"][layer],
                           5, p["final_g"], layer == depth - 1)
    return x, new


def kernel(x_prompt, x_sample, state_gdn, cache_diff_k, cache_diff_v, cache_gqa_k, cache_gqa_v, cache_swa_k, cache_swa_v, c, c_ctx, mod_w, mod_b, norm_mix_g, norm_ffn_g, final_g, gdn_w_in, gdn_conv_w, gdn_w_ab, gdn_a_log, gdn_dt_bias, gdn_norm_g, gdn_w_out, diff_w_in, diff_lambda, diff_subln_g, diff_w_out, gqa_w_in, gqa_q_norm_g, gqa_k_norm_g, gqa_w_out, swa_w_in, swa_sinks, swa_w_out, moe_router, moe_w_gate, moe_w_up, moe_w_down):
    p = dict(mod_w=mod_w, mod_b=mod_b, norm_mix_g=norm_mix_g, norm_ffn_g=norm_ffn_g, final_g=final_g,
             gdn_w_in=gdn_w_in, gdn_conv_w=gdn_conv_w, gdn_w_ab=gdn_w_ab, gdn_a_log=gdn_a_log,
             gdn_dt_bias=gdn_dt_bias, gdn_norm_g=gdn_norm_g, gdn_w_out=gdn_w_out,
             diff_w_in=diff_w_in, diff_lambda=diff_lambda, diff_subln_g=diff_subln_g, diff_w_out=diff_w_out,
             gqa_w_in=gqa_w_in, gqa_q_norm_g=gqa_q_norm_g, gqa_k_norm_g=gqa_k_norm_g, gqa_w_out=gqa_w_out,
             swa_w_in=swa_w_in, swa_sinks=swa_sinks, swa_w_out=swa_w_out,
             moe_router=moe_router, moe_w_gate=moe_w_gate, moe_w_up=moe_w_up, moe_w_down=moe_w_down)
    batch, seq, _ = x_prompt.shape
    dec_batch, dec_seq, _ = x_sample.shape
    depth = mod_w.shape[0]
    rows = 8 * ((1 + dec_batch + 7) // 8)
    cond = jnp.zeros((rows, D_MODEL), F32).at[0].set(c_ctx).at[1:1 + dec_batch].set(c)
    mod_all = _mod_proj(cond, mod_w, mod_b).reshape(depth, rows, 6, D_MODEL)
    mods_ctx = [mod_all[l, 0:1] for l in range(depth)]
    mods_lat = [mod_all[l, 1:1 + dec_batch] for l in range(depth)]

    y_prompt, st = _trunk(x_prompt.reshape(batch * seq, D_MODEL), mods_ctx, batch * seq, batch, seq, None, p)
    caches = dict(state_gdn=state_gdn, cache_diff_k=cache_diff_k, cache_diff_v=cache_diff_v,
                  cache_gqa_k=cache_gqa_k, cache_gqa_v=cache_gqa_v,
                  cache_swa_k=cache_swa_k, cache_swa_v=cache_swa_v)
    y_sample, _ = _trunk(x_sample.reshape(dec_batch * dec_seq, D_MODEL), mods_lat, dec_seq, dec_batch, dec_seq,
                         caches, p)

    def stack(xs):
        return jnp.stack(xs, axis=1)

    return (y_prompt.reshape(batch, seq, D_MODEL), y_sample.reshape(dec_batch, dec_seq, D_MODEL),
            stack(st["gdn"]), stack(st["diff_k"]), stack(st["diff_v"]),
            stack(st["gqa_k"]), stack(st["gqa_v"]), stack(st["swa_k"]), stack(st["swa_v"]))
```

```python
import functools
import math

import jax
import jax.numpy as jnp
from jax import lax
from jax.experimental import pallas as pl
from jax.experimental.pallas import tpu as pltpu

F32 = jnp.float32
BF16 = jnp.bfloat16

D_MODEL = 1024
GRID_W = 64
ROPE_THETA = 10000.0
NORM_EPS = 1e-6
LANES = 128
GDN_HEADS = 8
GDN_DK = 128
GDN_CHUNK = 64
GDN_SUPER = 256
DIFF_HEADS = 8
DIFF_DH = 64
GQA_HEADS = 8
GQA_KV = 2
GQA_HD = 128
SWA_HEADS = 16
SWA_KV = 4
SWA_HD = 64
WINDOW = 128
N_EXPERTS = 16
EC_CAPACITY_FACTOR = 2
VMEM_LIMIT_BYTES = 56 * 1024 * 1024


def _params(*sem):
    return pltpu.CompilerParams(dimension_semantics=sem, vmem_limit_bytes=VMEM_LIMIT_BYTES)


def _silu(x):
    return x / (1.0 + jnp.exp(-x))


def _rms(x, g):
    return x * lax.rsqrt(jnp.mean(x * x, axis=-1, keepdims=True) + NORM_EPS) * g


def _dot(a, b):
    return jnp.dot(a, b, preferred_element_type=F32)


def _dot_nt(a, b):
    return lax.dot_general(a, b, (((1,), (1,)), ((), ())), preferred_element_type=F32)


def _split_bf16(x, parts):
    out = []
    for _ in range(parts - 1):
        hi = x.astype(BF16)
        out.append(hi)
        x = x - hi.astype(F32)
    out.append(x.astype(BF16))
    return out


def _mod_kernel(c_ref, w_ref, b_ref, o_ref):
    a = _silu(c_ref[...]).astype(BF16)
    o_ref[0] = _dot(a, w_ref[0].astype(BF16)) + b_ref[0]


def _mod_proj(cond, mod_w, mod_b):
    depth, _, width = mod_w.shape
    rows = cond.shape[0]
    tn = 1536
    return pl.pallas_call(
        _mod_kernel,
        grid=(depth, width // tn),
        in_specs=[pl.BlockSpec((rows, D_MODEL), lambda l, j: (0, 0)),
                  pl.BlockSpec((1, D_MODEL, tn), lambda l, j: (l, 0, j)),
                  pl.BlockSpec((1, 1, tn), lambda l, j: (l, 0, j))],
        out_specs=pl.BlockSpec((1, rows, tn), lambda l, j: (l, 0, j)),
        out_shape=jax.ShapeDtypeStruct((depth, rows, width), F32),
        compiler_params=_params("parallel", "parallel"),
        name="mod_proj",
    )(cond, mod_w, mod_b.reshape(depth, 1, width))


def _norm_mod(x, g, mod_ref, sh, sc):
    return _rms(x, g) * (1.0 + mod_ref[0, sc:sc + 1, :]) + mod_ref[0, sh:sh + 1, :]


def _nmm_kernel(x_ref, g_ref, mod_ref, w_ref, o_ref, *, sh, sc, chunk):
    h = _norm_mod(x_ref[...], g_ref[...], mod_ref, sh, sc).astype(BF16)
    width = w_ref.shape[1]
    for c0 in range(0, width, chunk):
        c1 = min(c0 + chunk, width)
        o_ref[:, c0:c1] = _dot(h, w_ref[:, c0:c1])


def _nmm(x, g, mod, w, rows_per_mod, sh, sc, tm=256):
    m = x.shape[0]
    width = w.shape[1]
    return pl.pallas_call(
        functools.partial(_nmm_kernel, sh=sh, sc=sc, chunk=512),
        grid=(m // tm,),
        in_specs=[pl.BlockSpec((tm, D_MODEL), lambda i: (i, 0)),
                  pl.BlockSpec((1, D_MODEL), lambda i: (0, 0)),
                  pl.BlockSpec((1, 6, D_MODEL), lambda i: ((i * tm) // rows_per_mod, 0, 0)),
                  pl.BlockSpec((D_MODEL, width), lambda i: (0, 0))],
        out_specs=pl.BlockSpec((tm, width), lambda i: (i, 0)),
        out_shape=jax.ShapeDtypeStruct((m, width), F32),
        compiler_params=_params("parallel"),
        name="norm_mod_matmul",
    )(x, g.reshape(1, D_MODEL), mod, w)


def _mmr_kernel(a_ref, w_ref, r_ref, mod_ref, o_ref, *, gi):
    acc = _dot(a_ref[...].astype(BF16), w_ref[...])
    o_ref[...] = r_ref[...] + mod_ref[0, gi:gi + 1, :] * acc


def _mm_resid(a, w, res, mod, rows_per_mod, gi, tm=512):
    m = a.shape[0]
    return pl.pallas_call(
        functools.partial(_mmr_kernel, gi=gi),
        grid=(m // tm,),
        in_specs=[pl.BlockSpec((tm, D_MODEL), lambda i: (i, 0)),
                  pl.BlockSpec((D_MODEL, D_MODEL), lambda i: (0, 0)),
                  pl.BlockSpec((tm, D_MODEL), lambda i: (i, 0)),
                  pl.BlockSpec((1, 6, D_MODEL), lambda i: ((i * tm) // rows_per_mod, 0, 0))],
        out_specs=pl.BlockSpec((tm, D_MODEL), lambda i: (i, 0)),
        out_shape=jax.ShapeDtypeStruct((m, D_MODEL), F32),
        compiler_params=_params("parallel"),
        name="out_proj_resid",
    )(a, w, res, mod)


def _rope_tables(n_tokens, dim):
    rows = n_tokens // GRID_W
    row = jnp.broadcast_to(jnp.arange(rows, dtype=F32)[:, None], (rows, GRID_W)).reshape(-1)
    col = jnp.broadcast_to(jnp.arange(GRID_W, dtype=F32)[None, :], (rows, GRID_W)).reshape(-1)
    n_freq = dim // 4
    inv = ROPE_THETA ** (-jnp.arange(n_freq, dtype=F32) / n_freq)
    ang = jnp.concatenate([row[:, None] * inv, col[:, None] * inv], axis=-1)
    cos, sin = jnp.cos(ang), jnp.sin(ang)
    zero = jnp.zeros_like(sin)
    reps = LANES // dim
    cos_t = jnp.tile(jnp.concatenate([cos, cos], axis=-1), (1, reps))
    sin_a = jnp.tile(jnp.concatenate([-sin, zero], axis=-1), (1, reps))
    sin_b = jnp.tile(jnp.concatenate([zero, sin], axis=-1), (1, reps))
    return cos_t, sin_a, sin_b


def _rope(x, cos, sin_a, sin_b, dim):
    half = dim // 2
    return x * cos + pltpu.roll(x, LANES - half, 1) * sin_a + pltpu.roll(x, half, 1) * sin_b


def _prep_kernel(*refs, kind, rope):
    if kind == "gqa":
        p_ref, cos_ref, sa_ref, sb_ref, qg_ref, kg_ref, q_out, k_out, v_out, kf_out = refs
    else:
        p_ref, cos_ref, sa_ref, sb_ref, q_out, k_out, v_out = refs
    cos, sa, sb = cos_ref[...], sa_ref[...], sb_ref[...]

    def blk(j):
        return p_ref[:, j * LANES:(j + 1) * LANES]

    if kind == "gqa":
        scale = GQA_HD ** -0.5
        for j in range(GQA_HEADS):
            y = _rms(blk(j), qg_ref[...])
            if rope:
                y = _rope(y, cos, sa, sb, GQA_HD)
            q_out[:, j * LANES:(j + 1) * LANES] = (y * scale).astype(BF16)
        for j in range(GQA_KV):
            y = _rms(blk(GQA_HEADS + j), kg_ref[...])
            kf_out[:, j * LANES:(j + 1) * LANES] = y
            if rope:
                y = _rope(y, cos, sa, sb, GQA_HD)
            k_out[:, j * LANES:(j + 1) * LANES] = y.astype(BF16)
        v0 = (GQA_HEADS + GQA_KV) * LANES
        v_out[...] = p_ref[:, v0:v0 + GQA_KV * LANES].astype(BF16)
    elif kind == "diff":
        scale = DIFF_DH ** -0.5
        for j in range(DIFF_HEADS):
            y = blk(j)
            if rope:
                y = _rope(y, cos, sa, sb, DIFF_DH)
            q_out[:, j * LANES:(j + 1) * LANES] = (y * scale).astype(BF16)
            y = blk(DIFF_HEADS + j)
            if rope:
                y = _rope(y, cos, sa, sb, DIFF_DH)
            k_out[:, j * LANES:(j + 1) * LANES] = y.astype(BF16)
        v_out[...] = p_ref[:, 2 * DIFF_HEADS * LANES:3 * DIFF_HEADS * LANES].astype(BF16)
    else:
        scale = SWA_HD ** -0.5
        low = lax.broadcasted_iota(jnp.int32, cos.shape, 1) < SWA_HD

        def pad_pair(y, out, j):
            out[:, (2 * j) * LANES:(2 * j + 1) * LANES] = jnp.where(low, y, 0.0).astype(BF16)
            out[:, (2 * j + 1) * LANES:(2 * j + 2) * LANES] = jnp.where(
                low, pltpu.roll(y, SWA_HD, 1), 0.0).astype(BF16)

        nq = SWA_HEADS // 2
        for j in range(nq):
            y = blk(j)
            if rope:
                y = _rope(y, cos, sa, sb, SWA_HD)
            pad_pair(y * scale, q_out, j)
        for j in range(SWA_KV // 2):
            y = blk(nq + j)
            if rope:
                y = _rope(y, cos, sa, sb, SWA_HD)
            pad_pair(y, k_out, j)
        v0 = (nq + SWA_KV // 2) * LANES
        v_out[...] = p_ref[:, v0:v0 + (SWA_KV // 2) * LANES].astype(BF16)


def _attn_prep(proj, kind, seq, rope, gains=None, tm=256):
    m, width = proj.shape
    dim = GQA_HD if kind == "gqa" else DIFF_DH
    cos, sa, sb = _rope_tables(seq, dim)
    nt = seq // tm
    if kind == "gqa":
        wq, wk, wv = GQA_HEADS * LANES, GQA_KV * LANES, GQA_KV * LANES
    elif kind == "diff":
        wq, wk, wv = DIFF_HEADS * LANES, DIFF_HEADS * LANES, DIFF_HEADS * LANES
    else:
        wq, wk, wv = SWA_HEADS * LANES, SWA_KV * LANES, (SWA_KV // 2) * LANES
    tab = pl.BlockSpec((tm, LANES), lambda i: (i % nt, 0))
    in_specs = [pl.BlockSpec((tm, width), lambda i: (i, 0)), tab, tab, tab]
    args = [proj, cos, sa, sb]
    out_shapes = [jax.ShapeDtypeStruct((m, wq), BF16), jax.ShapeDtypeStruct((m, wk), BF16),
                  jax.ShapeDtypeStruct((m, wv), BF16)]
    out_specs = [pl.BlockSpec((tm, wq), lambda i: (i, 0)), pl.BlockSpec((tm, wk), lambda i: (i, 0)),
                 pl.BlockSpec((tm, wv), lambda i: (i, 0))]
    if kind == "gqa":
        gspec = pl.BlockSpec((1, LANES), lambda i: (0, 0))
        in_specs += [gspec, gspec]
        args += [gains[0].reshape(1, LANES), gains[1].reshape(1, LANES)]
        out_shapes.append(jax.ShapeDtypeStruct((m, wk), F32))
        out_specs.append(pl.BlockSpec((tm, wk), lambda i: (i, 0)))
    return pl.pallas_call(
        functools.partial(_prep_kernel, kind=kind, rope=rope),
        grid=(m // tm,),
        in_specs=in_specs,
        out_specs=out_specs,
        out_shape=out_shapes,
        compiler_params=_params("parallel"),
        name="attn_prep_" + kind,
    )(*args)


def _pick_half(o_a, o_b, upper):
    low = lax.broadcasted_iota(jnp.int32, o_a.shape, 1) < SWA_HD
    if upper:
        return jnp.where(low, pltpu.roll(o_a, SWA_HD, 1), o_b)
    return jnp.where(low, o_a, pltpu.roll(o_b, SWA_HD, 1))


def _attn_kernel(*refs, mode, lam_init):
    if mode == "gqa":
        q_ref, k_ref, v_ref, o_ref = refs
    elif mode == "diff":
        q_ref, k_ref, v_ref, lam_ref, g_ref, o_ref = refs
    else:
        q_ref, k_ref, v_ref, sink_ref, o_ref = refs
    v = v_ref[...]
    if mode == "gqa":
        k = k_ref[...]
        for j in range(GQA_HEADS // GQA_KV):
            s = _dot_nt(q_ref[:, j * LANES:(j + 1) * LANES], k)
            e = jnp.exp(s - jnp.max(s, axis=-1, keepdims=True))
            o = _dot(e.astype(BF16), v) / jnp.sum(e, axis=-1, keepdims=True)
            o_ref[:, j * LANES:(j + 1) * LANES] = o.astype(BF16)
    elif mode == "diff":
        k = k_ref[...]
        q = q_ref[...]
        low = lax.broadcasted_iota(jnp.int32, q.shape, 1) < DIFF_DH
        lv = lam_ref[...]
        lam = (jnp.exp(jnp.sum(lv[0:1] * lv[1:2], axis=-1, keepdims=True))
               - jnp.exp(jnp.sum(lv[2:3] * lv[3:4], axis=-1, keepdims=True)) + lam_init)
        s0 = _dot_nt(jnp.where(low, q, 0.0).astype(BF16), k)
        s1 = _dot_nt(jnp.where(low, 0.0, q).astype(BF16), k)
        e0 = jnp.exp(s0 - jnp.max(s0, axis=-1, keepdims=True))
        e1 = jnp.exp(s1 - jnp.max(s1, axis=-1, keepdims=True))
        a = e0 * (1.0 / jnp.sum(e0, axis=-1, keepdims=True)) - e1 * (lam / jnp.sum(e1, axis=-1, keepdims=True))
        o = _dot(a.astype(BF16), v)
        o_ref[...] = (_rms(o, g_ref[...]) * (1.0 - lam_init)).astype(BF16)
    else:
        heads = SWA_HEADS // (SWA_KV // 2)
        group = SWA_HEADS // SWA_KV
        outs = []
        for j in range(heads):
            a = j // group
            s = _dot_nt(q_ref[:, j * LANES:(j + 1) * LANES], k_ref[:, a * LANES:(a + 1) * LANES])
            sink = sink_ref[0, :, j:j + 1]
            mx = jnp.maximum(jnp.max(s, axis=-1, keepdims=True), sink)
            e = jnp.exp(s - mx)
            den = jnp.sum(e, axis=-1, keepdims=True) + jnp.exp(sink - mx)
            outs.append(_dot(e.astype(BF16), v) / den)
        for jj in range(heads // 2):
            a = (2 * jj) // group
            o_ref[:, jj * LANES:(jj + 1) * LANES] = _pick_half(outs[2 * jj], outs[2 * jj + 1], a == 1).astype(BF16)


def _attention(q, k, v, mode, batch, seq, n_keys, extra=(), lam_init=0.0, tq=256):
    nq = seq // tq
    if mode == "gqa":
        groups, wq, wk, wv, wo = GQA_KV, (GQA_HEADS // GQA_KV) * LANES, LANES, LANES, (GQA_HEADS // GQA_KV) * LANES
    elif mode == "diff":
        groups, wq, wk, wv, wo = DIFF_HEADS, LANES, LANES, LANES, LANES
    else:
        groups, wq, wk, wv, wo = SWA_KV // 2, SWA_HEADS // 2 * LANES, 2 * LANES, LANES, 4 * LANES
    in_specs = [pl.BlockSpec((tq, wq), lambda b, g, i: (b * nq + i, g)),
                pl.BlockSpec((n_keys, wk), lambda b, g, i: (b, g)),
                pl.BlockSpec((n_keys, wv), lambda b, g, i: (b, g))]
    args = [q, k, v]
    if mode == "diff":
        lam_vecs, subln_g = extra
        in_specs += [pl.BlockSpec((4, DIFF_DH), lambda b, g, i: (0, 0)),
                     pl.BlockSpec((1, LANES), lambda b, g, i: (0, 0))]
        args += [lam_vecs, subln_g.reshape(1, LANES)]
    elif mode == "swa":
        (sinks,) = extra
        in_specs.append(pl.BlockSpec((1, 1, SWA_HEADS // 2), lambda b, g, i: (g, 0, 0)))
        args.append(sinks.reshape(2, 1, SWA_HEADS // 2))
    return pl.pallas_call(
        functools.partial(_attn_kernel, mode=mode, lam_init=lam_init),
        grid=(batch, groups, nq),
        in_specs=in_specs,
        out_specs=pl.BlockSpec((tq, wo), lambda b, g, i: (b * nq + i, g)),
        out_shape=jax.ShapeDtypeStruct((batch * seq, D_MODEL), BF16),
        compiler_params=_params("parallel", "parallel", "parallel"),
        name="attn_" + mode,
    )(*args)


def _band_kernel(q_ref, kp_ref, kc_ref, kn_ref, kx_ref, vp_ref, vc_ref, vn_ref, vx_ref, sink_ref, o_ref, *, nqb):
    i = pl.program_id(2)
    qi = lax.broadcasted_iota(jnp.int32, (WINDOW, WINDOW), 0)
    kj = lax.broadcasted_iota(jnp.int32, (WINDOW, WINDOW), 1)
    m_prev = (kj >= qi) & (i > 0)
    m_next = (kj <= qi) & (i < nqb - 1)
    heads = SWA_HEADS // (SWA_KV // 2)
    group = SWA_HEADS // SWA_KV
    outs = []
    for j in range(heads):
        a = j // group
        q = q_ref[:, j * LANES:(j + 1) * LANES]
        ks = slice(a * LANES, (a + 1) * LANES)
        s_p = jnp.where(m_prev, _dot_nt(q, kp_ref[:, ks]), -jnp.inf)
        s_c = _dot_nt(q, kc_ref[:, ks])
        s_n = jnp.where(m_next, _dot_nt(q, kn_ref[:, ks]), -jnp.inf)
        s_x = _dot_nt(q, kx_ref[:, ks])
        sink = sink_ref[0, :, j:j + 1]
        mx = jnp.maximum(jnp.maximum(jnp.max(s_p, axis=-1, keepdims=True), jnp.max(s_c, axis=-1, keepdims=True)),
                         jnp.maximum(jnp.max(s_n, axis=-1, keepdims=True), jnp.max(s_x, axis=-1, keepdims=True)))
        mx = jnp.maximum(mx, sink)
        e_p, e_c, e_n, e_x = jnp.exp(s_p - mx), jnp.exp(s_c - mx), jnp.exp(s_n - mx), jnp.exp(s_x - mx)
        den = (jnp.sum(e_p, axis=-1, keepdims=True) + jnp.sum(e_c, axis=-1, keepdims=True)
               + jnp.sum(e_n, axis=-1, keepdims=True) + jnp.sum(e_x, axis=-1, keepdims=True) + jnp.exp(sink - mx))
        o = (_dot(e_p.astype(BF16), vp_ref[...]) + _dot(e_c.astype(BF16), vc_ref[...])
             + _dot(e_n.astype(BF16), vn_ref[...]) + _dot(e_x.astype(BF16), vx_ref[...]))
        outs.append(o / den)
    for jj in range(heads // 2):
        a = (2 * jj) // group
        o_ref[:, jj * LANES:(jj + 1) * LANES] = _pick_half(outs[2 * jj], outs[2 * jj + 1], a == 1).astype(BF16)


def _band_attention(q, k, v, sinks, batch, seq, past):
    nqb = seq // WINDOW
    nkb = (seq + past) // WINDOW
    pairs = SWA_KV // 2
    xb = past // WINDOW
    assert past % WINDOW == 0 and seq % past == 0

    def band(shift, width):
        return pl.BlockSpec((WINDOW, width),
                            lambda b, g, i: (b * nkb + jnp.clip(i + shift, 0, nqb - 1), g))

    def ctx(width):
        return pl.BlockSpec((past, width), lambda b, g, i: (b * (nkb // xb) + seq // past, g))

    return pl.pallas_call(
        functools.partial(_band_kernel, nqb=nqb),
        grid=(batch, pairs, nqb),
        in_specs=[pl.BlockSpec((WINDOW, SWA_HEADS // 2 * LANES), lambda b, g, i: (b * nqb + i, g)),
                  band(-1, 2 * LANES), band(0, 2 * LANES), band(1, 2 * LANES), ctx(2 * LANES),
                  band(-1, LANES), band(0, LANES), band(1, LANES), ctx(LANES),
                  pl.BlockSpec((1, 1, SWA_HEADS // 2), lambda b, g, i: (g, 0, 0))],
        out_specs=pl.BlockSpec((WINDOW, 4 * LANES), lambda b, g, i: (b * nqb + i, g)),
        out_shape=jax.ShapeDtypeStruct((batch * seq, D_MODEL), BF16),
        compiler_params=_params("parallel", "parallel", "parallel"),
        name="attn_band",
    )(q, k, k, k, k, v, v, v, v, sinks.reshape(2, 1, SWA_HEADS // 2))


def _gdn_prep_kernel(p_ref, w_ref, o_ref):
    x = p_ref[0]
    t = x.shape[0]
    w = w_ref[...]
    row = lax.broadcasted_iota(jnp.int32, x.shape, 0)
    xm = jnp.where(row == 0, 0.0, pltpu.roll(x, 1, 0))
    xp = jnp.where(row == t - 1, 0.0, pltpu.roll(x, t - 1, 0))
    y = _silu(xm * w[0:1] + x * w[1:2] + xp * w[2:3])
    rs = lax.rsqrt(jnp.sum(y * y, axis=-1, keepdims=True) + NORM_EPS)
    j = pl.program_id(1)
    is_q = (j < GDN_HEADS).astype(F32)
    normed = (j < 2 * GDN_HEADS).astype(F32)
    scale = is_q * (GDN_DK ** -0.5) + (1.0 - is_q)
    o_ref[0] = y * (rs * (scale * normed) + (1.0 - normed))


def _gdn_prep(proj3, conv_w):
    batch, seq, _ = proj3.shape
    nblk = 3 * GDN_HEADS
    return pl.pallas_call(
        _gdn_prep_kernel,
        grid=(batch, nblk),
        in_specs=[pl.BlockSpec((1, seq, LANES), lambda b, j: (b, 0, j)),
                  pl.BlockSpec((3, LANES), lambda b, j: (0, j))],
        out_specs=pl.BlockSpec((1, seq, LANES), lambda b, j: (b, 0, j)),
        out_shape=jax.ShapeDtypeStruct((batch, seq, nblk * LANES), F32),
        compiler_params=_params("parallel", "parallel"),
        name="gdn_conv",
    )(proj3, conv_w)


def _gdn_gate_kernel(p_ref, al_ref, dtb_ref, o_ref, t_ref):
    ab = p_ref[...]
    lane = lax.broadcasted_iota(jnp.int32, ab.shape, 1)
    is_g = (lane < 4 * GDN_HEADS) & ((lane & GDN_HEADS) == 0)
    is_fwd = lane < 2 * GDN_HEADS
    x = ab + dtb_ref[...]
    softplus = jnp.maximum(x, 0.0) + jnp.log(1.0 + jnp.exp(-jnp.abs(x)))
    g = jnp.where(is_g, -jnp.exp(al_ref[...]) * softplus, 0.0)
    beta = 1.0 / (1.0 + jnp.exp(-ab))
    ri = lax.broadcasted_iota(jnp.int32, (GDN_SUPER, GDN_SUPER), 0)
    ci = lax.broadcasted_iota(jnp.int32, (GDN_SUPER, GDN_SUPER), 1)
    same = (ri // GDN_CHUNK) == (ci // GDN_CHUNK)
    lower = jnp.where(same & (ri >= ci), 1.0, 0.0).astype(BF16)
    upper = jnp.where(same & (ri <= ci), 1.0, 0.0).astype(BF16)
    parts = _split_bf16(g, 3)
    pre = _dot(lower, parts[0]) + _dot(lower, parts[1]) + _dot(lower, parts[2])
    suf = _dot(upper, parts[0]) + _dot(upper, parts[1]) + _dot(upper, parts[2])
    o_ref[...] = jnp.where(is_g, jnp.where(is_fwd, pre, suf), beta)
    t_ref[...] = pre + suf - g


def _gdn_gates(proj, a_log, dt_bias):
    m, width = proj.shape
    col_blk = (width - LANES) // LANES
    pad = jnp.zeros((2, GDN_HEADS), F32)

    def row(v):
        return jnp.pad(jnp.stack([v.astype(F32), pad], axis=1).reshape(1, 4 * GDN_HEADS),
                       ((0, 0), (0, LANES - 4 * GDN_HEADS)))

    spec = pl.BlockSpec((GDN_SUPER, LANES), lambda i: (i, 0))
    vec = pl.BlockSpec((1, LANES), lambda i: (0, 0))
    return pl.pallas_call(
        _gdn_gate_kernel,
        grid=(m // GDN_SUPER,),
        in_specs=[pl.BlockSpec((GDN_SUPER, LANES), lambda i: (i, col_blk)), vec, vec],
        out_specs=[spec, spec],
        out_shape=[jax.ShapeDtypeStruct((m, LANES), F32), jax.ShapeDtypeStruct((m, LANES), F32)],
        compiler_params=_params("parallel"),
        name="gdn_gates",
    )(proj, row(a_log), row(dt_bias))


def _gdn_kernel(q_ref, k_ref, v_ref, z_ref, col_ref, row_ref, s0_ref, ng_ref, o_ref, sf_ref, of_s, ob_s, *, nsc, hps):
    n = GDN_SUPER
    ri = lax.broadcasted_iota(jnp.int32, (n, n), 0)
    ci = lax.broadcasted_iota(jnp.int32, (n, n), 1)
    b64 = (ri >> 6) == (ci >> 6)
    b32 = (ri >> 5) == (ci >> 5)
    b16 = (ri >> 4) == (ci >> 4)
    eye = jnp.where(ri == ci, 1.0, 0.0)
    causal = (b64 & (ri >= ci), b64 & (ri <= ci))
    strict = (b64 & (ri > ci), b64 & (ri < ci))
    nchunk = n // GDN_CHUNK

    def mm(a, b):
        return _dot(a.astype(BF16), b.astype(BF16))

    def lockstep(chains, states):
        cs = range(len(chains))
        dirs = [d for (_, d, _) in chains]
        lanes = [slice(hh * LANES, (hh + 1) * LANES) for (hh, _, _) in chains]
        q = [q_ref[0, sc, :, lanes[c]] for c, (_, _, sc) in enumerate(chains)]
        k = [k_ref[0, sc, :, lanes[c]] for c, (_, _, sc) in enumerate(chains)]
        v = [v_ref[0, sc, :, lanes[c]] for c, (_, _, sc) in enumerate(chains)]
        col = [col_ref[hh, sc] for (hh, _, sc) in chains]
        row = [row_ref[hh, sc] for (hh, _, sc) in chains]
        gcol = [col[c][:, 3 * dirs[c]:3 * dirs[c] + 1] for c in cs]
        bcol = [col[c][:, 3 * dirs[c] + 1:3 * dirs[c] + 2] for c in cs]
        tcol = [col[c][:, 3 * dirs[c] + 2:3 * dirs[c] + 3] for c in cs]
        grow = [row[c][3 * dirs[c]:3 * dirs[c] + 1, :] for c in cs]
        trow = [row[c][3 * dirs[c] + 2:3 * dirs[c] + 3, :] for c in cs]
        decay = [jnp.exp(jnp.where(causal[dirs[c]], gcol[c] - grow[c], -jnp.inf)) for c in cs]
        k16 = [k[c].astype(BF16) for c in cs]
        kb = [k[c] * bcol[c] for c in cs]
        kk = [_dot_nt(kb[c].astype(BF16), k16[c]) for c in cs]
        qk = [_dot_nt(q[c].astype(BF16), k16[c]) for c in cs]
        lmat = [jnp.where(strict[dirs[c]], kk[c] * decay[c], 0.0) for c in cs]
        intra = [(qk[c] * decay[c]).astype(BF16) for c in cs]
        l0 = [jnp.where(b16, lmat[c], 0.0) for c in cs]
        inv = [eye - l0[c] for c in cs]
        pw = [mm(l0[c], l0[c]) for c in cs]
        for rep in range(3):
            inv = [inv[c] + mm(inv[c], pw[c]) for c in cs]
            if rep < 2:
                pw = [mm(pw[c], pw[c]) for c in cs]
        for keep in (b32 & (~b16), ~b32):
            cm = [jnp.where(keep, lmat[c], 0.0) for c in cs]
            t = [mm(inv[c], cm[c]) for c in cs]
            inv = [inv[c] - mm(t[c], inv[c]) for c in cs]
        eg = [jnp.exp(gcol[c]) for c in cs]
        uw = [mm(inv[c], jnp.concatenate([v[c] * bcol[c], kb[c] * eg[c]], axis=1)) for c in cs]
        u = [uw[c][:, :LANES] for c in cs]
        w = [uw[c][:, LANES:].astype(BF16) for c in cs]
        qd = [(q[c] * eg[c]).astype(BF16) for c in cs]
        kdt = [jnp.transpose(k[c] * jnp.exp(tcol[c] - gcol[c])).astype(BF16) for c in cs]
        outs = [[None] * nchunk for _ in cs]
        states = list(states)
        for step in range(nchunk):
            js = [step if dirs[c] == 0 else nchunk - 1 - step for c in cs]
            rs = [slice(j * GDN_CHUNK, (j + 1) * GDN_CHUNK) for j in js]
            s16 = [states[c].astype(BF16) for c in cs]
            vn16 = [(u[c][rs[c]] - _dot(w[c][rs[c]], s16[c])).astype(BF16) for c in cs]
            qs = [_dot(qd[c][rs[c]], s16[c]) for c in cs]
            for c in cs:
                outs[c][js[c]] = qs[c] + _dot(intra[c][rs[c], rs[c]], vn16[c])
            states = [states[c] * jnp.exp(trow[c][:, js[c] * GDN_CHUNK:js[c] * GDN_CHUNK + 1])
                      + _dot(kdt[c][:, rs[c]], vn16[c]) for c in cs]
        return [jnp.concatenate(outs[c], axis=0) for c in cs], states

    def body(i, carry):
        chains = [(hh, d, i if d == 0 else nsc - 1 - i) for hh in range(hps) for d in range(2)]
        outs, states = lockstep(chains, carry)
        for c, (hh, d, sc) in enumerate(chains):
            if d == 0:
                of_s[hh, sc] = outs[c]
            else:
                ob_s[hh, sc] = outs[c]
        return tuple(states)

    init = tuple(s0_ref[0, d, hh] for hh in range(hps) for d in range(2))
    fin = body(0, init) if nsc == 1 else lax.fori_loop(0, nsc, body, init)
    for hh in range(hps):
        sf_ref[0, 0, hh] = fin[2 * hh]
        sf_ref[0, 1, hh] = fin[2 * hh + 1]
        lanes = slice(hh * LANES, (hh + 1) * LANES)
        for sc in range(nsc):
            o = _rms(of_s[hh, sc] + ob_s[hh, sc], ng_ref[...])
            o_ref[0, sc, :, lanes] = (o * _silu(z_ref[0, sc, :, lanes])).astype(BF16)


def _gdn_scan(qkv, proj, cols, rows, s0, norm_g, batch, seq, hps=2):
    nsc = seq // GDN_SUPER
    h = GDN_HEADS
    qkv4 = qkv.reshape(batch, nsc, GDN_SUPER, 3 * h * LANES)
    proj4 = proj.reshape(batch, nsc, GDN_SUPER, proj.shape[-1])

    def tile(off):
        return pl.BlockSpec((1, nsc, GDN_SUPER, hps * LANES), lambda b, j: (b, 0, 0, off // hps + j))

    st = pl.BlockSpec((1, 2, hps, GDN_DK, LANES), lambda b, j: (b, 0, j, 0, 0))
    return pl.pallas_call(
        functools.partial(_gdn_kernel, nsc=nsc, hps=hps),
        grid=(batch, h // hps),
        in_specs=[tile(0), tile(h), tile(2 * h), tile(3 * h),
                  pl.BlockSpec((hps, nsc, GDN_SUPER, 8), lambda b, j: (j, b, 0, 0)),
                  pl.BlockSpec((hps, nsc, 8, GDN_SUPER), lambda b, j: (j, b, 0, 0)),
                  st, pl.BlockSpec((1, LANES), lambda b, j: (0, 0))],
        out_specs=[tile(0), st],
        out_shape=[jax.ShapeDtypeStruct((batch, nsc, GDN_SUPER, h * LANES), BF16),
                   jax.ShapeDtypeStruct((batch, 2, h, GDN_DK, LANES), F32)],
        scratch_shapes=[pltpu.VMEM((hps, nsc, GDN_SUPER, LANES), F32),
                        pltpu.VMEM((hps, nsc, GDN_SUPER, LANES), F32)],
        compiler_params=_params("parallel", "parallel"),
        name="gdn_scan",
    )(qkv4, qkv4, qkv4, proj4, cols, rows, s0, norm_g.reshape(1, LANES))


def _gdn_mixer(x, g, mod, rows_per_mod, batch, seq, s0, w_cat, conv_w, a_log, dt_bias, norm_g):
    proj = _nmm(x, g, mod, w_cat, rows_per_mod, 0, 1)
    qkv = _gdn_prep(proj.reshape(batch, seq, proj.shape[-1]), conv_w)
    gates, tot = _gdn_gates(proj, a_log, dt_bias)
    h = GDN_HEADS
    n = batch * seq
    per = jnp.stack([gates[:, 0:h], gates[:, h:2 * h], tot[:, 0:h],
                     gates[:, 2 * h:3 * h], gates[:, 3 * h:4 * h], tot[:, 2 * h:3 * h],
                     jnp.zeros((n, h), F32), jnp.zeros((n, h), F32)], axis=-1)
    per = per.reshape(n // GDN_SUPER, GDN_SUPER, h, 8)
    cols = jnp.transpose(per, (2, 0, 1, 3))
    rows = jnp.transpose(per, (2, 0, 3, 1))
    o, s_fin = _gdn_scan(qkv, proj, cols, rows, s0, norm_g, batch, seq)
    return o.reshape(n, D_MODEL), s_fin


SUBLANES = 8


def _store_token_tiles(ref, x):
    rows = x.shape[0]
    for j in range(SUBLANES):
        ref[pl.ds(j, rows, stride=SUBLANES), :] = x[:, j * LANES:(j + 1) * LANES]


def _load_token_tiles(ref, rows):
    return jnp.concatenate([ref[pl.ds(j, rows, stride=SUBLANES), :] for j in range(SUBLANES)], axis=1)


def _moe_pre_kernel(x_ref, g_ref, mod_ref, wr_ref, h_ref, aff_ref):
    h = _norm_mod(x_ref[...], g_ref[...], mod_ref, 3, 4)
    _store_token_tiles(h_ref, h)
    h1, h2 = _split_bf16(h, 2)
    logits = _dot(h1, wr_ref[0]) + (_dot(h1, wr_ref[1]) + _dot(h2, wr_ref[0]))
    e = jnp.exp(logits - jnp.max(logits, axis=-1, keepdims=True))
    aff_ref[...] = e / jnp.sum(e, axis=-1, keepdims=True)


def _moe_pre(x, g, mod, rows_per_mod, w_router, tm=512):
    m = x.shape[0]
    w1 = w_router.astype(BF16)
    w2 = (w_router - w1.astype(F32)).astype(BF16)
    return pl.pallas_call(
        _moe_pre_kernel,
        grid=(m // tm,),
        in_specs=[pl.BlockSpec((tm, D_MODEL), lambda i: (i, 0)),
                  pl.BlockSpec((1, D_MODEL), lambda i: (0, 0)),
                  pl.BlockSpec((1, 6, D_MODEL), lambda i: ((i * tm) // rows_per_mod, 0, 0)),
                  pl.BlockSpec((2, D_MODEL, N_EXPERTS), lambda i: (0, 0, 0))],
        out_specs=[pl.BlockSpec((tm * SUBLANES, LANES), lambda i: (i, 0)),
                   pl.BlockSpec((tm, N_EXPERTS), lambda i: (i, 0))],
        out_shape=[jax.ShapeDtypeStruct((m * SUBLANES, LANES), F32), jax.ShapeDtypeStruct((m, N_EXPERTS), F32)],
        compiler_params=_params("parallel"),
        name="moe_router",
    )(x, g.reshape(1, D_MODEL), mod, jnp.stack([w1, w2]))


def _ffn_kernel(g0_ref, gat_ref, sct_ref, gate_ref, wg_ref, wu_ref, wd_ref, h_hbm, z_hbm,
                xb0, xb1, yb0, yb1, wg_s, wu_s, wd_s, g_s, s_s, gsem, ssem, psem, *, tc, pairs_per_e, npairs):
    m = pl.program_id(0)
    hb = (m % 2) * (3 * tc)
    nb = 3 * tc - hb
    last = npairs - 1

    def tile_in(off, buf, i, sem):
        return pltpu.make_async_copy(h_hbm.at[pl.ds(pl.multiple_of(off, SUBLANES), SUBLANES)],
                                     buf.at[pl.ds(i * SUBLANES, SUBLANES)], sem)

    def tile_out(off, buf, i, sem):
        return pltpu.make_async_copy(buf.at[pl.ds(i * SUBLANES, SUBLANES)],
                                     z_hbm.at[pl.ds(pl.multiple_of(off, SUBLANES), SUBLANES)], sem)

    def tables(row, base):
        return (pltpu.make_async_copy(gat_ref.at[row], g_s.at[pl.ds(base, 2 * tc)], psem.at[0]),
                pltpu.make_async_copy(sct_ref.at[row], s_s.at[pl.ds(base, 3 * tc)], psem.at[1]))

    def wait_tiles(copy):
        for _ in range(tc):
            copy.wait()

    @pl.when(m == 0)
    def _():
        yb1[...] = jnp.zeros_like(yb1)
        first = pltpu.make_async_copy(g0_ref.at[0], g_s.at[pl.ds(0, tc)], psem.at[0])
        first.start()
        first.wait()
        for i in range(tc):
            tile_in(g_s[i], xb0, i, gsem.at[0]).start()
        for cp in tables(0, 0):
            cp.start()

    for cp in tables(m, hb):
        cp.wait()
    for cp in tables(jnp.minimum(m + 1, last), nb):
        cp.start()

    @pl.when(m % pairs_per_e == 0)
    def _():
        wg_s[...] = wg_ref[0].astype(BF16)
        wu_s[...] = wu_ref[0].astype(BF16)
        wd_s[...] = wd_ref[0].astype(BF16)

    def ffn(xb, gate):
        x = _load_token_tiles(xb, tc).astype(BF16)
        half = wg_s.shape[1] // 2
        acc = None
        for c0 in (0, half):
            a = _dot(x, wg_s[:, c0:c0 + half])
            b = _dot(x, wu_s[:, c0:c0 + half])
            part = _dot((_silu(a) * b).astype(BF16), wd_s[c0:c0 + half, :])
            acc = part if acc is None else acc + part
        return acc * gate

    wait_tiles(tile_in(0, xb0, 0, gsem.at[0]))
    for i in range(tc):
        tile_in(g_s[hb + i], xb1, i, gsem.at[1]).start()
        tile_out(s_s[hb + i], yb1, i, ssem.at[1]).start()
    y = ffn(xb0, gate_ref[0, 0:tc])

    @pl.when(m > 0)
    def _():
        wait_tiles(tile_out(0, yb0, 0, ssem.at[0]))

    _store_token_tiles(yb0, y)

    wait_tiles(tile_in(0, xb1, 0, gsem.at[1]))
    for i in range(tc):
        tile_in(g_s[hb + tc + i], xb0, i, gsem.at[0]).start()
        tile_out(s_s[hb + tc + i], yb0, i, ssem.at[0]).start()
    y = ffn(xb1, gate_ref[0, tc:2 * tc])
    wait_tiles(tile_out(0, yb1, 0, ssem.at[1]))
    _store_token_tiles(yb1, y)

    @pl.when(m == last)
    def _():
        for i in range(tc):
            tile_out(s_s[hb + 2 * tc + i], yb1, i, ssem.at[1]).start()
        wait_tiles(tile_out(0, yb1, 0, ssem.at[1]))
        wait_tiles(tile_out(0, yb0, 0, ssem.at[0]))
        wait_tiles(tile_in(0, xb0, 0, gsem.at[0]))
        for cp in tables(last, nb):
            cp.wait()


def _expert_ffn(h_tiles, gat, sct, gate, w_gate, w_up, w_down, z_rows, tc=512):
    n_exp, cap = gat.shape
    ff = w_gate.shape[-1]
    tc = min(tc, cap // 2)
    pairs_per_e = cap // (2 * tc)
    npairs = n_exp * pairs_per_e
    nsteps = 2 * npairs
    gat = gat.reshape(nsteps, tc)
    sct = sct.reshape(nsteps, tc)
    spare = ((z_rows - tc + jnp.arange(tc, dtype=jnp.int32)) * SUBLANES)[None]
    gat_next = jnp.concatenate([gat[1:], gat[:1]], axis=0).reshape(npairs, 2 * tc)
    sct_prev = jnp.concatenate([spare, sct[:-1]], axis=0).reshape(npairs, 2 * tc)
    sct3 = jnp.concatenate([sct_prev, sct[1::2]], axis=1)
    tile_buf = pltpu.VMEM((tc * SUBLANES, LANES), F32)
    wspec_in = pl.BlockSpec((1, D_MODEL, ff), lambda m: (m // pairs_per_e, 0, 0))
    return pl.pallas_call(
        functools.partial(_ffn_kernel, tc=tc, pairs_per_e=pairs_per_e, npairs=npairs),
        grid=(npairs,),
        in_specs=[pl.BlockSpec((1, tc), lambda m: (0, 0)),
                  pl.BlockSpec((npairs, 2 * tc), lambda m: (0, 0)),
                  pl.BlockSpec((npairs, 3 * tc), lambda m: (0, 0)),
                  pl.BlockSpec((1, 2 * tc, 1), lambda m: (m, 0, 0)),
                  wspec_in, wspec_in,
                  pl.BlockSpec((1, ff, D_MODEL), lambda m: (m // pairs_per_e, 0, 0)),
                  pl.BlockSpec(memory_space=pl.ANY)],
        out_specs=pl.BlockSpec(memory_space=pl.ANY),
        out_shape=jax.ShapeDtypeStruct((z_rows * SUBLANES, LANES), F32),
        scratch_shapes=[tile_buf, tile_buf, tile_buf, tile_buf,
                        pltpu.VMEM((D_MODEL, ff), BF16), pltpu.VMEM((D_MODEL, ff), BF16),
                        pltpu.VMEM((ff, D_MODEL), BF16),
                        pltpu.SMEM((6 * tc,), jnp.int32), pltpu.SMEM((6 * tc,), jnp.int32),
                        pltpu.SemaphoreType.DMA((2,)), pltpu.SemaphoreType.DMA((2,)),
                        pltpu.SemaphoreType.DMA((2,))],
        compiler_params=_params("arbitrary"),
        name="expert_ffn",
    )(gat[:1], gat_next, sct3, gate.reshape(npairs, 2 * tc, 1), w_gate, w_up, w_down, h_tiles)


def _combine_kernel(mc_ref, x_ref, cnt_ref, mod_ref, fg_ref, z_hbm, o_ref, zbuf, acc, sem, *, gi, final, tt, n_tok):
    i = pl.program_id(0)
    mc = mc_ref[i]

    def plane(r, sl):
        row0 = pl.multiple_of((r * n_tok + i * tt) * SUBLANES, SUBLANES)
        return pltpu.make_async_copy(z_hbm.at[pl.ds(row0, tt * SUBLANES)], zbuf.at[sl], sem.at[sl])

    @pl.when(mc > 0)
    def _():
        plane(0, 0).start()

    acc[...] = jnp.zeros_like(acc)
    cnt = cnt_ref[...]

    def body(r, c):
        sl = r % 2

        @pl.when(r + 1 < mc)
        def _():
            plane(r + 1, 1 - sl).start()

        plane(r, sl).wait()
        keep = cnt > r
        for j in range(SUBLANES):
            acc[:, j * LANES:(j + 1) * LANES] += jnp.where(
                keep, zbuf[sl, pl.ds(j, tt, stride=SUBLANES), :], 0.0)
        return c

    lax.fori_loop(0, mc, body, 0)
    x = x_ref[...] + mod_ref[0, gi:gi + 1, :] * acc[...]
    if final:
        x = _rms(x, fg_ref[...])
    o_ref[...] = x


def _combine(x, z, cnt, maxcnt, mod, rows_per_mod, gi, final_g, final, tt=256):
    m = x.shape[0]
    return pl.pallas_call(
        functools.partial(_combine_kernel, gi=gi, final=final, tt=tt, n_tok=m),
        grid_spec=pltpu.PrefetchScalarGridSpec(
            num_scalar_prefetch=1,
            grid=(m // tt,),
            in_specs=[pl.BlockSpec((tt, D_MODEL), lambda i, mc: (i, 0)),
                      pl.BlockSpec((tt, 1), lambda i, mc: (i, 0)),
                      pl.BlockSpec((1, 6, D_MODEL), lambda i, mc: ((i * tt) // rows_per_mod, 0, 0)),
                      pl.BlockSpec((1, D_MODEL), lambda i, mc: (0, 0)),
                      pl.BlockSpec(memory_space=pl.ANY)],
            out_specs=pl.BlockSpec((tt, D_MODEL), lambda i, mc: (i, 0)),
            scratch_shapes=[pltpu.VMEM((2, tt * SUBLANES, LANES), F32), pltpu.VMEM((tt, D_MODEL), F32),
                            pltpu.SemaphoreType.DMA((2,))]),
        out_shape=jax.ShapeDtypeStruct((m, D_MODEL), F32),
        compiler_params=_params("arbitrary"),
        name="moe_combine",
    )(maxcnt, x, cnt.reshape(m, 1), mod, final_g.reshape(1, D_MODEL), z)


def _expert_choice(x, g, mod, rows_per_mod, w_router, w_gate, w_up, w_down, gi, final_g, final):
    n = x.shape[0]
    cap = EC_CAPACITY_FACTOR * n // N_EXPERTS
    tt = min(256, n)
    h, aff = _moe_pre(x, g, mod, rows_per_mod, w_router)
    gate, idx = lax.top_k(aff.T, cap)
    mask = jnp.zeros((N_EXPERTS, n), jnp.int32).at[jnp.arange(N_EXPERTS)[:, None], idx].set(1)
    rank = jnp.cumsum(mask, axis=0) - mask
    gat = idx * SUBLANES
    sct = (jnp.take_along_axis(rank, idx, axis=1) * n + idx) * SUBLANES
    cnt = jnp.sum(mask, axis=0)
    maxcnt = jnp.max(cnt.reshape(n // tt, tt), axis=1)
    z = _expert_ffn(h, gat, sct, gate, w_gate, w_up, w_down, N_EXPERTS * n + cap)
    return _combine(x, z, cnt, maxcnt, mod, rows_per_mod, gi, final_g, final, tt)


def _trunk(x, mods, rows_per_mod, batch, seq, caches, p):
    depth = p["mod_w"].shape[0]
    n = batch * seq
    new = {}
    for layer in range(depth):
        kind, slot = layer % 4, layer // 4
        mod = mods[layer]
        g_mix = p["norm_mix_g"][layer]
        if kind == 0:
            w_ab = jnp.pad(p["gdn_w_ab"][slot], ((0, 0), (0, LANES - 4 * GDN_HEADS)))
            w_cat = jnp.concatenate([p["gdn_w_in"][slot], w_ab], axis=1).astype(BF16)
            if caches is None:
                s0 = jnp.zeros((batch, 2, GDN_HEADS, GDN_DK, LANES), F32)
            else:
                s0 = caches["state_gdn"][:, slot]
            o, s_fin = _gdn_mixer(x, g_mix, mod, rows_per_mod, batch, seq, s0, w_cat, p["gdn_conv_w"][slot],
                                  p["gdn_a_log"][slot], p["gdn_dt_bias"][slot], p["gdn_norm_g"][slot])
            new.setdefault("gdn", []).append(s_fin)
            w_out = p["gdn_w_out"][slot]
        elif kind == 1:
            proj = _nmm(x, g_mix, mod, p["diff_w_in"][slot].astype(BF16), rows_per_mod, 0, 1)
            q, k, v = _attn_prep(proj, "diff", seq, caches is not None)
            nk = seq
            if caches is None:
                hw = DIFF_HEADS * LANES
                new.setdefault("diff_k", []).append(proj[:, hw:2 * hw].reshape(batch, seq, DIFF_HEADS, LANES))
                new.setdefault("diff_v", []).append(proj[:, 2 * hw:].reshape(batch, seq, DIFF_HEADS, LANES))
            else:
                kc = caches["cache_diff_k"][:, slot].reshape(batch, -1, DIFF_HEADS * LANES).astype(BF16)
                vc = caches["cache_diff_v"][:, slot].reshape(batch, -1, DIFF_HEADS * LANES).astype(BF16)
                nk = seq + kc.shape[1]
                k = jnp.concatenate([k.reshape(batch, seq, -1), kc], axis=1).reshape(batch * nk, -1)
                v = jnp.concatenate([v.reshape(batch, seq, -1), vc], axis=1).reshape(batch * nk, -1)
            lam_init = 0.8 - 0.6 * math.exp(-0.3 * layer)
            o = _attention(q, k, v, "diff", batch, seq, nk,
                           (p["diff_lambda"][slot], p["diff_subln_g"][slot]), lam_init)
            w_out = p["diff_w_out"][slot]
        elif kind == 2:
            proj = _nmm(x, g_mix, mod, p["gqa_w_in"][slot].astype(BF16), rows_per_mod, 0, 1)
            q, k, v, kf = _attn_prep(proj, "gqa", seq, caches is not None,
                                     (p["gqa_q_norm_g"][slot], p["gqa_k_norm_g"][slot]))
            nk = seq
            if caches is None:
                v0 = (GQA_HEADS + GQA_KV) * LANES
                new.setdefault("gqa_k", []).append(kf.reshape(batch, seq, GQA_KV, GQA_HD))
                new.setdefault("gqa_v", []).append(proj[:, v0:].reshape(batch, seq, GQA_KV, GQA_HD))
            else:
                kc = caches["cache_gqa_k"][:, slot].reshape(batch, -1, GQA_KV * GQA_HD).astype(BF16)
                vc = caches["cache_gqa_v"][:, slot].reshape(batch, -1, GQA_KV * GQA_HD).astype(BF16)
                nk = seq + kc.shape[1]
                k = jnp.concatenate([k.reshape(batch, seq, -1), kc], axis=1).reshape(batch * nk, -1)
                v = jnp.concatenate([v.reshape(batch, seq, -1), vc], axis=1).reshape(batch * nk, -1)
            o = _attention(q, k, v, "gqa", batch, seq, nk)
            w_out = p["gqa_w_out"][slot]
        else:
            proj = _nmm(x, g_mix, mod, p["swa_w_in"][slot].astype(BF16), rows_per_mod, 0, 1)
            q, k, v = _attn_prep(proj, "swa", seq, caches is not None)
            sinks = p["swa_sinks"][slot].astype(F32)
            if caches is None:
                k0 = SWA_HEADS * SWA_HD
                v0 = k0 + SWA_KV * SWA_HD
                new.setdefault("swa_k", []).append(proj[:, k0:v0].reshape(batch, seq, SWA_KV, SWA_HD))
                new.setdefault("swa_v", []).append(proj[:, v0:].reshape(batch, seq, SWA_KV, SWA_HD))
                o = _attention(q, k, v, "swa", batch, seq, seq, (sinks,))
            else:
                kc = jnp.pad(caches["cache_swa_k"][:, slot], ((0, 0), (0, 0), (0, 0), (0, LANES - SWA_HD)))
                past = kc.shape[1]
                kc = kc.reshape(batch, past, SWA_KV * LANES).astype(BF16)
                vc = caches["cache_swa_v"][:, slot].reshape(batch, past, SWA_KV * SWA_HD).astype(BF16)
                nk = seq + past
                k = jnp.concatenate([k.reshape(batch, seq, -1), kc], axis=1).reshape(batch * nk, -1)
                v = jnp.concatenate([v.reshape(batch, seq, -1), vc], axis=1).reshape(batch * nk, -1)
                o = _band_attention(q, k, v, sinks, batch, seq, past)
            w_out = p["swa_w_out"][slot]
        x = _mm_resid(o, w_out.astype(BF16), x, mod, rows_per_mod, 2)
        x = _expert_choice(x, p["norm_ffn_g"][layer], mod, rows_per_mod, p["moe_router"][layer],
                           p["moe_w_gate"][layer], p["moe_w_up"][layer], p["moe_w_down"][layer],
                           5, p["final_g"], layer == depth - 1)
    return x, new


def kernel(x_prompt, x_sample, state_gdn, cache_diff_k, cache_diff_v, cache_gqa_k, cache_gqa_v, cache_swa_k, cache_swa_v, c, c_ctx, mod_w, mod_b, norm_mix_g, norm_ffn_g, final_g, gdn_w_in, gdn_conv_w, gdn_w_ab, gdn_a_log, gdn_dt_bias, gdn_norm_g, gdn_w_out, diff_w_in, diff_lambda, diff_subln_g, diff_w_out, gqa_w_in, gqa_q_norm_g, gqa_k_norm_g, gqa_w_out, swa_w_in, swa_sinks, swa_w_out, moe_router, moe_w_gate, moe_w_up, moe_w_down):
    p = dict(mod_w=mod_w, mod_b=mod_b, norm_mix_g=norm_mix_g, norm_ffn_g=norm_ffn_g, final_g=final_g,
             gdn_w_in=gdn_w_in, gdn_conv_w=gdn_conv_w, gdn_w_ab=gdn_w_ab, gdn_a_log=gdn_a_log,
             gdn_dt_bias=gdn_dt_bias, gdn_norm_g=gdn_norm_g, gdn_w_out=gdn_w_out,
             diff_w_in=diff_w_in, diff_lambda=diff_lambda, diff_subln_g=diff_subln_g, diff_w_out=diff_w_out,
             gqa_w_in=gqa_w_in, gqa_q_norm_g=gqa_q_norm_g, gqa_k_norm_g=gqa_k_norm_g, gqa_w_out=gqa_w_out,
             swa_w_in=swa_w_in, swa_sinks=swa_sinks, swa_w_out=swa_w_out,
             moe_router=moe_router, moe_w_gate=moe_w_gate, moe_w_up=moe_w_up, moe_w_down=moe_w_down)
    batch, seq, _ = x_prompt.shape
    dec_batch, dec_seq, _ = x_sample.shape
    depth = mod_w.shape[0]
    rows = 8 * ((1 + dec_batch + 7) // 8)
    cond = jnp.zeros((rows, D_MODEL), F32).at[0].set(c_ctx).at[1:1 + dec_batch].set(c)
    mod_all = _mod_proj(cond, mod_w, mod_b).reshape(depth, rows, 6, D_MODEL)
    mods_ctx = [mod_all[l, 0:1] for l in range(depth)]
    mods_lat = [mod_all[l, 1:1 + dec_batch] for l in range(depth)]

    y_prompt, st = _trunk(x_prompt.reshape(batch * seq, D_MODEL), mods_ctx, batch * seq, batch, seq, None, p)
    caches = dict(state_gdn=state_gdn, cache_diff_k=cache_diff_k, cache_diff_v=cache_diff_v,
                  cache_gqa_k=cache_gqa_k, cache_gqa_v=cache_gqa_v,
                  cache_swa_k=cache_swa_k, cache_swa_v=cache_swa_v)
    y_sample, _ = _trunk(x_sample.reshape(dec_batch * dec_seq, D_MODEL), mods_lat, dec_seq, dec_batch, dec_seq,
                         caches, p)

    def stack(xs):
        return jnp.stack(xs, axis=1)

    return (y_prompt.reshape(batch, seq, D_MODEL), y_sample.reshape(dec_batch, dec_seq, D_MODEL),
            stack(st["gdn"]), stack(st["diff_k"]), stack(st["diff_v"]),
            stack(st["gqa_k"]), stack(st["gqa_v"]), stack(st["swa_k"]), stack(st["swa_v"]))
```

```python
import functools
import math

import jax
import jax.numpy as jnp
from jax import lax
from jax.experimental import pallas as pl
from jax.experimental.pallas import tpu as pltpu

F32 = jnp.float32
BF16 = jnp.bfloat16

D_MODEL = 1024
GRID_W = 64
ROPE_THETA = 10000.0
NORM_EPS = 1e-6
LANES = 128
GDN_HEADS = 8
GDN_DK = 128
GDN_CHUNK = 64
GDN_SUPER = 256
DIFF_HEADS = 8
DIFF_DH = 64
GQA_HEADS = 8
GQA_KV = 2
GQA_HD = 128
SWA_HEADS = 16
SWA_KV = 4
SWA_HD = 64
WINDOW = 128
N_EXPERTS = 16
EC_CAPACITY_FACTOR = 2
VMEM_LIMIT_BYTES = 56 * 1024 * 1024


def _params(*sem):
    return pltpu.CompilerParams(dimension_semantics=sem, vmem_limit_bytes=VMEM_LIMIT_BYTES)


def _silu(x):
    return x / (1.0 + jnp.exp(-x))


def _rms(x, g):
    return x * lax.rsqrt(jnp.mean(x * x, axis=-1, keepdims=True) + NORM_EPS) * g


def _dot(a, b):
    return jnp.dot(a, b, preferred_element_type=F32)


def _dot_nt(a, b):
    return lax.dot_general(a, b, (((1,), (1,)), ((), ())), preferred_element_type=F32)


def _split_bf16(x, parts):
    out = []
    for _ in range(parts - 1):
        hi = x.astype(BF16)
        out.append(hi)
        x = x - hi.astype(F32)
    out.append(x.astype(BF16))
    return out


def _mod_kernel(c_ref, w_ref, b_ref, o_ref):
    a = _silu(c_ref[...]).astype(BF16)
    o_ref[0] = _dot(a, w_ref[0].astype(BF16)) + b_ref[0]


def _mod_proj(cond, mod_w, mod_b):
    depth, _, width = mod_w.shape
    rows = cond.shape[0]
    tn = 1536
    return pl.pallas_call(
        _mod_kernel,
        grid=(depth, width // tn),
        in_specs=[pl.BlockSpec((rows, D_MODEL), lambda l, j: (0, 0)),
                  pl.BlockSpec((1, D_MODEL, tn), lambda l, j: (l, 0, j)),
                  pl.BlockSpec((1, 1, tn), lambda l, j: (l, 0, j))],
        out_specs=pl.BlockSpec((1, rows, tn), lambda l, j: (l, 0, j)),
        out_shape=jax.ShapeDtypeStruct((depth, rows, width), F32),
        compiler_params=_params("parallel", "parallel"),
        name="mod_proj",
    )(cond, mod_w, mod_b.reshape(depth, 1, width))


def _norm_mod(x, g, mod_ref, sh, sc):
    return _rms(x, g) * (1.0 + mod_ref[0, sc:sc + 1, :]) + mod_ref[0, sh:sh + 1, :]


def _nmm_kernel(x_ref, g_ref, mod_ref, w_ref, o_ref, *, sh, sc, chunk):
    h = _norm_mod(x_ref[...], g_ref[...], mod_ref, sh, sc).astype(BF16)
    width = w_ref.shape[1]
    for c0 in range(0, width, chunk):
        c1 = min(c0 + chunk, width)
        o_ref[:, c0:c1] = _dot(h, w_ref[:, c0:c1])


def _nmm(x, g, mod, w, rows_per_mod, sh, sc, tm=256):
    m = x.shape[0]
    width = w.shape[1]
    return pl.pallas_call(
        functools.partial(_nmm_kernel, sh=sh, sc=sc, chunk=512),
        grid=(m // tm,),
        in_specs=[pl.BlockSpec((tm, D_MODEL), lambda i: (i, 0)),
                  pl.BlockSpec((1, D_MODEL), lambda i: (0, 0)),
                  pl.BlockSpec((1, 6, D_MODEL), lambda i: ((i * tm) // rows_per_mod, 0, 0)),
                  pl.BlockSpec((D_MODEL, width), lambda i: (0, 0))],
        out_specs=pl.BlockSpec((tm, width), lambda i: (i, 0)),
        out_shape=jax.ShapeDtypeStruct((m, width), F32),
        compiler_params=_params("parallel"),
        name="norm_mod_matmul",
    )(x, g.reshape(1, D_MODEL), mod, w)


def _mmr_kernel(a_ref, w_ref, r_ref, mod_ref, o_ref, *, gi):
    acc = _dot(a_ref[...].astype(BF16), w_ref[...])
    o_ref[...] = r_ref[...] + mod_ref[0, gi:gi + 1, :] * acc


def _mm_resid(a, w, res, mod, rows_per_mod, gi, tm=512):
    m = a.shape[0]
    return pl.pallas_call(
        functools.partial(_mmr_kernel, gi=gi),
        grid=(m // tm,),
        in_specs=[pl.BlockSpec((tm, D_MODEL), lambda i: (i, 0)),
                  pl.BlockSpec((D_MODEL, D_MODEL), lambda i: (0, 0)),
                  pl.BlockSpec((tm, D_MODEL), lambda i: (i, 0)),
                  pl.BlockSpec((1, 6, D_MODEL), lambda i: ((i * tm) // rows_per_mod, 0, 0))],
        out_specs=pl.BlockSpec((tm, D_MODEL), lambda i: (i, 0)),
        out_shape=jax.ShapeDtypeStruct((m, D_MODEL), F32),
        compiler_params=_params("parallel"),
        name="out_proj_resid",
    )(a, w, res, mod)


def _rope_tables(n_tokens, dim):
    rows = n_tokens // GRID_W
    row = jnp.broadcast_to(jnp.arange(rows, dtype=F32)[:, None], (rows, GRID_W)).reshape(-1)
    col = jnp.broadcast_to(jnp.arange(GRID_W, dtype=F32)[None, :], (rows, GRID_W)).reshape(-1)
    n_freq = dim // 4
    inv = ROPE_THETA ** (-jnp.arange(n_freq, dtype=F32) / n_freq)
    ang = jnp.concatenate([row[:, None] * inv, col[:, None] * inv], axis=-1)
    cos, sin = jnp.cos(ang), jnp.sin(ang)
    zero = jnp.zeros_like(sin)
    reps = LANES // dim
    cos_t = jnp.tile(jnp.concatenate([cos, cos], axis=-1), (1, reps))
    sin_a = jnp.tile(jnp.concatenate([-sin, zero], axis=-1), (1, reps))
    sin_b = jnp.tile(jnp.concatenate([zero, sin], axis=-1), (1, reps))
    return cos_t, sin_a, sin_b


def _rope(x, cos, sin_a, sin_b, dim):
    half = dim // 2
    return x * cos + pltpu.roll(x, LANES - half, 1) * sin_a + pltpu.roll(x, half, 1) * sin_b


def _prep_kernel(*refs, kind, rope):
    if kind == "gqa":
        p_ref, cos_ref, sa_ref, sb_ref, qg_ref, kg_ref, q_out, k_out, v_out, kf_out = refs
    elif kind == "diff" and not rope:
        p_ref, cos_ref, sa_ref, sb_ref, q_out, k_out, v_out, kc_out, vc_out = refs
        hw = DIFF_HEADS * LANES
        _store_token_tiles(kc_out, p_ref[:, hw:2 * hw])
        _store_token_tiles(vc_out, p_ref[:, 2 * hw:3 * hw])
    else:
        p_ref, cos_ref, sa_ref, sb_ref, q_out, k_out, v_out = refs
    cos, sa, sb = cos_ref[...], sa_ref[...], sb_ref[...]

    def blk(j):
        return p_ref[:, j * LANES:(j + 1) * LANES]

    if kind == "gqa":
        scale = GQA_HD ** -0.5 * LOG2E
        for j in range(GQA_HEADS):
            y = _rms(blk(j), qg_ref[...])
            if rope:
                y = _rope(y, cos, sa, sb, GQA_HD)
            q_out[:, j * LANES:(j + 1) * LANES] = (y * scale).astype(BF16)
        for j in range(GQA_KV):
            y = _rms(blk(GQA_HEADS + j), kg_ref[...])
            kf_out[:, j * LANES:(j + 1) * LANES] = y
            if rope:
                y = _rope(y, cos, sa, sb, GQA_HD)
            k_out[:, j * LANES:(j + 1) * LANES] = y.astype(BF16)
        v0 = (GQA_HEADS + GQA_KV) * LANES
        v_out[...] = p_ref[:, v0:v0 + GQA_KV * LANES].astype(BF16)
    elif kind == "diff":
        scale = DIFF_DH ** -0.5 * LOG2E
        for j in range(DIFF_HEADS):
            y = blk(j)
            if rope:
                y = _rope(y, cos, sa, sb, DIFF_DH)
            q_out[:, j * LANES:(j + 1) * LANES] = (y * scale).astype(BF16)
            y = blk(DIFF_HEADS + j)
            if rope:
                y = _rope(y, cos, sa, sb, DIFF_DH)
            k_out[:, j * LANES:(j + 1) * LANES] = y.astype(BF16)
        v_out[...] = p_ref[:, 2 * DIFF_HEADS * LANES:3 * DIFF_HEADS * LANES].astype(BF16)
    else:
        scale = SWA_HD ** -0.5 * LOG2E
        low = lax.broadcasted_iota(jnp.int32, cos.shape, 1) < SWA_HD

        def pad_pair(y, out, j):
            out[:, (2 * j) * LANES:(2 * j + 1) * LANES] = jnp.where(low, y, 0.0).astype(BF16)
            out[:, (2 * j + 1) * LANES:(2 * j + 2) * LANES] = jnp.where(
                low, pltpu.roll(y, SWA_HD, 1), 0.0).astype(BF16)

        nq = SWA_HEADS // 2
        for j in range(nq):
            y = blk(j)
            if rope:
                y = _rope(y, cos, sa, sb, SWA_HD)
            pad_pair(y * scale, q_out, j)
        for j in range(SWA_KV // 2):
            y = blk(nq + j)
            if rope:
                y = _rope(y, cos, sa, sb, SWA_HD)
            pad_pair(y, k_out, j)
        v0 = (nq + SWA_KV // 2) * LANES
        v_out[...] = p_ref[:, v0:v0 + (SWA_KV // 2) * LANES].astype(BF16)


def _attn_prep(proj, kind, seq, rope, gains=None, tm=256):
    m, width = proj.shape
    dim = GQA_HD if kind == "gqa" else DIFF_DH
    cos, sa, sb = _rope_tables(seq, dim)
    nt = seq // tm
    if kind == "gqa":
        wq, wk, wv = GQA_HEADS * LANES, GQA_KV * LANES, GQA_KV * LANES
    elif kind == "diff":
        wq, wk, wv = DIFF_HEADS * LANES, DIFF_HEADS * LANES, DIFF_HEADS * LANES
    else:
        wq, wk, wv = SWA_HEADS * LANES, SWA_KV * LANES, (SWA_KV // 2) * LANES
    tab = pl.BlockSpec((tm, LANES), lambda i: (i % nt, 0))
    in_specs = [pl.BlockSpec((tm, width), lambda i: (i, 0)), tab, tab, tab]
    args = [proj, cos, sa, sb]
    out_shapes = [jax.ShapeDtypeStruct((m, wq), BF16), jax.ShapeDtypeStruct((m, wk), BF16),
                  jax.ShapeDtypeStruct((m, wv), BF16)]
    out_specs = [pl.BlockSpec((tm, wq), lambda i: (i, 0)), pl.BlockSpec((tm, wk), lambda i: (i, 0)),
                 pl.BlockSpec((tm, wv), lambda i: (i, 0))]
    if kind == "gqa":
        gspec = pl.BlockSpec((1, LANES), lambda i: (0, 0))
        in_specs += [gspec, gspec]
        args += [gains[0].reshape(1, LANES), gains[1].reshape(1, LANES)]
        out_shapes.append(jax.ShapeDtypeStruct((m, wk), F32))
        out_specs.append(pl.BlockSpec((tm, wk), lambda i: (i, 0)))
    elif kind == "diff" and not rope:
        out_shapes += [jax.ShapeDtypeStruct((m * SUBLANES, LANES), F32)] * 2
        out_specs += [pl.BlockSpec((tm * SUBLANES, LANES), lambda i: (i, 0))] * 2
    return pl.pallas_call(
        functools.partial(_prep_kernel, kind=kind, rope=rope),
        grid=(m // tm,),
        in_specs=in_specs,
        out_specs=out_specs,
        out_shape=out_shapes,
        compiler_params=_params("parallel"),
        name="attn_prep_" + kind,
    )(*args)


LOG2E = 1.4426950408889634


def _with_ones(v):
    return jnp.concatenate([v, jnp.ones_like(v)], axis=1)


def _softmax_pv(s, mx, v_ones):
    e = jnp.exp2(s - mx).astype(BF16)
    acc = _dot(e, v_ones)
    return acc[:, :LANES], acc[:, LANES:LANES + 1]


def _pick_half(o_a, o_b, upper):
    low = lax.broadcasted_iota(jnp.int32, o_a.shape, 1) < SWA_HD
    if upper:
        return jnp.where(low, pltpu.roll(o_a, SWA_HD, 1), o_b)
    return jnp.where(low, o_a, pltpu.roll(o_b, SWA_HD, 1))


def _attn_kernel(*refs, mode, lam_init):
    if mode == "gqa":
        q_ref, k_ref, v_ref, o_ref = refs
    elif mode == "diff":
        q_ref, k_ref, v_ref, lam_ref, g_ref, o_ref = refs
    else:
        q_ref, k_ref, v_ref, sink_ref, o_ref = refs
    v = _with_ones(v_ref[...])
    if mode == "gqa":
        k = k_ref[...]
        for j in range(GQA_HEADS // GQA_KV):
            s = _dot_nt(q_ref[:, j * LANES:(j + 1) * LANES], k)
            num, den = _softmax_pv(s, jnp.max(s, axis=-1, keepdims=True), v)
            o_ref[:, j * LANES:(j + 1) * LANES] = (num / den).astype(BF16)
    elif mode == "diff":
        k = k_ref[...]
        q = q_ref[...]
        low = lax.broadcasted_iota(jnp.int32, q.shape, 1) < DIFF_DH
        lv = lam_ref[...]
        lam = (jnp.exp(jnp.sum(lv[0:1] * lv[1:2], axis=-1, keepdims=True))
               - jnp.exp(jnp.sum(lv[2:3] * lv[3:4], axis=-1, keepdims=True)) + lam_init)
        s0 = _dot_nt(jnp.where(low, q, 0.0).astype(BF16), k)
        s1 = _dot_nt(jnp.where(low, 0.0, q).astype(BF16), k)
        n0, d0 = _softmax_pv(s0, jnp.max(s0, axis=-1, keepdims=True), v)
        n1, d1 = _softmax_pv(s1, jnp.max(s1, axis=-1, keepdims=True), v)
        o = n0 / d0 - n1 * (lam / d1)
        o_ref[...] = (_rms(o, g_ref[...]) * (1.0 - lam_init)).astype(BF16)
    else:
        heads = SWA_HEADS // (SWA_KV // 2)
        group = SWA_HEADS // SWA_KV
        outs = []
        for j in range(heads):
            a = j // group
            s = _dot_nt(q_ref[:, j * LANES:(j + 1) * LANES], k_ref[:, a * LANES:(a + 1) * LANES])
            sink = sink_ref[0, :, j:j + 1] * LOG2E
            mx = jnp.maximum(jnp.max(s, axis=-1, keepdims=True), sink)
            num, den = _softmax_pv(s, mx, v)
            outs.append(num / (den + jnp.exp2(sink - mx)))
        for jj in range(heads // 2):
            a = (2 * jj) // group
            o_ref[:, jj * LANES:(jj + 1) * LANES] = _pick_half(outs[2 * jj], outs[2 * jj + 1], a == 1).astype(BF16)


def _attention(q, k, v, mode, batch, seq, n_keys, extra=(), lam_init=0.0, tq=256):
    nq = seq // tq
    if mode == "gqa":
        groups, wq, wk, wv, wo = GQA_KV, (GQA_HEADS // GQA_KV) * LANES, LANES, LANES, (GQA_HEADS // GQA_KV) * LANES
    elif mode == "diff":
        groups, wq, wk, wv, wo = DIFF_HEADS, LANES, LANES, LANES, LANES
    else:
        groups, wq, wk, wv, wo = SWA_KV // 2, SWA_HEADS // 2 * LANES, 2 * LANES, LANES, 4 * LANES
    in_specs = [pl.BlockSpec((tq, wq), lambda b, g, i: (b * nq + i, g)),
                pl.BlockSpec((n_keys, wk), lambda b, g, i: (b, g)),
                pl.BlockSpec((n_keys, wv), lambda b, g, i: (b, g))]
    args = [q, k, v]
    if mode == "diff":
        lam_vecs, subln_g = extra
        in_specs += [pl.BlockSpec((4, DIFF_DH), lambda b, g, i: (0, 0)),
                     pl.BlockSpec((1, LANES), lambda b, g, i: (0, 0))]
        args += [lam_vecs, subln_g.reshape(1, LANES)]
    elif mode == "swa":
        (sinks,) = extra
        in_specs.append(pl.BlockSpec((1, 1, SWA_HEADS // 2), lambda b, g, i: (g, 0, 0)))
        args.append(sinks.reshape(2, 1, SWA_HEADS // 2))
    return pl.pallas_call(
        functools.partial(_attn_kernel, mode=mode, lam_init=lam_init),
        grid=(batch, groups, nq),
        in_specs=in_specs,
        out_specs=pl.BlockSpec((tq, wo), lambda b, g, i: (b * nq + i, g)),
        out_shape=jax.ShapeDtypeStruct((batch * seq, D_MODEL), BF16),
        compiler_params=_params("parallel", "parallel", "parallel"),
        name="attn_" + mode,
    )(*args)


def _band_kernel(q_ref, kp_ref, kc_ref, kn_ref, kx_ref, vp_ref, vc_ref, vn_ref, vx_ref, sink_ref, o_ref, *, nqb):
    i = pl.program_id(2)
    qi = lax.broadcasted_iota(jnp.int32, (WINDOW, WINDOW), 0)
    kj = lax.broadcasted_iota(jnp.int32, (WINDOW, WINDOW), 1)
    open_blk = jnp.zeros((WINDOW, WINDOW), F32)
    bias = jnp.concatenate([jnp.where((kj >= qi) & (i > 0), 0.0, -jnp.inf), open_blk,
                            jnp.where((kj <= qi) & (i < nqb - 1), 0.0, -jnp.inf),
                            jnp.zeros((WINDOW, kx_ref.shape[0]), F32)], axis=1)
    keys = jnp.concatenate([kp_ref[...], kc_ref[...], kn_ref[...], kx_ref[...]], axis=0)
    vals = _with_ones(jnp.concatenate([vp_ref[...], vc_ref[...], vn_ref[...], vx_ref[...]], axis=0))
    heads = SWA_HEADS // (SWA_KV // 2)
    group = SWA_HEADS // SWA_KV
    outs = []
    for j in range(heads):
        a = j // group
        s = _dot_nt(q_ref[:, j * LANES:(j + 1) * LANES], keys[:, a * LANES:(a + 1) * LANES]) + bias
        sink = sink_ref[0, :, j:j + 1] * LOG2E
        mx = jnp.maximum(jnp.max(s, axis=-1, keepdims=True), sink)
        num, den = _softmax_pv(s, mx, vals)
        outs.append(num / (den + jnp.exp2(sink - mx)))
    for jj in range(heads // 2):
        a = (2 * jj) // group
        o_ref[:, jj * LANES:(jj + 1) * LANES] = _pick_half(outs[2 * jj], outs[2 * jj + 1], a == 1).astype(BF16)


def _band_attention(q, k, v, sinks, batch, seq, past):
    nqb = seq // WINDOW
    nkb = (seq + past) // WINDOW
    pairs = SWA_KV // 2
    xb = past // WINDOW
    assert past % WINDOW == 0 and seq % past == 0

    def band(shift, width):
        return pl.BlockSpec((WINDOW, width),
                            lambda b, g, i: (b * nkb + jnp.clip(i + shift, 0, nqb - 1), g))

    def ctx(width):
        return pl.BlockSpec((past, width), lambda b, g, i: (b * (nkb // xb) + seq // past, g))

    return pl.pallas_call(
        functools.partial(_band_kernel, nqb=nqb),
        grid=(batch, pairs, nqb),
        in_specs=[pl.BlockSpec((WINDOW, SWA_HEADS // 2 * LANES), lambda b, g, i: (b * nqb + i, g)),
                  band(-1, 2 * LANES), band(0, 2 * LANES), band(1, 2 * LANES), ctx(2 * LANES),
                  band(-1, LANES), band(0, LANES), band(1, LANES), ctx(LANES),
                  pl.BlockSpec((1, 1, SWA_HEADS // 2), lambda b, g, i: (g, 0, 0))],
        out_specs=pl.BlockSpec((WINDOW, 4 * LANES), lambda b, g, i: (b * nqb + i, g)),
        out_shape=jax.ShapeDtypeStruct((batch * seq, D_MODEL), BF16),
        compiler_params=_params("parallel", "parallel", "parallel"),
        name="attn_band",
    )(q, k, k, k, k, v, v, v, v, sinks.reshape(2, 1, SWA_HEADS // 2))


def _gdn_prep_kernel(p_ref, w_ref, o_ref):
    x = p_ref[0]
    t = x.shape[0]
    w = w_ref[...]
    row = lax.broadcasted_iota(jnp.int32, x.shape, 0)
    xm = jnp.where(row == 0, 0.0, pltpu.roll(x, 1, 0))
    xp = jnp.where(row == t - 1, 0.0, pltpu.roll(x, t - 1, 0))
    y = _silu(xm * w[0:1] + x * w[1:2] + xp * w[2:3])
    rs = lax.rsqrt(jnp.sum(y * y, axis=-1, keepdims=True) + NORM_EPS)
    j = pl.program_id(1)
    is_q = (j < GDN_HEADS).astype(F32)
    normed = (j < 2 * GDN_HEADS).astype(F32)
    scale = is_q * (GDN_DK ** -0.5) + (1.0 - is_q)
    o_ref[0] = y * (rs * (scale * normed) + (1.0 - normed))


def _gdn_prep(proj3, conv_w):
    batch, seq, _ = proj3.shape
    nblk = 3 * GDN_HEADS
    return pl.pallas_call(
        _gdn_prep_kernel,
        grid=(batch, nblk),
        in_specs=[pl.BlockSpec((1, seq, LANES), lambda b, j: (b, 0, j)),
                  pl.BlockSpec((3, LANES), lambda b, j: (0, j))],
        out_specs=pl.BlockSpec((1, seq, LANES), lambda b, j: (b, 0, j)),
        out_shape=jax.ShapeDtypeStruct((batch, seq, nblk * LANES), F32),
        compiler_params=_params("parallel", "parallel"),
        name="gdn_conv",
    )(proj3, conv_w)


def _gdn_gate_kernel(p_ref, al_ref, dtb_ref, o_ref, t_ref):
    ab = p_ref[...]
    lane = lax.broadcasted_iota(jnp.int32, ab.shape, 1)
    is_g = (lane < 4 * GDN_HEADS) & ((lane & GDN_HEADS) == 0)
    is_fwd = lane < 2 * GDN_HEADS
    x = ab + dtb_ref[...]
    softplus = jnp.maximum(x, 0.0) + jnp.log(1.0 + jnp.exp(-jnp.abs(x)))
    g = jnp.where(is_g, -jnp.exp(al_ref[...]) * softplus, 0.0)
    beta = 1.0 / (1.0 + jnp.exp(-ab))
    ri = lax.broadcasted_iota(jnp.int32, (GDN_SUPER, GDN_SUPER), 0)
    ci = lax.broadcasted_iota(jnp.int32, (GDN_SUPER, GDN_SUPER), 1)
    same = (ri // GDN_CHUNK) == (ci // GDN_CHUNK)
    lower = jnp.where(same & (ri >= ci), 1.0, 0.0).astype(BF16)
    upper = jnp.where(same & (ri <= ci), 1.0, 0.0).astype(BF16)
    parts = _split_bf16(g, 3)
    pre = _dot(lower, parts[0]) + _dot(lower, parts[1]) + _dot(lower, parts[2])
    suf = _dot(upper, parts[0]) + _dot(upper, parts[1]) + _dot(upper, parts[2])
    o_ref[...] = jnp.where(is_g, jnp.where(is_fwd, pre, suf), beta)
    t_ref[...] = pre + suf - g


def _gdn_gates(proj, a_log, dt_bias):
    m, width = proj.shape
    col_blk = (width - LANES) // LANES
    pad = jnp.zeros((2, GDN_HEADS), F32)

    def row(v):
        return jnp.pad(jnp.stack([v.astype(F32), pad], axis=1).reshape(1, 4 * GDN_HEADS),
                       ((0, 0), (0, LANES - 4 * GDN_HEADS)))

    spec = pl.BlockSpec((GDN_SUPER, LANES), lambda i: (i, 0))
    vec = pl.BlockSpec((1, LANES), lambda i: (0, 0))
    return pl.pallas_call(
        _gdn_gate_kernel,
        grid=(m // GDN_SUPER,),
        in_specs=[pl.BlockSpec((GDN_SUPER, LANES), lambda i: (i, col_blk)), vec, vec],
        out_specs=[spec, spec],
        out_shape=[jax.ShapeDtypeStruct((m, LANES), F32), jax.ShapeDtypeStruct((m, LANES), F32)],
        compiler_params=_params("parallel"),
        name="gdn_gates",
    )(proj, row(a_log), row(dt_bias))


def _gdn_kernel(q_ref, k_ref, v_ref, z_ref, col_ref, row_ref, s0_ref, ng_ref, o_ref, sf_ref, of_s, ob_s, *, nsc, hps):
    n = GDN_SUPER
    ri = lax.broadcasted_iota(jnp.int32, (n, n), 0)
    ci = lax.broadcasted_iota(jnp.int32, (n, n), 1)
    b64 = (ri >> 6) == (ci >> 6)
    b32 = (ri >> 5) == (ci >> 5)
    b16 = (ri >> 4) == (ci >> 4)
    eye = jnp.where(ri == ci, 1.0, 0.0)
    causal = (b64 & (ri >= ci), b64 & (ri <= ci))
    strict = (b64 & (ri > ci), b64 & (ri < ci))
    nchunk = n // GDN_CHUNK

    def mm(a, b):
        return _dot(a.astype(BF16), b.astype(BF16))

    def lockstep(chains, states):
        cs = range(len(chains))
        dirs = [d for (_, d, _) in chains]
        lanes = [slice(hh * LANES, (hh + 1) * LANES) for (hh, _, _) in chains]
        q = [q_ref[0, sc, :, lanes[c]] for c, (_, _, sc) in enumerate(chains)]
        k = [k_ref[0, sc, :, lanes[c]] for c, (_, _, sc) in enumerate(chains)]
        v = [v_ref[0, sc, :, lanes[c]] for c, (_, _, sc) in enumerate(chains)]
        col = [col_ref[hh, sc] for (hh, _, sc) in chains]
        row = [row_ref[hh, sc] for (hh, _, sc) in chains]
        gcol = [col[c][:, 3 * dirs[c]:3 * dirs[c] + 1] for c in cs]
        bcol = [col[c][:, 3 * dirs[c] + 1:3 * dirs[c] + 2] for c in cs]
        tcol = [col[c][:, 3 * dirs[c] + 2:3 * dirs[c] + 3] for c in cs]
        grow = [row[c][3 * dirs[c]:3 * dirs[c] + 1, :] for c in cs]
        trow = [row[c][3 * dirs[c] + 2:3 * dirs[c] + 3, :] for c in cs]
        decay = [jnp.exp(jnp.where(causal[dirs[c]], gcol[c] - grow[c], -jnp.inf)) for c in cs]
        k16 = [k[c].astype(BF16) for c in cs]
        kb = [k[c] * bcol[c] for c in cs]
        kk = [_dot_nt(kb[c].astype(BF16), k16[c]) for c in cs]
        qk = [_dot_nt(q[c].astype(BF16), k16[c]) for c in cs]
        lmat = [jnp.where(strict[dirs[c]], kk[c] * decay[c], 0.0) for c in cs]
        intra = [(qk[c] * decay[c]).astype(BF16) for c in cs]
        l0 = [jnp.where(b16, lmat[c], 0.0) for c in cs]
        inv = [eye - l0[c] for c in cs]
        pw = [mm(l0[c], l0[c]) for c in cs]
        for rep in range(3):
            inv = [inv[c] + mm(inv[c], pw[c]) for c in cs]
            if rep < 2:
                pw = [mm(pw[c], pw[c]) for c in cs]
        for keep in (b32 & (~b16), ~b32):
            cm = [jnp.where(keep, lmat[c], 0.0) for c in cs]
            t = [mm(inv[c], cm[c]) for c in cs]
            inv = [inv[c] - mm(t[c], inv[c]) for c in cs]
        eg = [jnp.exp(gcol[c]) for c in cs]
        uw = [mm(inv[c], jnp.concatenate([v[c] * bcol[c], kb[c] * eg[c]], axis=1)) for c in cs]
        u = [uw[c][:, :LANES] for c in cs]
        w = [uw[c][:, LANES:].astype(BF16) for c in cs]
        qd = [(q[c] * eg[c]).astype(BF16) for c in cs]
        kdt = [jnp.transpose(k[c] * jnp.exp(tcol[c] - gcol[c])).astype(BF16) for c in cs]
        outs = [[None] * nchunk for _ in cs]
        states = list(states)
        for step in range(nchunk):
            js = [step if dirs[c] == 0 else nchunk - 1 - step for c in cs]
            rs = [slice(j * GDN_CHUNK, (j + 1) * GDN_CHUNK) for j in js]
            s16 = [states[c].astype(BF16) for c in cs]
            vn16 = [(u[c][rs[c]] - _dot(w[c][rs[c]], s16[c])).astype(BF16) for c in cs]
            qs = [_dot(qd[c][rs[c]], s16[c]) for c in cs]
            for c in cs:
                outs[c][js[c]] = qs[c] + _dot(intra[c][rs[c], rs[c]], vn16[c])
            states = [states[c] * jnp.exp(trow[c][:, js[c] * GDN_CHUNK:js[c] * GDN_CHUNK + 1])
                      + _dot(kdt[c][:, rs[c]], vn16[c]) for c in cs]
        return [jnp.concatenate(outs[c], axis=0) for c in cs], states

    def body(i, carry):
        chains = [(hh, d, i if d == 0 else nsc - 1 - i) for hh in range(hps) for d in range(2)]
        outs, states = lockstep(chains, carry)
        for c, (hh, d, sc) in enumerate(chains):
            if d == 0:
                of_s[hh, sc] = outs[c]
            else:
                ob_s[hh, sc] = outs[c]
        return tuple(states)

    init = tuple(s0_ref[0, d, hh] for hh in range(hps) for d in range(2))
    fin = body(0, init) if nsc == 1 else lax.fori_loop(0, nsc, body, init)
    for hh in range(hps):
        sf_ref[0, 0, hh] = fin[2 * hh]
        sf_ref[0, 1, hh] = fin[2 * hh + 1]
        lanes = slice(hh * LANES, (hh + 1) * LANES)
        for sc in range(nsc):
            o = _rms(of_s[hh, sc] + ob_s[hh, sc], ng_ref[...])
            o_ref[0, sc, :, lanes] = (o * _silu(z_ref[0, sc, :, lanes])).astype(BF16)


def _gdn_scan(qkv, proj, cols, rows, s0, norm_g, batch, seq, hps=2):
    nsc = seq // GDN_SUPER
    h = GDN_HEADS
    qkv4 = qkv.reshape(batch, nsc, GDN_SUPER, 3 * h * LANES)
    proj4 = proj.reshape(batch, nsc, GDN_SUPER, proj.shape[-1])

    def tile(off):
        return pl.BlockSpec((1, nsc, GDN_SUPER, hps * LANES), lambda b, j: (b, 0, 0, off // hps + j))

    st = pl.BlockSpec((1, 2, hps, GDN_DK, LANES), lambda b, j: (b, 0, j, 0, 0))
    return pl.pallas_call(
        functools.partial(_gdn_kernel, nsc=nsc, hps=hps),
        grid=(batch, h // hps),
        in_specs=[tile(0), tile(h), tile(2 * h), tile(3 * h),
                  pl.BlockSpec((hps, nsc, GDN_SUPER, 8), lambda b, j: (j, b, 0, 0)),
                  pl.BlockSpec((hps, nsc, 8, GDN_SUPER), lambda b, j: (j, b, 0, 0)),
                  st, pl.BlockSpec((1, LANES), lambda b, j: (0, 0))],
        out_specs=[tile(0), st],
        out_shape=[jax.ShapeDtypeStruct((batch, nsc, GDN_SUPER, h * LANES), BF16),
                   jax.ShapeDtypeStruct((batch, 2, h, GDN_DK, LANES), F32)],
        scratch_shapes=[pltpu.VMEM((hps, nsc, GDN_SUPER, LANES), F32),
                        pltpu.VMEM((hps, nsc, GDN_SUPER, LANES), F32)],
        compiler_params=_params("parallel", "parallel"),
        name="gdn_scan",
    )(qkv4, qkv4, qkv4, proj4, cols, rows, s0, norm_g.reshape(1, LANES))


def _gdn_mixer(x, g, mod, rows_per_mod, batch, seq, s0, w_cat, conv_w, a_log, dt_bias, norm_g):
    proj = _nmm(x, g, mod, w_cat, rows_per_mod, 0, 1)
    qkv = _gdn_prep(proj.reshape(batch, seq, proj.shape[-1]), conv_w)
    gates, tot = _gdn_gates(proj, a_log, dt_bias)
    h = GDN_HEADS
    n = batch * seq
    per = jnp.stack([gates[:, 0:h], gates[:, h:2 * h], tot[:, 0:h],
                     gates[:, 2 * h:3 * h], gates[:, 3 * h:4 * h], tot[:, 2 * h:3 * h],
                     jnp.zeros((n, h), F32), jnp.zeros((n, h), F32)], axis=-1)
    per = per.reshape(n // GDN_SUPER, GDN_SUPER, h, 8)
    cols = jnp.transpose(per, (2, 0, 1, 3))
    rows = jnp.transpose(per, (2, 0, 3, 1))
    o, s_fin = _gdn_scan(qkv, proj, cols, rows, s0, norm_g, batch, seq)
    return o.reshape(n, D_MODEL), s_fin


SUBLANES = 8


def _store_token_tiles(ref, x):
    rows = x.shape[0]
    for j in range(SUBLANES):
        ref[pl.ds(j, rows, stride=SUBLANES), :] = x[:, j * LANES:(j + 1) * LANES]


def _load_token_tiles(ref, rows):
    return jnp.concatenate([ref[pl.ds(j, rows, stride=SUBLANES), :] for j in range(SUBLANES)], axis=1)


def _moe_pre_kernel(x_ref, g_ref, mod_ref, wr_ref, h_ref, aff_ref):
    h = _norm_mod(x_ref[...], g_ref[...], mod_ref, 3, 4)
    _store_token_tiles(h_ref, h)
    h1, h2 = _split_bf16(h, 2)
    logits = _dot_nt(wr_ref[0], h1) + (_dot_nt(wr_ref[1], h1) + _dot_nt(wr_ref[0], h2))
    e = jnp.exp(logits - jnp.max(logits, axis=0, keepdims=True))
    aff_ref[...] = e / jnp.sum(e, axis=0, keepdims=True)


def _moe_pre(x, g, mod, rows_per_mod, w_router, tm=512):
    m = x.shape[0]
    w_router = w_router.T
    w1 = w_router.astype(BF16)
    w2 = (w_router - w1.astype(F32)).astype(BF16)
    return pl.pallas_call(
        _moe_pre_kernel,
        grid=(m // tm,),
        in_specs=[pl.BlockSpec((tm, D_MODEL), lambda i: (i, 0)),
                  pl.BlockSpec((1, D_MODEL), lambda i: (0, 0)),
                  pl.BlockSpec((1, 6, D_MODEL), lambda i: ((i * tm) // rows_per_mod, 0, 0)),
                  pl.BlockSpec((2, N_EXPERTS, D_MODEL), lambda i: (0, 0, 0))],
        out_specs=[pl.BlockSpec((tm * SUBLANES, LANES), lambda i: (i, 0)),
                   pl.BlockSpec((N_EXPERTS, tm), lambda i: (0, i))],
        out_shape=[jax.ShapeDtypeStruct((m * SUBLANES, LANES), F32), jax.ShapeDtypeStruct((N_EXPERTS, m), F32)],
        compiler_params=_params("parallel"),
        name="moe_router",
    )(x, g.reshape(1, D_MODEL), mod, jnp.stack([w1, w2]))


def _route_kernel(aff_ref, gat_ref, sct_ref, gsel_ref, cnt_ref, rank_s, *, cap, n_tok):
    e = pl.program_id(0)
    a = aff_ref[0]
    nb = a.shape[0]
    bits = pltpu.bitcast(a, jnp.int32)

    def count(m):
        return jnp.sum(jnp.sum(jnp.where(m, 1.0, 0.0), axis=1, keepdims=True), axis=0, keepdims=True)

    def bisect(_, lohi):
        lo, hi = lohi
        mid = lo + ((hi - lo) >> 1)
        ok = count(bits >= mid) >= cap
        return jnp.where(ok, mid, lo), jnp.where(ok, hi, mid)

    lo, _ = lax.fori_loop(0, 31, bisect, (jnp.zeros((1, 1), jnp.int32), jnp.full((1, 1), 0x7F800000, jnp.int32)))
    above = bits > lo
    tied = bits == lo
    need = cap - count(above)

    li = lax.broadcasted_iota(jnp.int32, (LANES, LANES), 0)
    lj = lax.broadcasted_iota(jnp.int32, (LANES, LANES), 1)
    upper = jnp.where(li <= lj, 1.0, 0.0).astype(BF16)
    bi = lax.broadcasted_iota(jnp.int32, (nb, nb), 0)
    bj = lax.broadcasted_iota(jnp.int32, (nb, nb), 1)
    before = jnp.where(bj < bi, 1.0, 0.0).astype(BF16)

    def prefix(m):
        inside = _dot(m.astype(BF16), upper)
        total = jnp.broadcast_to(inside[:, LANES - 1:LANES], inside.shape)
        return inside, _dot(before, total.astype(BF16))

    tin, tbase = prefix(jnp.where(tied, 1.0, 0.0))
    mask = jnp.where(above | (tied & (tin + tbase <= need)), 1.0, 0.0)
    inside, base = prefix(mask)

    @pl.when(e == 0)
    def _():
        rank_s[...] = jnp.zeros_like(rank_s)

    rank = rank_s[...]
    block_end = jnp.transpose(base + jnp.broadcast_to(inside[:, LANES - 1:LANES], base.shape))[0:1, :]
    slot = lax.broadcasted_iota(jnp.int32, (cap, 1), 0).astype(F32)
    blk = jnp.sum(jnp.where(block_end <= slot, 1.0, 0.0), axis=1, keepdims=True)
    pick = jnp.where(lax.broadcasted_iota(jnp.int32, (cap, nb), 1).astype(F32) == blk, 1.0, 0.0).astype(BF16)
    base_hi = jnp.floor(base * (1.0 / 64.0))
    a1, a2, a3 = _split_bf16(a, 3)
    table = jnp.concatenate([inside.astype(BF16), base_hi.astype(BF16), (base - 64.0 * base_hi).astype(BF16),
                             rank.astype(BF16), a1, a2, a3], axis=1)
    got = _dot(pick, table)

    def part(j):
        return got[:, j * LANES:(j + 1) * LANES]

    target = slot - (64.0 * part(1)[:, 0:1] + part(2)[:, 0:1])
    lane = jnp.sum(jnp.where(part(0) <= target, 1.0, 0.0), axis=1, keepdims=True)
    here = lax.broadcasted_iota(jnp.int32, (cap, LANES), 1).astype(F32) == lane
    token = LANES * blk + lane
    rank_sel = jnp.sum(jnp.where(here, part(3), 0.0), axis=1, keepdims=True)
    gat_ref[0] = (token * SUBLANES).astype(jnp.int32)
    sct_ref[0] = ((rank_sel * n_tok + token) * SUBLANES).astype(jnp.int32)
    gsel_ref[0] = jnp.sum(jnp.where(here, (part(4) + part(5)) + part(6), 0.0), axis=1, keepdims=True)
    rank_s[...] = rank + mask
    cnt_ref[...] = (rank + mask).astype(jnp.int32)


def _route(aff3, cap):
    n_exp, nb, _ = aff3.shape
    col = pl.BlockSpec((1, cap, 1), lambda e: (e, 0, 0))
    return pl.pallas_call(
        functools.partial(_route_kernel, cap=cap, n_tok=nb * LANES),
        grid=(n_exp,),
        in_specs=[pl.BlockSpec((1, nb, LANES), lambda e: (e, 0, 0))],
        out_specs=[col, col, col, pl.BlockSpec((nb, LANES), lambda e: (0, 0))],
        out_shape=[jax.ShapeDtypeStruct((n_exp, cap, 1), jnp.int32), jax.ShapeDtypeStruct((n_exp, cap, 1), jnp.int32),
                   jax.ShapeDtypeStruct((n_exp, cap, 1), F32), jax.ShapeDtypeStruct((nb, LANES), jnp.int32)],
        scratch_shapes=[pltpu.VMEM((nb, LANES), F32)],
        compiler_params=_params("arbitrary"),
        name="moe_route",
    )(aff3)


def _ffn_kernel(g0_ref, gat_ref, sct_ref, gate_ref, wg_ref, wu_ref, wd_ref, h_hbm, z_hbm,
                xb0, xb1, yb0, yb1, wg_s, wu_s, wd_s, g_s, s_s, gsem, ssem, psem, *, tc, pairs_per_e, npairs):
    m = pl.program_id(0)
    hb = (m % 2) * (3 * tc)
    nb = 3 * tc - hb
    last = npairs - 1

    def tile_in(off, buf, i, sem):
        return pltpu.make_async_copy(h_hbm.at[pl.ds(pl.multiple_of(off, SUBLANES), SUBLANES)],
                                     buf.at[pl.ds(i * SUBLANES, SUBLANES)], sem)

    def tile_out(off, buf, i, sem):
        return pltpu.make_async_copy(buf.at[pl.ds(i * SUBLANES, SUBLANES)],
                                     z_hbm.at[pl.ds(pl.multiple_of(off, SUBLANES), SUBLANES)], sem)

    def tables(row, base):
        return (pltpu.make_async_copy(gat_ref.at[row], g_s.at[pl.ds(base, 2 * tc)], psem.at[0]),
                pltpu.make_async_copy(sct_ref.at[row], s_s.at[pl.ds(base, 3 * tc)], psem.at[1]))

    def wait_tiles(copy):
        for _ in range(tc):
            copy.wait()

    @pl.when(m == 0)
    def _():
        yb1[...] = jnp.zeros_like(yb1)
        first = pltpu.make_async_copy(g0_ref.at[0], g_s.at[pl.ds(0, tc)], psem.at[0])
        first.start()
        first.wait()
        for i in range(tc):
            tile_in(g_s[i], xb0, i, gsem.at[0]).start()
        for cp in tables(0, 0):
            cp.start()

    for cp in tables(m, hb):
        cp.wait()
    for cp in tables(jnp.minimum(m + 1, last), nb):
        cp.start()

    @pl.when(m % pairs_per_e == 0)
    def _():
        wg_s[...] = wg_ref[0, 0].astype(BF16)
        wu_s[...] = wu_ref[0, 0].astype(BF16)
        wd_s[...] = wd_ref[0, 0].astype(BF16)

    def ffn(xb, gate):
        x = _load_token_tiles(xb, tc).astype(BF16)
        half = wg_s.shape[1] // 2
        acc = None
        for c0 in (0, half):
            a = _dot(x, wg_s[:, c0:c0 + half])
            b = _dot(x, wu_s[:, c0:c0 + half])
            part = _dot((_silu(a) * b).astype(BF16), wd_s[c0:c0 + half, :])
            acc = part if acc is None else acc + part
        return acc * gate

    wait_tiles(tile_in(0, xb0, 0, gsem.at[0]))
    for i in range(tc):
        tile_in(g_s[hb + i], xb1, i, gsem.at[1]).start()
        tile_out(s_s[hb + i], yb1, i, ssem.at[1]).start()
    y = ffn(xb0, gate_ref[0, 0:tc])

    @pl.when(m > 0)
    def _():
        wait_tiles(tile_out(0, yb0, 0, ssem.at[0]))

    _store_token_tiles(yb0, y)

    wait_tiles(tile_in(0, xb1, 0, gsem.at[1]))
    for i in range(tc):
        tile_in(g_s[hb + tc + i], xb0, i, gsem.at[0]).start()
        tile_out(s_s[hb + tc + i], yb0, i, ssem.at[0]).start()
    y = ffn(xb1, gate_ref[0, tc:2 * tc])
    wait_tiles(tile_out(0, yb1, 0, ssem.at[1]))
    _store_token_tiles(yb1, y)

    @pl.when(m == last)
    def _():
        for i in range(tc):
            tile_out(s_s[hb + 2 * tc + i], yb1, i, ssem.at[1]).start()
        wait_tiles(tile_out(0, yb1, 0, ssem.at[1]))
        wait_tiles(tile_out(0, yb0, 0, ssem.at[0]))
        wait_tiles(tile_in(0, xb0, 0, gsem.at[0]))
        for cp in tables(last, nb):
            cp.wait()


def _expert_ffn(h_tiles, gat, sct, gate, w_gate, w_up, w_down, layer, z_rows, tc=512):
    n_exp, cap = gat.shape
    ff = w_gate.shape[-1]
    tc = min(tc, cap // 2)
    pairs_per_e = cap // (2 * tc)
    npairs = n_exp * pairs_per_e
    nsteps = 2 * npairs
    gat = gat.reshape(nsteps, tc)
    sct = sct.reshape(nsteps, tc)
    spare = ((z_rows - tc + jnp.arange(tc, dtype=jnp.int32)) * SUBLANES)[None]
    gat_next = jnp.concatenate([gat[1:], gat[:1]], axis=0).reshape(npairs, 2 * tc)
    sct_prev = jnp.concatenate([spare, sct[:-1]], axis=0).reshape(npairs, 2 * tc)
    sct3 = jnp.concatenate([sct_prev, sct[1::2]], axis=1)
    tile_buf = pltpu.VMEM((tc * SUBLANES, LANES), F32)
    wspec_in = pl.BlockSpec((1, 1, D_MODEL, ff), lambda m: (layer, m // pairs_per_e, 0, 0))
    return pl.pallas_call(
        functools.partial(_ffn_kernel, tc=tc, pairs_per_e=pairs_per_e, npairs=npairs),
        grid=(npairs,),
        in_specs=[pl.BlockSpec((1, tc), lambda m: (0, 0)),
                  pl.BlockSpec((npairs, 2 * tc), lambda m: (0, 0)),
                  pl.BlockSpec((npairs, 3 * tc), lambda m: (0, 0)),
                  pl.BlockSpec((1, 2 * tc, 1), lambda m: (m, 0, 0)),
                  wspec_in, wspec_in,
                  pl.BlockSpec((1, 1, ff, D_MODEL), lambda m: (layer, m // pairs_per_e, 0, 0)),
                  pl.BlockSpec(memory_space=pl.ANY)],
        out_specs=pl.BlockSpec(memory_space=pl.ANY),
        out_shape=jax.ShapeDtypeStruct((z_rows * SUBLANES, LANES), F32),
        scratch_shapes=[tile_buf, tile_buf, tile_buf, tile_buf,
                        pltpu.VMEM((D_MODEL, ff), BF16), pltpu.VMEM((D_MODEL, ff), BF16),
                        pltpu.VMEM((ff, D_MODEL), BF16),
                        pltpu.SMEM((6 * tc,), jnp.int32), pltpu.SMEM((6 * tc,), jnp.int32),
                        pltpu.SemaphoreType.DMA((2,)), pltpu.SemaphoreType.DMA((2,)),
                        pltpu.SemaphoreType.DMA((2,))],
        compiler_params=_params("arbitrary"),
        name="expert_ffn",
    )(gat[:1], gat_next, sct3, gate.reshape(npairs, 2 * tc, 1), w_gate, w_up, w_down, h_tiles)


def _combine_kernel(mc_ref, x_ref, cnt_ref, mod_ref, fg_ref, z_hbm, o_ref, zbuf, acc, sem, *, gi, final, tt, n_tok):
    i = pl.program_id(0)
    mc = mc_ref[i]

    def plane(r, sl):
        row0 = pl.multiple_of((r * n_tok + i * tt) * SUBLANES, SUBLANES)
        return pltpu.make_async_copy(z_hbm.at[pl.ds(row0, tt * SUBLANES)], zbuf.at[sl], sem.at[sl])

    @pl.when(mc > 0)
    def _():
        plane(0, 0).start()

    acc[...] = jnp.zeros_like(acc)
    cnt = cnt_ref[...]

    def body(r, c):
        sl = r % 2

        @pl.when(r + 1 < mc)
        def _():
            plane(r + 1, 1 - sl).start()

        plane(r, sl).wait()
        keep = cnt > r
        for j in range(SUBLANES):
            acc[:, j * LANES:(j + 1) * LANES] += jnp.where(
                keep, zbuf[sl, pl.ds(j, tt, stride=SUBLANES), :], 0.0)
        return c

    lax.fori_loop(0, mc, body, 0)
    x = x_ref[...] + mod_ref[0, gi:gi + 1, :] * acc[...]
    if final:
        x = _rms(x, fg_ref[...])
    o_ref[...] = x


def _combine(x, z, cnt, maxcnt, mod, rows_per_mod, gi, final_g, final, tt=256):
    m = x.shape[0]
    return pl.pallas_call(
        functools.partial(_combine_kernel, gi=gi, final=final, tt=tt, n_tok=m),
        grid_spec=pltpu.PrefetchScalarGridSpec(
            num_scalar_prefetch=1,
            grid=(m // tt,),
            in_specs=[pl.BlockSpec((tt, D_MODEL), lambda i, mc: (i, 0)),
                      pl.BlockSpec((tt, 1), lambda i, mc: (i, 0)),
                      pl.BlockSpec((1, 6, D_MODEL), lambda i, mc: ((i * tt) // rows_per_mod, 0, 0)),
                      pl.BlockSpec((1, D_MODEL), lambda i, mc: (0, 0)),
                      pl.BlockSpec(memory_space=pl.ANY)],
            out_specs=pl.BlockSpec((tt, D_MODEL), lambda i, mc: (i, 0)),
            scratch_shapes=[pltpu.VMEM((2, tt * SUBLANES, LANES), F32), pltpu.VMEM((tt, D_MODEL), F32),
                            pltpu.SemaphoreType.DMA((2,))]),
        out_shape=jax.ShapeDtypeStruct((m, D_MODEL), F32),
        compiler_params=_params("arbitrary"),
        name="moe_combine",
    )(maxcnt, x, cnt.reshape(m, 1), mod, final_g.reshape(1, D_MODEL), z)


def _expert_choice(x, g, mod, rows_per_mod, w_router, w_gate, w_up, w_down, layer, gi, final_g, final):
    n = x.shape[0]
    cap = EC_CAPACITY_FACTOR * n // N_EXPERTS
    tt = min(256, n)
    h, aff = _moe_pre(x, g, mod, rows_per_mod, w_router)
    gat, sct, gate, cnt = _route(aff.reshape(N_EXPERTS, n // LANES, LANES), cap)
    gat, sct, gate = (a.reshape(N_EXPERTS, cap) for a in (gat, sct, gate))
    cnt = cnt.reshape(n)
    maxcnt = jnp.max(cnt.reshape(n // tt, tt), axis=1)
    z = _expert_ffn(h, gat, sct, gate, w_gate, w_up, w_down, layer, N_EXPERTS * n + cap)
    return _combine(x, z, cnt, maxcnt, mod, rows_per_mod, gi, final_g, final, tt)


def _trunk(x, mods, rows_per_mod, batch, seq, caches, p):
    depth = p["mod_w"].shape[0]
    n = batch * seq
    new = {}
    for layer in range(depth):
        kind, slot = layer % 4, layer // 4
        mod = mods[layer]
        g_mix = p["norm_mix_g"][layer]
        if kind == 0:
            w_ab = jnp.pad(p["gdn_w_ab"][slot], ((0, 0), (0, LANES - 4 * GDN_HEADS)))
            w_cat = jnp.concatenate([p["gdn_w_in"][slot], w_ab], axis=1).astype(BF16)
            if caches is None:
                s0 = jnp.zeros((batch, 2, GDN_HEADS, GDN_DK, LANES), F32)
            else:
                s0 = caches["state_gdn"][:, slot]
            o, s_fin = _gdn_mixer(x, g_mix, mod, rows_per_mod, batch, seq, s0, w_cat, p["gdn_conv_w"][slot],
                                  p["gdn_a_log"][slot], p["gdn_dt_bias"][slot], p["gdn_norm_g"][slot])
            new.setdefault("gdn", []).append(s_fin)
            w_out = p["gdn_w_out"][slot]
        elif kind == 1:
            proj = _nmm(x, g_mix, mod, p["diff_w_in"][slot].astype(BF16), rows_per_mod, 0, 1)
            nk = seq
            if caches is None:
                assert DIFF_HEADS == SUBLANES
                q, k, v, kc, vc = _attn_prep(proj, "diff", seq, False)
                new.setdefault("diff_k", []).append(kc.reshape(batch, seq, DIFF_HEADS, LANES))
                new.setdefault("diff_v", []).append(vc.reshape(batch, seq, DIFF_HEADS, LANES))
            else:
                q, k, v = _attn_prep(proj, "diff", seq, True)
                kc = caches["cache_diff_k"][:, slot].reshape(batch, -1, DIFF_HEADS * LANES).astype(BF16)
                vc = caches["cache_diff_v"][:, slot].reshape(batch, -1, DIFF_HEADS * LANES).astype(BF16)
                nk = seq + kc.shape[1]
                k = jnp.concatenate([k.reshape(batch, seq, -1), kc], axis=1).reshape(batch * nk, -1)
                v = jnp.concatenate([v.reshape(batch, seq, -1), vc], axis=1).reshape(batch * nk, -1)
            lam_init = 0.8 - 0.6 * math.exp(-0.3 * layer)
            o = _attention(q, k, v, "diff", batch, seq, nk,
                           (p["diff_lambda"][slot], p["diff_subln_g"][slot]), lam_init)
            w_out = p["diff_w_out"][slot]
        elif kind == 2:
            proj = _nmm(x, g_mix, mod, p["gqa_w_in"][slot].astype(BF16), rows_per_mod, 0, 1)
            q, k, v, kf = _attn_prep(proj, "gqa", seq, caches is not None,
                                     (p["gqa_q_norm_g"][slot], p["gqa_k_norm_g"][slot]))
            nk = seq
            if caches is None:
                v0 = (GQA_HEADS + GQA_KV) * LANES
                new.setdefault("gqa_k", []).append(kf.reshape(batch, seq, GQA_KV, GQA_HD))
                new.setdefault("gqa_v", []).append(proj[:, v0:].reshape(batch, seq, GQA_KV, GQA_HD))
            else:
                kc = caches["cache_gqa_k"][:, slot].reshape(batch, -1, GQA_KV * GQA_HD).astype(BF16)
                vc = caches["cache_gqa_v"][:, slot].reshape(batch, -1, GQA_KV * GQA_HD).astype(BF16)
                nk = seq + kc.shape[1]
                k = jnp.concatenate([k.reshape(batch, seq, -1), kc], axis=1).reshape(batch * nk, -1)
                v = jnp.concatenate([v.reshape(batch, seq, -1), vc], axis=1).reshape(batch * nk, -1)
            o = _attention(q, k, v, "gqa", batch, seq, nk)
            w_out = p["gqa_w_out"][slot]
        else:
            proj = _nmm(x, g_mix, mod, p["swa_w_in"][slot].astype(BF16), rows_per_mod, 0, 1)
            q, k, v = _attn_prep(proj, "swa", seq, caches is not None)
            sinks = p["swa_sinks"][slot].astype(F32)
            if caches is None:
                k0 = SWA_HEADS * SWA_HD
                v0 = k0 + SWA_KV * SWA_HD
                new.setdefault("swa_k", []).append(proj[:, k0:v0].reshape(batch, seq, SWA_KV, SWA_HD))
                new.setdefault("swa_v", []).append(proj[:, v0:].reshape(batch, seq, SWA_KV, SWA_HD))
                o = _attention(q, k, v, "swa", batch, seq, seq, (sinks,))
            else:
                kc = jnp.pad(caches["cache_swa_k"][:, slot], ((0, 0), (0, 0), (0, 0), (0, LANES - SWA_HD)))
                past = kc.shape[1]
                kc = kc.reshape(batch, past, SWA_KV * LANES).astype(BF16)
                vc = caches["cache_swa_v"][:, slot].reshape(batch, past, SWA_KV * SWA_HD).astype(BF16)
                nk = seq + past
                k = jnp.concatenate([k.reshape(batch, seq, -1), kc], axis=1).reshape(batch * nk, -1)
                v = jnp.concatenate([v.reshape(batch, seq, -1), vc], axis=1).reshape(batch * nk, -1)
                o = _band_attention(q, k, v, sinks, batch, seq, past)
            w_out = p["swa_w_out"][slot]
        x = _mm_resid(o, w_out.astype(BF16), x, mod, rows_per_mod, 2)
        x = _expert_choice(x, p["norm_ffn_g"][layer], mod, rows_per_mod, p["moe_router"][layer],
                           p["moe_w_gate"], p["moe_w_up"], p["moe_w_down"], layer,
                           5, p["final_g"], layer == depth - 1)
    return x, new


def kernel(x_prompt, x_sample, state_gdn, cache_diff_k, cache_diff_v, cache_gqa_k, cache_gqa_v, cache_swa_k, cache_swa_v, c, c_ctx, mod_w, mod_b, norm_mix_g, norm_ffn_g, final_g, gdn_w_in, gdn_conv_w, gdn_w_ab, gdn_a_log, gdn_dt_bias, gdn_norm_g, gdn_w_out, diff_w_in, diff_lambda, diff_subln_g, diff_w_out, gqa_w_in, gqa_q_norm_g, gqa_k_norm_g, gqa_w_out, swa_w_in, swa_sinks, swa_w_out, moe_router, moe_w_gate, moe_w_up, moe_w_down):
    p = dict(mod_w=mod_w, mod_b=mod_b, norm_mix_g=norm_mix_g, norm_ffn_g=norm_ffn_g, final_g=final_g,
             gdn_w_in=gdn_w_in, gdn_conv_w=gdn_conv_w, gdn_w_ab=gdn_w_ab, gdn_a_log=gdn_a_log,
             gdn_dt_bias=gdn_dt_bias, gdn_norm_g=gdn_norm_g, gdn_w_out=gdn_w_out,
             diff_w_in=diff_w_in, diff_lambda=diff_lambda, diff_subln_g=diff_subln_g, diff_w_out=diff_w_out,
             gqa_w_in=gqa_w_in, gqa_q_norm_g=gqa_q_norm_g, gqa_k_norm_g=gqa_k_norm_g, gqa_w_out=gqa_w_out,
             swa_w_in=swa_w_in, swa_sinks=swa_sinks, swa_w_out=swa_w_out,
             moe_router=moe_router, moe_w_gate=moe_w_gate, moe_w_up=moe_w_up, moe_w_down=moe_w_down)
    batch, seq, _ = x_prompt.shape
    dec_batch, dec_seq, _ = x_sample.shape
    depth = mod_w.shape[0]
    rows = 8 * ((1 + dec_batch + 7) // 8)
    cond = jnp.zeros((rows, D_MODEL), F32).at[0].set(c_ctx).at[1:1 + dec_batch].set(c)
    mod_all = _mod_proj(cond, mod_w, mod_b).reshape(depth, rows, 6, D_MODEL)
    mods_ctx = [mod_all[l, 0:1] for l in range(depth)]
    mods_lat = [mod_all[l, 1:1 + dec_batch] for l in range(depth)]

    y_prompt, st = _trunk(x_prompt.reshape(batch * seq, D_MODEL), mods_ctx, batch * seq, batch, seq, None, p)
    caches = dict(state_gdn=state_gdn, cache_diff_k=cache_diff_k, cache_diff_v=cache_diff_v,
                  cache_gqa_k=cache_gqa_k, cache_gqa_v=cache_gqa_v,
                  cache_swa_k=cache_swa_k, cache_swa_v=cache_swa_v)
    y_sample, _ = _trunk(x_sample.reshape(dec_batch * dec_seq, D_MODEL), mods_lat, dec_seq, dec_batch, dec_seq,
                         caches, p)

    def stack(xs):
        return jnp.stack(xs, axis=1)

    return (y_prompt.reshape(batch, seq, D_MODEL), y_sample.reshape(dec_batch, dec_seq, D_MODEL),
            stack(st["gdn"]), stack(st["diff_k"]), stack(st["diff_v"]),
            stack(st["gqa_k"]), stack(st["gqa_v"]), stack(st["swa_k"]), stack(st["swa_v"]))
```

```python
import functools
import math

import jax
import jax.numpy as jnp
from jax import lax
from jax.experimental import pallas as pl
from jax.experimental.pallas import tpu as pltpu

F32 = jnp.float32
BF16 = jnp.bfloat16

D_MODEL = 1024
GRID_W = 64
ROPE_THETA = 10000.0
NORM_EPS = 1e-6
LANES = 128
GDN_HEADS = 8
GDN_DK = 128
GDN_CHUNK = 64
GDN_SUPER = 256
DIFF_HEADS = 8
DIFF_DH = 64
GQA_HEADS = 8
GQA_KV = 2
GQA_HD = 128
SWA_HEADS = 16
SWA_KV = 4
SWA_HD = 64
WINDOW = 128
N_EXPERTS = 16
EC_CAPACITY_FACTOR = 2
VMEM_LIMIT_BYTES = 56 * 1024 * 1024


def _params(*sem):
    return pltpu.CompilerParams(dimension_semantics=sem, vmem_limit_bytes=VMEM_LIMIT_BYTES)


def _silu(x):
    return x / (1.0 + jnp.exp(-x))


def _rms(x, g):
    return x * lax.rsqrt(jnp.mean(x * x, axis=-1, keepdims=True) + NORM_EPS) * g


def _dot(a, b):
    return jnp.dot(a, b, preferred_element_type=F32)


def _dot_nt(a, b):
    return lax.dot_general(a, b, (((1,), (1,)), ((), ())), preferred_element_type=F32)


def _split_bf16(x, parts):
    out = []
    for _ in range(parts - 1):
        hi = x.astype(BF16)
        out.append(hi)
        x = x - hi.astype(F32)
    out.append(x.astype(BF16))
    return out


def _mod_kernel(c_ref, w_ref, b_ref, o_ref):
    a = _silu(c_ref[...]).astype(BF16)
    o_ref[0] = _dot(a, w_ref[0].astype(BF16)) + b_ref[0]


def _mod_proj(cond, mod_w, mod_b):
    depth, _, width = mod_w.shape
    rows = cond.shape[0]
    tn = 1536
    return pl.pallas_call(
        _mod_kernel,
        grid=(depth, width // tn),
        in_specs=[pl.BlockSpec((rows, D_MODEL), lambda l, j: (0, 0)),
                  pl.BlockSpec((1, D_MODEL, tn), lambda l, j: (l, 0, j)),
                  pl.BlockSpec((1, 1, tn), lambda l, j: (l, 0, j))],
        out_specs=pl.BlockSpec((1, rows, tn), lambda l, j: (l, 0, j)),
        out_shape=jax.ShapeDtypeStruct((depth, rows, width), F32),
        compiler_params=_params("parallel", "parallel"),
        name="mod_proj",
    )(cond, mod_w, mod_b.reshape(depth, 1, width))


def _norm_mod(x, g, mod_ref, sh, sc):
    return _rms(x, g) * (1.0 + mod_ref[0, sc:sc + 1, :]) + mod_ref[0, sh:sh + 1, :]


def _nmm_kernel(x_ref, g_ref, mod_ref, w_ref, o_ref, *, sh, sc, chunk):
    h = _norm_mod(x_ref[...], g_ref[...], mod_ref, sh, sc).astype(BF16)
    width = w_ref.shape[1]
    for c0 in range(0, width, chunk):
        c1 = min(c0 + chunk, width)
        o_ref[:, c0:c1] = _dot(h, w_ref[:, c0:c1])


def _nmm(x, g, mod, w, rows_per_mod, sh, sc, tm=256):
    m = x.shape[0]
    width = w.shape[1]
    return pl.pallas_call(
        functools.partial(_nmm_kernel, sh=sh, sc=sc, chunk=512),
        grid=(m // tm,),
        in_specs=[pl.BlockSpec((tm, D_MODEL), lambda i: (i, 0)),
                  pl.BlockSpec((1, D_MODEL), lambda i: (0, 0)),
                  pl.BlockSpec((1, 6, D_MODEL), lambda i: ((i * tm) // rows_per_mod, 0, 0)),
                  pl.BlockSpec((D_MODEL, width), lambda i: (0, 0))],
        out_specs=pl.BlockSpec((tm, width), lambda i: (i, 0)),
        out_shape=jax.ShapeDtypeStruct((m, width), F32),
        compiler_params=_params("parallel"),
        name="norm_mod_matmul",
    )(x, g.reshape(1, D_MODEL), mod, w)


def _mmr_kernel(a_ref, w_ref, r_ref, mod_ref, o_ref, *, gi):
    acc = _dot(a_ref[...].astype(BF16), w_ref[...])
    o_ref[...] = r_ref[...] + mod_ref[0, gi:gi + 1, :] * acc


def _mm_resid(a, w, res, mod, rows_per_mod, gi, tm=512):
    m = a.shape[0]
    return pl.pallas_call(
        functools.partial(_mmr_kernel, gi=gi),
        grid=(m // tm,),
        in_specs=[pl.BlockSpec((tm, D_MODEL), lambda i: (i, 0)),
                  pl.BlockSpec((D_MODEL, D_MODEL), lambda i: (0, 0)),
                  pl.BlockSpec((tm, D_MODEL), lambda i: (i, 0)),
                  pl.BlockSpec((1, 6, D_MODEL), lambda i: ((i * tm) // rows_per_mod, 0, 0))],
        out_specs=pl.BlockSpec((tm, D_MODEL), lambda i: (i, 0)),
        out_shape=jax.ShapeDtypeStruct((m, D_MODEL), F32),
        compiler_params=_params("parallel"),
        name="out_proj_resid",
    )(a, w, res, mod)


def _rope_tables(n_tokens, dim):
    rows = n_tokens // GRID_W
    row = jnp.broadcast_to(jnp.arange(rows, dtype=F32)[:, None], (rows, GRID_W)).reshape(-1)
    col = jnp.broadcast_to(jnp.arange(GRID_W, dtype=F32)[None, :], (rows, GRID_W)).reshape(-1)
    n_freq = dim // 4
    inv = ROPE_THETA ** (-jnp.arange(n_freq, dtype=F32) / n_freq)
    ang = jnp.concatenate([row[:, None] * inv, col[:, None] * inv], axis=-1)
    cos, sin = jnp.cos(ang), jnp.sin(ang)
    zero = jnp.zeros_like(sin)
    reps = LANES // dim
    cos_t = jnp.tile(jnp.concatenate([cos, cos], axis=-1), (1, reps))
    sin_a = jnp.tile(jnp.concatenate([-sin, zero], axis=-1), (1, reps))
    sin_b = jnp.tile(jnp.concatenate([zero, sin], axis=-1), (1, reps))
    return cos_t, sin_a, sin_b


def _rope(x, cos, sin_a, sin_b, dim):
    half = dim // 2
    return x * cos + pltpu.roll(x, LANES - half, 1) * sin_a + pltpu.roll(x, half, 1) * sin_b


def _prep_kernel(*refs, kind, rope):
    if kind == "gqa":
        p_ref, cos_ref, sa_ref, sb_ref, qg_ref, kg_ref, q_out, k_out, v_out, kf_out = refs
    elif kind == "diff" and not rope:
        p_ref, cos_ref, sa_ref, sb_ref, q_out, k_out, v_out, kc_out, vc_out = refs
        hw = DIFF_HEADS * LANES
        _store_token_tiles(kc_out, p_ref[:, hw:2 * hw])
        _store_token_tiles(vc_out, p_ref[:, 2 * hw:3 * hw])
    else:
        p_ref, cos_ref, sa_ref, sb_ref, q_out, k_out, v_out = refs
    cos, sa, sb = cos_ref[...], sa_ref[...], sb_ref[...]

    def blk(j):
        return p_ref[:, j * LANES:(j + 1) * LANES]

    if kind == "gqa":
        scale = GQA_HD ** -0.5 * LOG2E
        for j in range(GQA_HEADS):
            y = _rms(blk(j), qg_ref[...])
            if rope:
                y = _rope(y, cos, sa, sb, GQA_HD)
            q_out[:, j * LANES:(j + 1) * LANES] = (y * scale).astype(BF16)
        for j in range(GQA_KV):
            y = _rms(blk(GQA_HEADS + j), kg_ref[...])
            kf_out[:, j * LANES:(j + 1) * LANES] = y
            if rope:
                y = _rope(y, cos, sa, sb, GQA_HD)
            k_out[:, j * LANES:(j + 1) * LANES] = y.astype(BF16)
        v0 = (GQA_HEADS + GQA_KV) * LANES
        v_out[...] = p_ref[:, v0:v0 + GQA_KV * LANES].astype(BF16)
    elif kind == "diff":
        scale = DIFF_DH ** -0.5 * LOG2E
        for j in range(DIFF_HEADS):
            y = blk(j)
            if rope:
                y = _rope(y, cos, sa, sb, DIFF_DH)
            q_out[:, j * LANES:(j + 1) * LANES] = (y * scale).astype(BF16)
            y = blk(DIFF_HEADS + j)
            if rope:
                y = _rope(y, cos, sa, sb, DIFF_DH)
            k_out[:, j * LANES:(j + 1) * LANES] = y.astype(BF16)
        v_out[...] = p_ref[:, 2 * DIFF_HEADS * LANES:3 * DIFF_HEADS * LANES].astype(BF16)
    else:
        scale = SWA_HD ** -0.5 * LOG2E
        low = lax.broadcasted_iota(jnp.int32, cos.shape, 1) < SWA_HD

        def pad_pair(y, out, j):
            out[:, (2 * j) * LANES:(2 * j + 1) * LANES] = jnp.where(low, y, 0.0).astype(BF16)
            out[:, (2 * j + 1) * LANES:(2 * j + 2) * LANES] = jnp.where(
                low, pltpu.roll(y, SWA_HD, 1), 0.0).astype(BF16)

        nq = SWA_HEADS // 2
        for j in range(nq):
            y = blk(j)
            if rope:
                y = _rope(y, cos, sa, sb, SWA_HD)
            pad_pair(y * scale, q_out, j)
        for j in range(SWA_KV // 2):
            y = blk(nq + j)
            if rope:
                y = _rope(y, cos, sa, sb, SWA_HD)
            pad_pair(y, k_out, j)
        v0 = (nq + SWA_KV // 2) * LANES
        v_out[...] = p_ref[:, v0:v0 + (SWA_KV // 2) * LANES].astype(BF16)


def _attn_prep(proj, kind, seq, rope, gains=None, tm=256):
    m, width = proj.shape
    dim = GQA_HD if kind == "gqa" else DIFF_DH
    cos, sa, sb = _rope_tables(seq, dim)
    nt = seq // tm
    if kind == "gqa":
        wq, wk, wv = GQA_HEADS * LANES, GQA_KV * LANES, GQA_KV * LANES
    elif kind == "diff":
        wq, wk, wv = DIFF_HEADS * LANES, DIFF_HEADS * LANES, DIFF_HEADS * LANES
    else:
        wq, wk, wv = SWA_HEADS * LANES, SWA_KV * LANES, (SWA_KV // 2) * LANES
    tab = pl.BlockSpec((tm, LANES), lambda i: (i % nt, 0))
    in_specs = [pl.BlockSpec((tm, width), lambda i: (i, 0)), tab, tab, tab]
    args = [proj, cos, sa, sb]
    out_shapes = [jax.ShapeDtypeStruct((m, wq), BF16), jax.ShapeDtypeStruct((m, wk), BF16),
                  jax.ShapeDtypeStruct((m, wv), BF16)]
    out_specs = [pl.BlockSpec((tm, wq), lambda i: (i, 0)), pl.BlockSpec((tm, wk), lambda i: (i, 0)),
                 pl.BlockSpec((tm, wv), lambda i: (i, 0))]
    if kind == "gqa":
        gspec = pl.BlockSpec((1, LANES), lambda i: (0, 0))
        in_specs += [gspec, gspec]
        args += [gains[0].reshape(1, LANES), gains[1].reshape(1, LANES)]
        out_shapes.append(jax.ShapeDtypeStruct((m, wk), F32))
        out_specs.append(pl.BlockSpec((tm, wk), lambda i: (i, 0)))
    elif kind == "diff" and not rope:
        out_shapes += [jax.ShapeDtypeStruct((m * SUBLANES, LANES), F32)] * 2
        out_specs += [pl.BlockSpec((tm * SUBLANES, LANES), lambda i: (i, 0))] * 2
    return pl.pallas_call(
        functools.partial(_prep_kernel, kind=kind, rope=rope),
        grid=(m // tm,),
        in_specs=in_specs,
        out_specs=out_specs,
        out_shape=out_shapes,
        compiler_params=_params("parallel"),
        name="attn_prep_" + kind,
    )(*args)


LOG2E = 1.4426950408889634


def _with_ones(v):
    return jnp.concatenate([v, jnp.ones_like(v)], axis=1)


def _softmax_pv(s, mx, v_ones):
    e = jnp.exp2(s - mx).astype(BF16)
    acc = _dot(e, v_ones)
    return acc[:, :LANES], acc[:, LANES:LANES + 1]


def _pick_half(o_a, o_b, upper):
    low = lax.broadcasted_iota(jnp.int32, o_a.shape, 1) < SWA_HD
    if upper:
        return jnp.where(low, pltpu.roll(o_a, SWA_HD, 1), o_b)
    return jnp.where(low, o_a, pltpu.roll(o_b, SWA_HD, 1))


def _attn_kernel(*refs, mode, lam_init):
    if mode == "gqa":
        q_ref, k_ref, v_ref, o_ref = refs
    elif mode == "diff":
        q_ref, k_ref, v_ref, lam_ref, g_ref, o_ref = refs
    else:
        q_ref, k_ref, v_ref, sink_ref, o_ref = refs
    v = _with_ones(v_ref[...])
    if mode == "gqa":
        k = k_ref[...]
        for j in range(GQA_HEADS // GQA_KV):
            s = _dot_nt(q_ref[:, j * LANES:(j + 1) * LANES], k)
            num, den = _softmax_pv(s, jnp.max(s, axis=-1, keepdims=True), v)
            o_ref[:, j * LANES:(j + 1) * LANES] = (num / den).astype(BF16)
    elif mode == "diff":
        k = k_ref[...]
        q = q_ref[...]
        low = lax.broadcasted_iota(jnp.int32, q.shape, 1) < DIFF_DH
        lv = lam_ref[...]
        lam = (jnp.exp(jnp.sum(lv[0:1] * lv[1:2], axis=-1, keepdims=True))
               - jnp.exp(jnp.sum(lv[2:3] * lv[3:4], axis=-1, keepdims=True)) + lam_init)
        s0 = _dot_nt(jnp.where(low, q, 0.0).astype(BF16), k)
        s1 = _dot_nt(jnp.where(low, 0.0, q).astype(BF16), k)
        n0, d0 = _softmax_pv(s0, jnp.max(s0, axis=-1, keepdims=True), v)
        n1, d1 = _softmax_pv(s1, jnp.max(s1, axis=-1, keepdims=True), v)
        o = n0 / d0 - n1 * (lam / d1)
        o_ref[...] = (_rms(o, g_ref[...]) * (1.0 - lam_init)).astype(BF16)
    else:
        heads = SWA_HEADS // (SWA_KV // 2)
        group = SWA_HEADS // SWA_KV
        outs = []
        for j in range(heads):
            a = j // group
            s = _dot_nt(q_ref[:, j * LANES:(j + 1) * LANES], k_ref[:, a * LANES:(a + 1) * LANES])
            sink = sink_ref[0, :, j:j + 1] * LOG2E
            mx = jnp.maximum(jnp.max(s, axis=-1, keepdims=True), sink)
            num, den = _softmax_pv(s, mx, v)
            outs.append(num / (den + jnp.exp2(sink - mx)))
        for jj in range(heads // 2):
            a = (2 * jj) // group
            o_ref[:, jj * LANES:(jj + 1) * LANES] = _pick_half(outs[2 * jj], outs[2 * jj + 1], a == 1).astype(BF16)


def _attention(q, k, v, mode, batch, seq, n_keys, extra=(), lam_init=0.0, tq=256):
    nq = seq // tq
    if mode == "gqa":
        groups, wq, wk, wv, wo = GQA_KV, (GQA_HEADS // GQA_KV) * LANES, LANES, LANES, (GQA_HEADS // GQA_KV) * LANES
    elif mode == "diff":
        groups, wq, wk, wv, wo = DIFF_HEADS, LANES, LANES, LANES, LANES
    else:
        groups, wq, wk, wv, wo = SWA_KV // 2, SWA_HEADS // 2 * LANES, 2 * LANES, LANES, 4 * LANES
    in_specs = [pl.BlockSpec((tq, wq), lambda b, g, i: (b * nq + i, g)),
                pl.BlockSpec((n_keys, wk), lambda b, g, i: (b, g)),
                pl.BlockSpec((n_keys, wv), lambda b, g, i: (b, g))]
    args = [q, k, v]
    if mode == "diff":
        lam_vecs, subln_g = extra
        in_specs += [pl.BlockSpec((4, DIFF_DH), lambda b, g, i: (0, 0)),
                     pl.BlockSpec((1, LANES), lambda b, g, i: (0, 0))]
        args += [lam_vecs, subln_g.reshape(1, LANES)]
    elif mode == "swa":
        (sinks,) = extra
        in_specs.append(pl.BlockSpec((1, 1, SWA_HEADS // 2), lambda b, g, i: (g, 0, 0)))
        args.append(sinks.reshape(2, 1, SWA_HEADS // 2))
    return pl.pallas_call(
        functools.partial(_attn_kernel, mode=mode, lam_init=lam_init),
        grid=(batch, groups, nq),
        in_specs=in_specs,
        out_specs=pl.BlockSpec((tq, wo), lambda b, g, i: (b * nq + i, g)),
        out_shape=jax.ShapeDtypeStruct((batch * seq, D_MODEL), BF16),
        compiler_params=_params("parallel", "parallel", "parallel"),
        name="attn_" + mode,
    )(*args)


def _band_kernel(q_ref, kp_ref, kc_ref, kn_ref, kx_ref, vp_ref, vc_ref, vn_ref, vx_ref, sink_ref, o_ref, *, nqb):
    i = pl.program_id(2)
    qi = lax.broadcasted_iota(jnp.int32, (WINDOW, WINDOW), 0)
    kj = lax.broadcasted_iota(jnp.int32, (WINDOW, WINDOW), 1)
    open_blk = jnp.zeros((WINDOW, WINDOW), F32)
    bias = jnp.concatenate([jnp.where((kj >= qi) & (i > 0), 0.0, -jnp.inf), open_blk,
                            jnp.where((kj <= qi) & (i < nqb - 1), 0.0, -jnp.inf),
                            jnp.zeros((WINDOW, kx_ref.shape[0]), F32)], axis=1)
    keys = jnp.concatenate([kp_ref[...], kc_ref[...], kn_ref[...], kx_ref[...]], axis=0)
    vals = _with_ones(jnp.concatenate([vp_ref[...], vc_ref[...], vn_ref[...], vx_ref[...]], axis=0))
    heads = SWA_HEADS // (SWA_KV // 2)
    group = SWA_HEADS // SWA_KV
    outs = []
    for j in range(heads):
        a = j // group
        s = _dot_nt(q_ref[:, j * LANES:(j + 1) * LANES], keys[:, a * LANES:(a + 1) * LANES]) + bias
        sink = sink_ref[0, :, j:j + 1] * LOG2E
        mx = jnp.maximum(jnp.max(s, axis=-1, keepdims=True), sink)
        num, den = _softmax_pv(s, mx, vals)
        outs.append(num / (den + jnp.exp2(sink - mx)))
    for jj in range(heads // 2):
        a = (2 * jj) // group
        o_ref[:, jj * LANES:(jj + 1) * LANES] = _pick_half(outs[2 * jj], outs[2 * jj + 1], a == 1).astype(BF16)


def _band_attention(q, k, v, sinks, batch, seq, past):
    nqb = seq // WINDOW
    nkb = (seq + past) // WINDOW
    pairs = SWA_KV // 2
    xb = past // WINDOW
    assert past % WINDOW == 0 and seq % past == 0

    def band(shift, width):
        return pl.BlockSpec((WINDOW, width),
                            lambda b, g, i: (b * nkb + jnp.clip(i + shift, 0, nqb - 1), g))

    def ctx(width):
        return pl.BlockSpec((past, width), lambda b, g, i: (b * (nkb // xb) + seq // past, g))

    return pl.pallas_call(
        functools.partial(_band_kernel, nqb=nqb),
        grid=(batch, pairs, nqb),
        in_specs=[pl.BlockSpec((WINDOW, SWA_HEADS // 2 * LANES), lambda b, g, i: (b * nqb + i, g)),
                  band(-1, 2 * LANES), band(0, 2 * LANES), band(1, 2 * LANES), ctx(2 * LANES),
                  band(-1, LANES), band(0, LANES), band(1, LANES), ctx(LANES),
                  pl.BlockSpec((1, 1, SWA_HEADS // 2), lambda b, g, i: (g, 0, 0))],
        out_specs=pl.BlockSpec((WINDOW, 4 * LANES), lambda b, g, i: (b * nqb + i, g)),
        out_shape=jax.ShapeDtypeStruct((batch * seq, D_MODEL), BF16),
        compiler_params=_params("parallel", "parallel", "parallel"),
        name="attn_band",
    )(q, k, k, k, k, v, v, v, v, sinks.reshape(2, 1, SWA_HEADS // 2))


def _gdn_prep_kernel(p_ref, w_ref, o_ref):
    x = p_ref[0]
    t = x.shape[0]
    w = w_ref[...]
    row = lax.broadcasted_iota(jnp.int32, x.shape, 0)
    xm = jnp.where(row == 0, 0.0, pltpu.roll(x, 1, 0))
    xp = jnp.where(row == t - 1, 0.0, pltpu.roll(x, t - 1, 0))
    y = _silu(xm * w[0:1] + x * w[1:2] + xp * w[2:3])
    rs = lax.rsqrt(jnp.sum(y * y, axis=-1, keepdims=True) + NORM_EPS)
    j = pl.program_id(1)
    is_q = (j < GDN_HEADS).astype(F32)
    normed = (j < 2 * GDN_HEADS).astype(F32)
    scale = is_q * (GDN_DK ** -0.5) + (1.0 - is_q)
    o_ref[0] = y * (rs * (scale * normed) + (1.0 - normed))


def _gdn_prep(proj3, conv_w):
    batch, seq, _ = proj3.shape
    nblk = 3 * GDN_HEADS
    return pl.pallas_call(
        _gdn_prep_kernel,
        grid=(batch, nblk),
        in_specs=[pl.BlockSpec((1, seq, LANES), lambda b, j: (b, 0, j)),
                  pl.BlockSpec((3, LANES), lambda b, j: (0, j))],
        out_specs=pl.BlockSpec((1, seq, LANES), lambda b, j: (b, 0, j)),
        out_shape=jax.ShapeDtypeStruct((batch, seq, nblk * LANES), F32),
        compiler_params=_params("parallel", "parallel"),
        name="gdn_conv",
    )(proj3, conv_w)


def _gdn_gate_kernel(p_ref, al_ref, dtb_ref, o_ref, t_ref):
    ab = p_ref[...]
    lane = lax.broadcasted_iota(jnp.int32, ab.shape, 1)
    is_g = (lane < 4 * GDN_HEADS) & ((lane & GDN_HEADS) == 0)
    is_fwd = lane < 2 * GDN_HEADS
    x = ab + dtb_ref[...]
    softplus = jnp.maximum(x, 0.0) + jnp.log(1.0 + jnp.exp(-jnp.abs(x)))
    g = jnp.where(is_g, -jnp.exp(al_ref[...]) * softplus, 0.0)
    beta = 1.0 / (1.0 + jnp.exp(-ab))
    ri = lax.broadcasted_iota(jnp.int32, (GDN_SUPER, GDN_SUPER), 0)
    ci = lax.broadcasted_iota(jnp.int32, (GDN_SUPER, GDN_SUPER), 1)
    same = (ri // GDN_CHUNK) == (ci // GDN_CHUNK)
    lower = jnp.where(same & (ri >= ci), 1.0, 0.0).astype(BF16)
    upper = jnp.where(same & (ri <= ci), 1.0, 0.0).astype(BF16)
    parts = _split_bf16(g, 3)
    pre = _dot(lower, parts[0]) + _dot(lower, parts[1]) + _dot(lower, parts[2])
    suf = _dot(upper, parts[0]) + _dot(upper, parts[1]) + _dot(upper, parts[2])
    o_ref[...] = jnp.where(is_g, jnp.where(is_fwd, pre, suf), beta)
    t_ref[...] = pre + suf - g


def _gdn_gates(proj, a_log, dt_bias):
    m, width = proj.shape
    col_blk = (width - LANES) // LANES
    pad = jnp.zeros((2, GDN_HEADS), F32)

    def row(v):
        return jnp.pad(jnp.stack([v.astype(F32), pad], axis=1).reshape(1, 4 * GDN_HEADS),
                       ((0, 0), (0, LANES - 4 * GDN_HEADS)))

    spec = pl.BlockSpec((GDN_SUPER, LANES), lambda i: (i, 0))
    vec = pl.BlockSpec((1, LANES), lambda i: (0, 0))
    return pl.pallas_call(
        _gdn_gate_kernel,
        grid=(m // GDN_SUPER,),
        in_specs=[pl.BlockSpec((GDN_SUPER, LANES), lambda i: (i, col_blk)), vec, vec],
        out_specs=[spec, spec],
        out_shape=[jax.ShapeDtypeStruct((m, LANES), F32), jax.ShapeDtypeStruct((m, LANES), F32)],
        compiler_params=_params("parallel"),
        name="gdn_gates",
    )(proj, row(a_log), row(dt_bias))


def _gdn_kernel(q_ref, k_ref, v_ref, z_ref, col_ref, row_ref, s0_ref, ng_ref, o_ref, sf_ref, of_s, ob_s, *, nsc, hps):
    n = GDN_SUPER
    ri = lax.broadcasted_iota(jnp.int32, (n, n), 0)
    ci = lax.broadcasted_iota(jnp.int32, (n, n), 1)
    b64 = (ri >> 6) == (ci >> 6)
    b32 = (ri >> 5) == (ci >> 5)
    b16 = (ri >> 4) == (ci >> 4)
    eye = jnp.where(ri == ci, 1.0, 0.0)
    causal = (b64 & (ri >= ci), b64 & (ri <= ci))
    strict = (b64 & (ri > ci), b64 & (ri < ci))
    nchunk = n // GDN_CHUNK

    def mm(a, b):
        return _dot(a.astype(BF16), b.astype(BF16))

    def lockstep(chains, states):
        cs = range(len(chains))
        dirs = [d for (_, d, _) in chains]
        lanes = [slice(hh * LANES, (hh + 1) * LANES) for (hh, _, _) in chains]
        q = [q_ref[0, sc, :, lanes[c]] for c, (_, _, sc) in enumerate(chains)]
        k = [k_ref[0, sc, :, lanes[c]] for c, (_, _, sc) in enumerate(chains)]
        v = [v_ref[0, sc, :, lanes[c]] for c, (_, _, sc) in enumerate(chains)]
        col = [col_ref[hh, sc] for (hh, _, sc) in chains]
        row = [row_ref[hh, sc] for (hh, _, sc) in chains]
        gcol = [col[c][:, 3 * dirs[c]:3 * dirs[c] + 1] for c in cs]
        bcol = [col[c][:, 3 * dirs[c] + 1:3 * dirs[c] + 2] for c in cs]
        tcol = [col[c][:, 3 * dirs[c] + 2:3 * dirs[c] + 3] for c in cs]
        grow = [row[c][3 * dirs[c]:3 * dirs[c] + 1, :] for c in cs]
        trow = [row[c][3 * dirs[c] + 2:3 * dirs[c] + 3, :] for c in cs]
        decay = [jnp.exp(jnp.where(causal[dirs[c]], gcol[c] - grow[c], -jnp.inf)) for c in cs]
        k16 = [k[c].astype(BF16) for c in cs]
        kb = [k[c] * bcol[c] for c in cs]
        kk = [_dot_nt(kb[c].astype(BF16), k16[c]) for c in cs]
        qk = [_dot_nt(q[c].astype(BF16), k16[c]) for c in cs]
        lmat = [jnp.where(strict[dirs[c]], kk[c] * decay[c], 0.0) for c in cs]
        intra = [(qk[c] * decay[c]).astype(BF16) for c in cs]
        l0 = [jnp.where(b16, lmat[c], 0.0) for c in cs]
        inv = [eye - l0[c] for c in cs]
        pw = [mm(l0[c], l0[c]) for c in cs]
        for rep in range(3):
            inv = [inv[c] + mm(inv[c], pw[c]) for c in cs]
            if rep < 2:
                pw = [mm(pw[c], pw[c]) for c in cs]
        for keep in (b32 & (~b16), ~b32):
            cm = [jnp.where(keep, lmat[c], 0.0) for c in cs]
            t = [mm(inv[c], cm[c]) for c in cs]
            inv = [inv[c] - mm(t[c], inv[c]) for c in cs]
        eg = [jnp.exp(gcol[c]) for c in cs]
        uw = [mm(inv[c], jnp.concatenate([v[c] * bcol[c], kb[c] * eg[c]], axis=1)) for c in cs]
        u = [uw[c][:, :LANES] for c in cs]
        w = [uw[c][:, LANES:].astype(BF16) for c in cs]
        qd = [(q[c] * eg[c]).astype(BF16) for c in cs]
        kdt = [jnp.transpose(k[c] * jnp.exp(tcol[c] - gcol[c])).astype(BF16) for c in cs]
        outs = [[None] * nchunk for _ in cs]
        states = list(states)
        for step in range(nchunk):
            js = [step if dirs[c] == 0 else nchunk - 1 - step for c in cs]
            rs = [slice(j * GDN_CHUNK, (j + 1) * GDN_CHUNK) for j in js]
            s16 = [states[c].astype(BF16) for c in cs]
            vn16 = [(u[c][rs[c]] - _dot(w[c][rs[c]], s16[c])).astype(BF16) for c in cs]
            qs = [_dot(qd[c][rs[c]], s16[c]) for c in cs]
            for c in cs:
                outs[c][js[c]] = qs[c] + _dot(intra[c][rs[c], rs[c]], vn16[c])
            states = [states[c] * jnp.exp(trow[c][:, js[c] * GDN_CHUNK:js[c] * GDN_CHUNK + 1])
                      + _dot(kdt[c][:, rs[c]], vn16[c]) for c in cs]
        return [jnp.concatenate(outs[c], axis=0) for c in cs], states

    def body(i, carry):
        chains = [(hh, d, i if d == 0 else nsc - 1 - i) for hh in range(hps) for d in range(2)]
        outs, states = lockstep(chains, carry)
        for c, (hh, d, sc) in enumerate(chains):
            if d == 0:
                of_s[hh, sc] = outs[c]
            else:
                ob_s[hh, sc] = outs[c]
        return tuple(states)

    init = tuple(s0_ref[0, d, hh] for hh in range(hps) for d in range(2))
    fin = body(0, init) if nsc == 1 else lax.fori_loop(0, nsc, body, init)
    for hh in range(hps):
        sf_ref[0, 0, hh] = fin[2 * hh]
        sf_ref[0, 1, hh] = fin[2 * hh + 1]
        lanes = slice(hh * LANES, (hh + 1) * LANES)
        for sc in range(nsc):
            o = _rms(of_s[hh, sc] + ob_s[hh, sc], ng_ref[...])
            o_ref[0, sc, :, lanes] = (o * _silu(z_ref[0, sc, :, lanes])).astype(BF16)


def _gdn_scan(qkv, proj, cols, rows, s0, norm_g, batch, seq):
    nsc = seq // GDN_SUPER
    hps = 4 if nsc == 1 else 2
    h = GDN_HEADS
    qkv4 = qkv.reshape(batch, nsc, GDN_SUPER, 3 * h * LANES)
    proj4 = proj.reshape(batch, nsc, GDN_SUPER, proj.shape[-1])

    def tile(off):
        return pl.BlockSpec((1, nsc, GDN_SUPER, hps * LANES), lambda b, j: (b, 0, 0, off // hps + j))

    st = pl.BlockSpec((1, 2, hps, GDN_DK, LANES), lambda b, j: (b, 0, j, 0, 0))
    return pl.pallas_call(
        functools.partial(_gdn_kernel, nsc=nsc, hps=hps),
        grid=(batch, h // hps),
        in_specs=[tile(0), tile(h), tile(2 * h), tile(3 * h),
                  pl.BlockSpec((hps, nsc, GDN_SUPER, 8), lambda b, j: (j, b, 0, 0)),
                  pl.BlockSpec((hps, nsc, 8, GDN_SUPER), lambda b, j: (j, b, 0, 0)),
                  st, pl.BlockSpec((1, LANES), lambda b, j: (0, 0))],
        out_specs=[tile(0), st],
        out_shape=[jax.ShapeDtypeStruct((batch, nsc, GDN_SUPER, h * LANES), BF16),
                   jax.ShapeDtypeStruct((batch, 2, h, GDN_DK, LANES), F32)],
        scratch_shapes=[pltpu.VMEM((hps, nsc, GDN_SUPER, LANES), F32),
                        pltpu.VMEM((hps, nsc, GDN_SUPER, LANES), F32)],
        compiler_params=_params("parallel", "parallel"),
        name="gdn_scan",
    )(qkv4, qkv4, qkv4, proj4, cols, rows, s0, norm_g.reshape(1, LANES))


def _gdn_mixer(x, g, mod, rows_per_mod, batch, seq, s0, w_cat, conv_w, a_log, dt_bias, norm_g):
    proj = _nmm(x, g, mod, w_cat, rows_per_mod, 0, 1)
    qkv = _gdn_prep(proj.reshape(batch, seq, proj.shape[-1]), conv_w)
    gates, tot = _gdn_gates(proj, a_log, dt_bias)
    h = GDN_HEADS
    n = batch * seq
    per = jnp.stack([gates[:, 0:h], gates[:, h:2 * h], tot[:, 0:h],
                     gates[:, 2 * h:3 * h], gates[:, 3 * h:4 * h], tot[:, 2 * h:3 * h],
                     jnp.zeros((n, h), F32), jnp.zeros((n, h), F32)], axis=-1)
    per = per.reshape(n // GDN_SUPER, GDN_SUPER, h, 8)
    cols = jnp.transpose(per, (2, 0, 1, 3))
    rows = jnp.transpose(per, (2, 0, 3, 1))
    o, s_fin = _gdn_scan(qkv, proj, cols, rows, s0, norm_g, batch, seq)
    return o.reshape(n, D_MODEL), s_fin


SUBLANES = 8


def _store_token_tiles(ref, x):
    rows = x.shape[0]
    for j in range(SUBLANES):
        ref[pl.ds(j, rows, stride=SUBLANES), :] = x[:, j * LANES:(j + 1) * LANES]


def _load_token_tiles(ref, rows):
    return jnp.concatenate([ref[pl.ds(j, rows, stride=SUBLANES), :] for j in range(SUBLANES)], axis=1)


def _moe_pre_kernel(x_ref, g_ref, mod_ref, wr_ref, h_ref, aff_ref):
    h = _norm_mod(x_ref[...], g_ref[...], mod_ref, 3, 4)
    _store_token_tiles(h_ref, h)
    h1, h2 = _split_bf16(h, 2)
    logits = _dot_nt(wr_ref[0], h1) + (_dot_nt(wr_ref[1], h1) + _dot_nt(wr_ref[0], h2))
    e = jnp.exp(logits - jnp.max(logits, axis=0, keepdims=True))
    aff_ref[...] = e / jnp.sum(e, axis=0, keepdims=True)


def _moe_pre(x, g, mod, rows_per_mod, w_router, tm=512):
    m = x.shape[0]
    w_router = w_router.T
    w1 = w_router.astype(BF16)
    w2 = (w_router - w1.astype(F32)).astype(BF16)
    return pl.pallas_call(
        _moe_pre_kernel,
        grid=(m // tm,),
        in_specs=[pl.BlockSpec((tm, D_MODEL), lambda i: (i, 0)),
                  pl.BlockSpec((1, D_MODEL), lambda i: (0, 0)),
                  pl.BlockSpec((1, 6, D_MODEL), lambda i: ((i * tm) // rows_per_mod, 0, 0)),
                  pl.BlockSpec((2, N_EXPERTS, D_MODEL), lambda i: (0, 0, 0))],
        out_specs=[pl.BlockSpec((tm * SUBLANES, LANES), lambda i: (i, 0)),
                   pl.BlockSpec((N_EXPERTS, tm), lambda i: (0, i))],
        out_shape=[jax.ShapeDtypeStruct((m * SUBLANES, LANES), F32), jax.ShapeDtypeStruct((N_EXPERTS, m), F32)],
        compiler_params=_params("parallel"),
        name="moe_router",
    )(x, g.reshape(1, D_MODEL), mod, jnp.stack([w1, w2]))


def _route_kernel(aff_ref, gat_ref, sct_ref, gsel_ref, cnt_ref, pos_ref, mask_s, rank_s, *, cap):
    e = pl.program_id(1)
    nb = aff_ref.shape[1]

    def count(m):
        return jnp.sum(jnp.sum(jnp.where(m, 1.0, 0.0), axis=1, keepdims=True), axis=0, keepdims=True)

    li = lax.broadcasted_iota(jnp.int32, (LANES, LANES), 0)
    lj = lax.broadcasted_iota(jnp.int32, (LANES, LANES), 1)
    upper = jnp.where(li <= lj, 1.0, 0.0).astype(BF16)
    bi = lax.broadcasted_iota(jnp.int32, (nb, nb), 0)
    bj = lax.broadcasted_iota(jnp.int32, (nb, nb), 1)
    before = jnp.where(bj < bi, 1.0, 0.0).astype(BF16)

    def prefix(m):
        inside = _dot(m.astype(BF16), upper)
        total = jnp.broadcast_to(inside[:, LANES - 1:LANES], inside.shape)
        return inside, _dot(before, total.astype(BF16))

    def split64(x):
        hi = jnp.floor(x * (1.0 / 64.0))
        return hi, x - 64.0 * hi

    @pl.when(pl.program_id(0) == 0)
    def _():
        bits = pltpu.bitcast(aff_ref[0], jnp.int32)

        def bisect(_, lohi):
            lo, hi = lohi
            mid = lo + ((hi - lo) >> 1)
            ok = count(bits >= mid) >= cap
            return jnp.where(ok, mid, lo), jnp.where(ok, hi, mid)

        lo, _ = lax.fori_loop(0, 31, bisect,
                              (jnp.zeros((1, 1), jnp.int32), jnp.full((1, 1), 0x7F800000, jnp.int32)))
        above = bits > lo
        tied = bits == lo
        need = cap - count(above)
        tin, tbase = prefix(jnp.where(tied, 1.0, 0.0))
        mask = jnp.where(above | (tied & (tin + tbase <= need)), 1.0, 0.0)
        mask_s[e] = mask

        @pl.when(e == 0)
        def _():
            rank_s[...] = jnp.zeros_like(rank_s)

        rank_s[...] += mask

    @pl.when(pl.program_id(0) == 1)
    def _():
        @pl.when(e == 0)
        def _():
            cnt = rank_s[...]
            inside = _dot(cnt.astype(BF16), upper)
            hi, lo = split64(jnp.broadcast_to(inside[:, LANES - 1:LANES], inside.shape))
            earlier = 64.0 * _dot(before, hi.astype(BF16)) + _dot(before, lo.astype(BF16))
            cnt_ref[...] = cnt.astype(jnp.int32)
            pos_ref[...] = (inside + earlier - cnt).astype(jnp.int32)
            rank_s[...] = jnp.zeros_like(rank_s)

        a = aff_ref[0]
        mask = mask_s[e]
        rank = rank_s[...]
        pos = pos_ref[...].astype(F32)
        inside, base = prefix(mask)
        block_end = jnp.transpose(base + jnp.broadcast_to(inside[:, LANES - 1:LANES], base.shape))[0:1, :]
        slot = lax.broadcasted_iota(jnp.int32, (cap, 1), 0).astype(F32)
        blk = jnp.sum(jnp.where(block_end <= slot, 1.0, 0.0), axis=1, keepdims=True)
        pick = jnp.where(lax.broadcasted_iota(jnp.int32, (cap, nb), 1).astype(F32) == blk, 1.0, 0.0).astype(BF16)
        base_hi, base_lo = split64(base)
        pos_hi = jnp.floor(pos * (1.0 / 4096.0))
        pos_mid, pos_lo = split64(pos - 4096.0 * pos_hi)
        a1, a2, a3 = _split_bf16(a, 3)
        table = jnp.concatenate([inside.astype(BF16), base_hi.astype(BF16), base_lo.astype(BF16), rank.astype(BF16),
                                 a1, a2, a3, pos_hi.astype(BF16), pos_mid.astype(BF16), pos_lo.astype(BF16)],
                                axis=1)
        got = _dot(pick, table)

        def part(j):
            return got[:, j * LANES:(j + 1) * LANES]

        target = slot - (64.0 * part(1)[:, 0:1] + part(2)[:, 0:1])
        lane = jnp.sum(jnp.where(part(0) <= target, 1.0, 0.0), axis=1, keepdims=True)
        here = lax.broadcasted_iota(jnp.int32, (cap, LANES), 1).astype(F32) == lane

        def at_token(x):
            return jnp.sum(jnp.where(here, x, 0.0), axis=1, keepdims=True)

        dest = at_token(part(3)) + at_token(4096.0 * part(7) + 64.0 * part(8) + part(9))
        gat_ref[0] = ((LANES * blk + lane) * SUBLANES).astype(jnp.int32)
        sct_ref[0] = (dest * SUBLANES).astype(jnp.int32)
        gsel_ref[0] = at_token((part(4) + part(5)) + part(6))
        rank_s[...] = rank + mask


def _route(aff3, cap):
    n_exp, nb, _ = aff3.shape
    col = pl.BlockSpec((1, cap, 1), lambda p, e: (p * e, 0, 0))
    tok = pl.BlockSpec((nb, LANES), lambda p, e: (0, 0))
    return pl.pallas_call(
        functools.partial(_route_kernel, cap=cap),
        grid=(2, n_exp),
        in_specs=[pl.BlockSpec((1, nb, LANES), lambda p, e: (e, 0, 0))],
        out_specs=[col, col, col, tok, tok],
        out_shape=[jax.ShapeDtypeStruct((n_exp, cap, 1), jnp.int32), jax.ShapeDtypeStruct((n_exp, cap, 1), jnp.int32),
                   jax.ShapeDtypeStruct((n_exp, cap, 1), F32),
                   jax.ShapeDtypeStruct((nb, LANES), jnp.int32), jax.ShapeDtypeStruct((nb, LANES), jnp.int32)],
        scratch_shapes=[pltpu.VMEM((n_exp, nb, LANES), F32), pltpu.VMEM((nb, LANES), F32)],
        compiler_params=_params("arbitrary", "arbitrary"),
        name="moe_route",
    )(aff3)


def _ffn_kernel(g0_ref, gat_ref, sct_ref, gate_ref, wg_ref, wu_ref, wd_ref, h_hbm, z_hbm,
                xb0, xb1, yb0, yb1, wg_s, wu_s, wd_s, g_s, s_s, gsem, ssem, psem, *, tc, pairs_per_e, npairs):
    m = pl.program_id(0)
    hb = (m % 2) * (3 * tc)
    nb = 3 * tc - hb
    last = npairs - 1

    def tile_in(off, buf, i, sem):
        return pltpu.make_async_copy(h_hbm.at[pl.ds(pl.multiple_of(off, SUBLANES), SUBLANES)],
                                     buf.at[pl.ds(i * SUBLANES, SUBLANES)], sem)

    def tile_out(off, buf, i, sem):
        return pltpu.make_async_copy(buf.at[pl.ds(i * SUBLANES, SUBLANES)],
                                     z_hbm.at[pl.ds(pl.multiple_of(off, SUBLANES), SUBLANES)], sem)

    def tables(row, base):
        return (pltpu.make_async_copy(gat_ref.at[row], g_s.at[pl.ds(base, 2 * tc)], psem.at[0]),
                pltpu.make_async_copy(sct_ref.at[row], s_s.at[pl.ds(base, 3 * tc)], psem.at[1]))

    def wait_tiles(copy):
        for _ in range(tc):
            copy.wait()

    @pl.when(m == 0)
    def _():
        yb1[...] = jnp.zeros_like(yb1)
        first = pltpu.make_async_copy(g0_ref.at[0], g_s.at[pl.ds(0, tc)], psem.at[0])
        first.start()
        first.wait()
        for i in range(tc):
            tile_in(g_s[i], xb0, i, gsem.at[0]).start()
        for cp in tables(0, 0):
            cp.start()

    for cp in tables(m, hb):
        cp.wait()
    for cp in tables(jnp.minimum(m + 1, last), nb):
        cp.start()

    @pl.when(m % pairs_per_e == 0)
    def _():
        wg_s[...] = wg_ref[0, 0].astype(BF16)
        wu_s[...] = wu_ref[0, 0].astype(BF16)
        wd_s[...] = wd_ref[0, 0].astype(BF16)

    def ffn(xb, gate):
        x = _load_token_tiles(xb, tc).astype(BF16)
        half = wg_s.shape[1] // 2
        acc = None
        for c0 in (0, half):
            a = _dot(x, wg_s[:, c0:c0 + half])
            b = _dot(x, wu_s[:, c0:c0 + half])
            part = _dot((_silu(a) * b).astype(BF16), wd_s[c0:c0 + half, :])
            acc = part if acc is None else acc + part
        return acc * gate

    wait_tiles(tile_in(0, xb0, 0, gsem.at[0]))
    for i in range(tc):
        tile_in(g_s[hb + i], xb1, i, gsem.at[1]).start()
        tile_out(s_s[hb + i], yb1, i, ssem.at[1]).start()
    y = ffn(xb0, gate_ref[0, 0:tc])

    @pl.when(m > 0)
    def _():
        wait_tiles(tile_out(0, yb0, 0, ssem.at[0]))

    _store_token_tiles(yb0, y)

    wait_tiles(tile_in(0, xb1, 0, gsem.at[1]))
    for i in range(tc):
        tile_in(g_s[hb + tc + i], xb0, i, gsem.at[0]).start()
        tile_out(s_s[hb + tc + i], yb0, i, ssem.at[0]).start()
    y = ffn(xb1, gate_ref[0, tc:2 * tc])
    wait_tiles(tile_out(0, yb1, 0, ssem.at[1]))
    _store_token_tiles(yb1, y)

    @pl.when(m == last)
    def _():
        for i in range(tc):
            tile_out(s_s[hb + 2 * tc + i], yb1, i, ssem.at[1]).start()
        wait_tiles(tile_out(0, yb1, 0, ssem.at[1]))
        wait_tiles(tile_out(0, yb0, 0, ssem.at[0]))
        wait_tiles(tile_in(0, xb0, 0, gsem.at[0]))
        for cp in tables(last, nb):
            cp.wait()


def _expert_ffn(h_tiles, gat, sct, gate, w_gate, w_up, w_down, layer, z_rows, tc=512):
    n_exp, cap = gat.shape
    ff = w_gate.shape[-1]
    tc = min(tc, cap // 2)
    pairs_per_e = cap // (2 * tc)
    npairs = n_exp * pairs_per_e
    nsteps = 2 * npairs
    gat = gat.reshape(nsteps, tc)
    sct = sct.reshape(nsteps, tc)
    spare = ((z_rows - tc + jnp.arange(tc, dtype=jnp.int32)) * SUBLANES)[None]
    gat_next = jnp.concatenate([gat[1:], gat[:1]], axis=0).reshape(npairs, 2 * tc)
    sct_prev = jnp.concatenate([spare, sct[:-1]], axis=0).reshape(npairs, 2 * tc)
    sct3 = jnp.concatenate([sct_prev, sct[1::2]], axis=1)
    tile_buf = pltpu.VMEM((tc * SUBLANES, LANES), F32)
    wspec_in = pl.BlockSpec((1, 1, D_MODEL, ff), lambda m: (layer, m // pairs_per_e, 0, 0))
    return pl.pallas_call(
        functools.partial(_ffn_kernel, tc=tc, pairs_per_e=pairs_per_e, npairs=npairs),
        grid=(npairs,),
        in_specs=[pl.BlockSpec((1, tc), lambda m: (0, 0)),
                  pl.BlockSpec((npairs, 2 * tc), lambda m: (0, 0)),
                  pl.BlockSpec((npairs, 3 * tc), lambda m: (0, 0)),
                  pl.BlockSpec((1, 2 * tc, 1), lambda m: (m, 0, 0)),
                  wspec_in, wspec_in,
                  pl.BlockSpec((1, 1, ff, D_MODEL), lambda m: (layer, m // pairs_per_e, 0, 0)),
                  pl.BlockSpec(memory_space=pl.ANY)],
        out_specs=pl.BlockSpec(memory_space=pl.ANY),
        out_shape=jax.ShapeDtypeStruct((z_rows * SUBLANES, LANES), F32),
        scratch_shapes=[tile_buf, tile_buf, tile_buf, tile_buf,
                        pltpu.VMEM((D_MODEL, ff), BF16), pltpu.VMEM((D_MODEL, ff), BF16),
                        pltpu.VMEM((ff, D_MODEL), BF16),
                        pltpu.SMEM((6 * tc,), jnp.int32), pltpu.SMEM((6 * tc,), jnp.int32),
                        pltpu.SemaphoreType.DMA((2,)), pltpu.SemaphoreType.DMA((2,)),
                        pltpu.SemaphoreType.DMA((2,))],
        compiler_params=_params("arbitrary"),
        name="expert_ffn",
    )(gat[:1], gat_next, sct3, gate.reshape(npairs, 2 * tc, 1), w_gate, w_up, w_down, h_tiles)


def _combine_kernel(st_ref, x_ref, cnt_ref, pos_ref, mod_ref, fg_ref, y_hbm, o_ref, ybuf, acc, sem,
                    *, gi, final, tt, rc):
    i = pl.program_id(0)
    first = st_ref[i]
    nchunk = (st_ref[i + 1] - first + (rc - 1)) // rc

    def chunk(c, sl):
        row0 = pl.multiple_of((first + c * rc) * SUBLANES, SUBLANES)
        return pltpu.make_async_copy(y_hbm.at[pl.ds(row0, rc * SUBLANES)], ybuf.at[sl], sem.at[sl])

    @pl.when(nchunk > 0)
    def _():
        chunk(0, 0).start()

    acc[...] = jnp.zeros_like(acc)
    begin = pos_ref[...] - first
    end = begin + cnt_ref[...]
    row = lax.broadcasted_iota(jnp.int32, (tt, rc), 1)

    def body(c, carry):
        sl = c % 2

        @pl.when(c + 1 < nchunk)
        def _():
            chunk(c + 1, 1 - sl).start()

        chunk(c, sl).wait()
        r = row + c * rc
        seg = jnp.where((r >= begin) & (r < end), 1.0, 0.0).astype(BF16)
        hi, lo = _split_bf16(_load_token_tiles(ybuf.at[sl], rc), 2)
        acc[...] += _dot(seg, hi) + _dot(seg, lo)
        return carry

    lax.fori_loop(0, nchunk, body, 0)
    x = x_ref[...] + mod_ref[0, gi:gi + 1, :] * acc[...]
    if final:
        x = _rms(x, fg_ref[...])
    o_ref[...] = x


def _combine(x, y, cnt, pos, mod, rows_per_mod, gi, final_g, final, tt, rc):
    m = x.shape[0]
    starts = jnp.concatenate([pos[::tt], (pos[-1:] + cnt[-1:])])
    col = pl.BlockSpec((tt, 1), lambda i, st: (i, 0))
    return pl.pallas_call(
        functools.partial(_combine_kernel, gi=gi, final=final, tt=tt, rc=rc),
        grid_spec=pltpu.PrefetchScalarGridSpec(
            num_scalar_prefetch=1,
            grid=(m // tt,),
            in_specs=[pl.BlockSpec((tt, D_MODEL), lambda i, st: (i, 0)), col, col,
                      pl.BlockSpec((1, 6, D_MODEL), lambda i, st: ((i * tt) // rows_per_mod, 0, 0)),
                      pl.BlockSpec((1, D_MODEL), lambda i, st: (0, 0)),
                      pl.BlockSpec(memory_space=pl.ANY)],
            out_specs=pl.BlockSpec((tt, D_MODEL), lambda i, st: (i, 0)),
            scratch_shapes=[pltpu.VMEM((2, rc * SUBLANES, LANES), F32), pltpu.VMEM((tt, D_MODEL), F32),
                            pltpu.SemaphoreType.DMA((2,))]),
        out_shape=jax.ShapeDtypeStruct((m, D_MODEL), F32),
        compiler_params=_params("arbitrary"),
        name="moe_combine",
    )(starts, x, cnt.reshape(m, 1), pos.reshape(m, 1), mod, final_g.reshape(1, D_MODEL), y)


def _expert_choice(x, g, mod, rows_per_mod, w_router, w_gate, w_up, w_down, layer, gi, final_g, final):
    n = x.shape[0]
    cap = EC_CAPACITY_FACTOR * n // N_EXPERTS
    tt = min(256, n)
    h, aff = _moe_pre(x, g, mod, rows_per_mod, w_router)
    gat, sct, gate, cnt, pos = _route(aff.reshape(N_EXPERTS, n // LANES, LANES), cap)
    gat, sct, gate = (a.reshape(N_EXPERTS, cap) for a in (gat, sct, gate))
    tc = min(512, cap // 2)
    y = _expert_ffn(h, gat, sct, gate, w_gate, w_up, w_down, layer, N_EXPERTS * cap + tc, tc)
    return _combine(x, y, cnt.reshape(n), pos.reshape(n), mod, rows_per_mod, gi, final_g, final, tt, min(256, tc))


def _trunk(x, mods, rows_per_mod, batch, seq, caches, p):
    depth = p["mod_w"].shape[0]
    n = batch * seq
    new = {}
    for layer in range(depth):
        kind, slot = layer % 4, layer // 4
        mod = mods[layer]
        g_mix = p["norm_mix_g"][layer]
        if kind == 0:
            w_ab = jnp.pad(p["gdn_w_ab"][slot], ((0, 0), (0, LANES - 4 * GDN_HEADS)))
            w_cat = jnp.concatenate([p["gdn_w_in"][slot], w_ab], axis=1).astype(BF16)
            if caches is None:
                s0 = jnp.zeros((batch, 2, GDN_HEADS, GDN_DK, LANES), F32)
            else:
                s0 = caches["state_gdn"][:, slot]
            o, s_fin = _gdn_mixer(x, g_mix, mod, rows_per_mod, batch, seq, s0, w_cat, p["gdn_conv_w"][slot],
                                  p["gdn_a_log"][slot], p["gdn_dt_bias"][slot], p["gdn_norm_g"][slot])
            new.setdefault("gdn", []).append(s_fin)
            w_out = p["gdn_w_out"][slot]
        elif kind == 1:
            proj = _nmm(x, g_mix, mod, p["diff_w_in"][slot].astype(BF16), rows_per_mod, 0, 1)
            nk = seq
            if caches is None:
                assert DIFF_HEADS == SUBLANES
                q, k, v, kc, vc = _attn_prep(proj, "diff", seq, False)
                new.setdefault("diff_k", []).append(kc.reshape(batch, seq, DIFF_HEADS, LANES))
                new.setdefault("diff_v", []).append(vc.reshape(batch, seq, DIFF_HEADS, LANES))
            else:
                q, k, v = _attn_prep(proj, "diff", seq, True)
                kc = caches["cache_diff_k"][:, slot].reshape(batch, -1, DIFF_HEADS * LANES).astype(BF16)
                vc = caches["cache_diff_v"][:, slot].reshape(batch, -1, DIFF_HEADS * LANES).astype(BF16)
                nk = seq + kc.shape[1]
                k = jnp.concatenate([k.reshape(batch, seq, -1), kc], axis=1).reshape(batch * nk, -1)
                v = jnp.concatenate([v.reshape(batch, seq, -1), vc], axis=1).reshape(batch * nk, -1)
            lam_init = 0.8 - 0.6 * math.exp(-0.3 * layer)
            o = _attention(q, k, v, "diff", batch, seq, nk,
                           (p["diff_lambda"][slot], p["diff_subln_g"][slot]), lam_init)
            w_out = p["diff_w_out"][slot]
        elif kind == 2:
            proj = _nmm(x, g_mix, mod, p["gqa_w_in"][slot].astype(BF16), rows_per_mod, 0, 1)
            q, k, v, kf = _attn_prep(proj, "gqa", seq, caches is not None,
                                     (p["gqa_q_norm_g"][slot], p["gqa_k_norm_g"][slot]))
            nk = seq
            if caches is None:
                v0 = (GQA_HEADS + GQA_KV) * LANES
                new.setdefault("gqa_k", []).append(kf.reshape(batch, seq, GQA_KV, GQA_HD))
                new.setdefault("gqa_v", []).append(proj[:, v0:].reshape(batch, seq, GQA_KV, GQA_HD))
            else:
                kc = caches["cache_gqa_k"][:, slot].reshape(batch, -1, GQA_KV * GQA_HD).astype(BF16)
                vc = caches["cache_gqa_v"][:, slot].reshape(batch, -1, GQA_KV * GQA_HD).astype(BF16)
                nk = seq + kc.shape[1]
                k = jnp.concatenate([k.reshape(batch, seq, -1), kc], axis=1).reshape(batch * nk, -1)
                v = jnp.concatenate([v.reshape(batch, seq, -1), vc], axis=1).reshape(batch * nk, -1)
            o = _attention(q, k, v, "gqa", batch, seq, nk)
            w_out = p["gqa_w_out"][slot]
        else:
            proj = _nmm(x, g_mix, mod, p["swa_w_in"][slot].astype(BF16), rows_per_mod, 0, 1)
            q, k, v = _attn_prep(proj, "swa", seq, caches is not None)
            sinks = p["swa_sinks"][slot].astype(F32)
            if caches is None:
                k0 = SWA_HEADS * SWA_HD
                v0 = k0 + SWA_KV * SWA_HD
                new.setdefault("swa_k", []).append(proj[:, k0:v0].reshape(batch, seq, SWA_KV, SWA_HD))
                new.setdefault("swa_v", []).append(proj[:, v0:].reshape(batch, seq, SWA_KV, SWA_HD))
                o = _attention(q, k, v, "swa", batch, seq, seq, (sinks,))
            else:
                kc = jnp.pad(caches["cache_swa_k"][:, slot], ((0, 0), (0, 0), (0, 0), (0, LANES - SWA_HD)))
                past = kc.shape[1]
                kc = kc.reshape(batch, past, SWA_KV * LANES).astype(BF16)
                vc = caches["cache_swa_v"][:, slot].reshape(batch, past, SWA_KV * SWA_HD).astype(BF16)
                nk = seq + past
                k = jnp.concatenate([k.reshape(batch, seq, -1), kc], axis=1).reshape(batch * nk, -1)
                v = jnp.concatenate([v.reshape(batch, seq, -1), vc], axis=1).reshape(batch * nk, -1)
                o = _band_attention(q, k, v, sinks, batch, seq, past)
            w_out = p["swa_w_out"][slot]
        x = _mm_resid(o, w_out.astype(BF16), x, mod, rows_per_mod, 2)
        x = _expert_choice(x, p["norm_ffn_g"][layer], mod, rows_per_mod, p["moe_router"][layer],
                           p["moe_w_gate"], p["moe_w_up"], p["moe_w_down"], layer,
                           5, p["final_g"], layer == depth - 1)
    return x, new


def kernel(x_prompt, x_sample, state_gdn, cache_diff_k, cache_diff_v, cache_gqa_k, cache_gqa_v, cache_swa_k, cache_swa_v, c, c_ctx, mod_w, mod_b, norm_mix_g, norm_ffn_g, final_g, gdn_w_in, gdn_conv_w, gdn_w_ab, gdn_a_log, gdn_dt_bias, gdn_norm_g, gdn_w_out, diff_w_in, diff_lambda, diff_subln_g, diff_w_out, gqa_w_in, gqa_q_norm_g, gqa_k_norm_g, gqa_w_out, swa_w_in, swa_sinks, swa_w_out, moe_router, moe_w_gate, moe_w_up, moe_w_down):
    p = dict(mod_w=mod_w, mod_b=mod_b, norm_mix_g=norm_mix_g, norm_ffn_g=norm_ffn_g, final_g=final_g,
             gdn_w_in=gdn_w_in, gdn_conv_w=gdn_conv_w, gdn_w_ab=gdn_w_ab, gdn_a_log=gdn_a_log,
             gdn_dt_bias=gdn_dt_bias, gdn_norm_g=gdn_norm_g, gdn_w_out=gdn_w_out,
             diff_w_in=diff_w_in, diff_lambda=diff_lambda, diff_subln_g=diff_subln_g, diff_w_out=diff_w_out,
             gqa_w_in=gqa_w_in, gqa_q_norm_g=gqa_q_norm_g, gqa_k_norm_g=gqa_k_norm_g, gqa_w_out=gqa_w_out,
             swa_w_in=swa_w_in, swa_sinks=swa_sinks, swa_w_out=swa_w_out,
             moe_router=moe_router, moe_w_gate=moe_w_gate, moe_w_up=moe_w_up, moe_w_down=moe_w_down)
    batch, seq, _ = x_prompt.shape
    dec_batch, dec_seq, _ = x_sample.shape
    depth = mod_w.shape[0]
    rows = 8 * ((1 + dec_batch + 7) // 8)
    cond = jnp.zeros((rows, D_MODEL), F32).at[0].set(c_ctx).at[1:1 + dec_batch].set(c)
    mod_all = _mod_proj(cond, mod_w, mod_b).reshape(depth, rows, 6, D_MODEL)
    mods_ctx = [mod_all[l, 0:1] for l in range(depth)]
    mods_lat = [mod_all[l, 1:1 + dec_batch] for l in range(depth)]

    y_prompt, st = _trunk(x_prompt.reshape(batch * seq, D_MODEL), mods_ctx, batch * seq, batch, seq, None, p)
    caches = dict(state_gdn=state_gdn, cache_diff_k=cache_diff_k, cache_diff_v=cache_diff_v,
                  cache_gqa_k=cache_gqa_k, cache_gqa_v=cache_gqa_v,
                  cache_swa_k=cache_swa_k, cache_swa_v=cache_swa_v)
    y_sample, _ = _trunk(x_sample.reshape(dec_batch * dec_seq, D_MODEL), mods_lat, dec_seq, dec_batch, dec_seq,
                         caches, p)

    def stack(xs):
        return jnp.stack(xs, axis=1)

    return (y_prompt.reshape(batch, seq, D_MODEL), y_sample.reshape(dec_batch, dec_seq, D_MODEL),
            stack(st["gdn"]), stack(st["diff_k"]), stack(st["diff_v"]),
            stack(st["gqa_k"]), stack(st["gqa_v"]), stack(st["swa_k"]), stack(st["swa_v"]))
```

```python
import functools
import math

import jax
import jax.numpy as jnp
from jax import lax
from jax.experimental import pallas as pl
from jax.experimental.pallas import tpu as pltpu

F32 = jnp.float32
BF16 = jnp.bfloat16

D_MODEL = 1024
GRID_W = 64
ROPE_THETA = 10000.0
NORM_EPS = 1e-6
LANES = 128
GDN_HEADS = 8
GDN_DK = 128
GDN_CHUNK = 64
GDN_SUPER = 256
DIFF_HEADS = 8
DIFF_DH = 64
GQA_HEADS = 8
GQA_KV = 2
GQA_HD = 128
SWA_HEADS = 16
SWA_KV = 4
SWA_HD = 64
WINDOW = 128
N_EXPERTS = 16
EC_CAPACITY_FACTOR = 2
VMEM_LIMIT_BYTES = 56 * 1024 * 1024


def _params(*sem):
    return pltpu.CompilerParams(dimension_semantics=sem, vmem_limit_bytes=VMEM_LIMIT_BYTES)


def _silu(x):
    return x / (1.0 + jnp.exp(-x))


def _rms(x, g):
    return x * lax.rsqrt(jnp.mean(x * x, axis=-1, keepdims=True) + NORM_EPS) * g


def _dot(a, b):
    return jnp.dot(a, b, preferred_element_type=F32)


def _dot_nt(a, b):
    return lax.dot_general(a, b, (((1,), (1,)), ((), ())), preferred_element_type=F32)


def _split_bf16(x, parts):
    out = []
    for _ in range(parts - 1):
        hi = x.astype(BF16)
        out.append(hi)
        x = x - hi.astype(F32)
    out.append(x.astype(BF16))
    return out


def _mod_kernel(c_ref, w_ref, b_ref, o_ref):
    a = _silu(c_ref[...]).astype(BF16)
    o_ref[0] = _dot(a, w_ref[0].astype(BF16)) + b_ref[0]


def _mod_proj(cond, mod_w, mod_b):
    depth, _, width = mod_w.shape
    rows = cond.shape[0]
    tn = 1536
    return pl.pallas_call(
        _mod_kernel,
        grid=(depth, width // tn),
        in_specs=[pl.BlockSpec((rows, D_MODEL), lambda l, j: (0, 0)),
                  pl.BlockSpec((1, D_MODEL, tn), lambda l, j: (l, 0, j)),
                  pl.BlockSpec((1, 1, tn), lambda l, j: (l, 0, j))],
        out_specs=pl.BlockSpec((1, rows, tn), lambda l, j: (l, 0, j)),
        out_shape=jax.ShapeDtypeStruct((depth, rows, width), F32),
        compiler_params=_params("parallel", "parallel"),
        name="mod_proj",
    )(cond, mod_w, mod_b.reshape(depth, 1, width))


def _norm_mod(x, g, mod_ref, sh, sc):
    return _rms(x, g) * (1.0 + mod_ref[0, sc:sc + 1, :]) + mod_ref[0, sh:sh + 1, :]


def _nmm_kernel(x_ref, g_ref, mod_ref, w_ref, o_ref, *, sh, sc, chunk):
    h = _norm_mod(x_ref[...], g_ref[...], mod_ref, sh, sc).astype(BF16)
    width = w_ref.shape[1]
    for c0 in range(0, width, chunk):
        c1 = min(c0 + chunk, width)
        o_ref[:, c0:c1] = _dot(h, w_ref[:, c0:c1])


def _nmm(x, g, mod, w, rows_per_mod, sh, sc, tm=256):
    m = x.shape[0]
    width = w.shape[1]
    return pl.pallas_call(
        functools.partial(_nmm_kernel, sh=sh, sc=sc, chunk=512),
        grid=(m // tm,),
        in_specs=[pl.BlockSpec((tm, D_MODEL), lambda i: (i, 0)),
                  pl.BlockSpec((1, D_MODEL), lambda i: (0, 0)),
                  pl.BlockSpec((1, 6, D_MODEL), lambda i: ((i * tm) // rows_per_mod, 0, 0)),
                  pl.BlockSpec((D_MODEL, width), lambda i: (0, 0))],
        out_specs=pl.BlockSpec((tm, width), lambda i: (i, 0)),
        out_shape=jax.ShapeDtypeStruct((m, width), F32),
        compiler_params=_params("parallel"),
        name="norm_mod_matmul",
    )(x, g.reshape(1, D_MODEL), mod, w)


def _mmr_kernel(a_ref, w_ref, r_ref, mod_ref, o_ref, *, gi):
    acc = _dot(a_ref[...].astype(BF16), w_ref[...])
    o_ref[...] = r_ref[...] + mod_ref[0, gi:gi + 1, :] * acc


def _mm_resid(a, w, res, mod, rows_per_mod, gi, tm=512):
    m = a.shape[0]
    return pl.pallas_call(
        functools.partial(_mmr_kernel, gi=gi),
        grid=(m // tm,),
        in_specs=[pl.BlockSpec((tm, D_MODEL), lambda i: (i, 0)),
                  pl.BlockSpec((D_MODEL, D_MODEL), lambda i: (0, 0)),
                  pl.BlockSpec((tm, D_MODEL), lambda i: (i, 0)),
                  pl.BlockSpec((1, 6, D_MODEL), lambda i: ((i * tm) // rows_per_mod, 0, 0))],
        out_specs=pl.BlockSpec((tm, D_MODEL), lambda i: (i, 0)),
        out_shape=jax.ShapeDtypeStruct((m, D_MODEL), F32),
        compiler_params=_params("parallel"),
        name="out_proj_resid",
    )(a, w, res, mod)


def _rope_tables(n_tokens, dim):
    rows = n_tokens // GRID_W
    row = jnp.broadcast_to(jnp.arange(rows, dtype=F32)[:, None], (rows, GRID_W)).reshape(-1)
    col = jnp.broadcast_to(jnp.arange(GRID_W, dtype=F32)[None, :], (rows, GRID_W)).reshape(-1)
    n_freq = dim // 4
    inv = ROPE_THETA ** (-jnp.arange(n_freq, dtype=F32) / n_freq)
    ang = jnp.concatenate([row[:, None] * inv, col[:, None] * inv], axis=-1)
    cos, sin = jnp.cos(ang), jnp.sin(ang)
    zero = jnp.zeros_like(sin)
    reps = LANES // dim
    cos_t = jnp.tile(jnp.concatenate([cos, cos], axis=-1), (1, reps))
    sin_a = jnp.tile(jnp.concatenate([-sin, zero], axis=-1), (1, reps))
    sin_b = jnp.tile(jnp.concatenate([zero, sin], axis=-1), (1, reps))
    return cos_t, sin_a, sin_b


def _rope(x, cos, sin_a, sin_b, dim):
    half = dim // 2
    return x * cos + pltpu.roll(x, LANES - half, 1) * sin_a + pltpu.roll(x, half, 1) * sin_b


def _prep_kernel(*refs, kind, rope):
    if kind == "gqa":
        p_ref, cos_ref, sa_ref, sb_ref, qg_ref, kg_ref, q_out, k_out, v_out, kf_out = refs
    elif kind == "diff" and not rope:
        p_ref, cos_ref, sa_ref, sb_ref, q_out, k_out, v_out, kc_out, vc_out = refs
        hw = DIFF_HEADS * LANES
        _store_token_tiles(kc_out, p_ref[:, hw:2 * hw])
        _store_token_tiles(vc_out, p_ref[:, 2 * hw:3 * hw])
    else:
        p_ref, cos_ref, sa_ref, sb_ref, q_out, k_out, v_out = refs
    cos, sa, sb = cos_ref[...], sa_ref[...], sb_ref[...]

    def blk(j):
        return p_ref[:, j * LANES:(j + 1) * LANES]

    if kind == "gqa":
        scale = GQA_HD ** -0.5 * LOG2E
        for j in range(GQA_HEADS):
            y = _rms(blk(j), qg_ref[...])
            if rope:
                y = _rope(y, cos, sa, sb, GQA_HD)
            q_out[:, j * LANES:(j + 1) * LANES] = (y * scale).astype(BF16)
        for j in range(GQA_KV):
            y = _rms(blk(GQA_HEADS + j), kg_ref[...])
            kf_out[:, j * LANES:(j + 1) * LANES] = y
            if rope:
                y = _rope(y, cos, sa, sb, GQA_HD)
            k_out[:, j * LANES:(j + 1) * LANES] = y.astype(BF16)
        v0 = (GQA_HEADS + GQA_KV) * LANES
        v_out[...] = p_ref[:, v0:v0 + GQA_KV * LANES].astype(BF16)
    elif kind == "diff":
        scale = DIFF_DH ** -0.5 * LOG2E
        for j in range(DIFF_HEADS):
            y = blk(j)
            if rope:
                y = _rope(y, cos, sa, sb, DIFF_DH)
            q_out[:, j * LANES:(j + 1) * LANES] = (y * scale).astype(BF16)
            y = blk(DIFF_HEADS + j)
            if rope:
                y = _rope(y, cos, sa, sb, DIFF_DH)
            k_out[:, j * LANES:(j + 1) * LANES] = y.astype(BF16)
        v_out[...] = p_ref[:, 2 * DIFF_HEADS * LANES:3 * DIFF_HEADS * LANES].astype(BF16)
    else:
        scale = SWA_HD ** -0.5 * LOG2E
        low = lax.broadcasted_iota(jnp.int32, cos.shape, 1) < SWA_HD

        def pad_pair(y, out, j):
            out[:, (2 * j) * LANES:(2 * j + 1) * LANES] = jnp.where(low, y, 0.0).astype(BF16)
            out[:, (2 * j + 1) * LANES:(2 * j + 2) * LANES] = jnp.where(
                low, pltpu.roll(y, SWA_HD, 1), 0.0).astype(BF16)

        nq = SWA_HEADS // 2
        for j in range(nq):
            y = blk(j)
            if rope:
                y = _rope(y, cos, sa, sb, SWA_HD)
            pad_pair(y * scale, q_out, j)
        for j in range(SWA_KV // 2):
            y = blk(nq + j)
            if rope:
                y = _rope(y, cos, sa, sb, SWA_HD)
            pad_pair(y, k_out, j)
        v0 = (nq + SWA_KV // 2) * LANES
        v_out[...] = p_ref[:, v0:v0 + (SWA_KV // 2) * LANES].astype(BF16)


def _attn_prep(proj, kind, seq, rope, gains=None, tm=256):
    m, width = proj.shape
    dim = GQA_HD if kind == "gqa" else DIFF_DH
    cos, sa, sb = _rope_tables(seq, dim)
    nt = seq // tm
    if kind == "gqa":
        wq, wk, wv = GQA_HEADS * LANES, GQA_KV * LANES, GQA_KV * LANES
    elif kind == "diff":
        wq, wk, wv = DIFF_HEADS * LANES, DIFF_HEADS * LANES, DIFF_HEADS * LANES
    else:
        wq, wk, wv = SWA_HEADS * LANES, SWA_KV * LANES, (SWA_KV // 2) * LANES
    tab = pl.BlockSpec((tm, LANES), lambda i: (i % nt, 0))
    in_specs = [pl.BlockSpec((tm, width), lambda i: (i, 0)), tab, tab, tab]
    args = [proj, cos, sa, sb]
    out_shapes = [jax.ShapeDtypeStruct((m, wq), BF16), jax.ShapeDtypeStruct((m, wk), BF16),
                  jax.ShapeDtypeStruct((m, wv), BF16)]
    out_specs = [pl.BlockSpec((tm, wq), lambda i: (i, 0)), pl.BlockSpec((tm, wk), lambda i: (i, 0)),
                 pl.BlockSpec((tm, wv), lambda i: (i, 0))]
    if kind == "gqa":
        gspec = pl.BlockSpec((1, LANES), lambda i: (0, 0))
        in_specs += [gspec, gspec]
        args += [gains[0].reshape(1, LANES), gains[1].reshape(1, LANES)]
        out_shapes.append(jax.ShapeDtypeStruct((m, wk), F32))
        out_specs.append(pl.BlockSpec((tm, wk), lambda i: (i, 0)))
    elif kind == "diff" and not rope:
        out_shapes += [jax.ShapeDtypeStruct((m * SUBLANES, LANES), F32)] * 2
        out_specs += [pl.BlockSpec((tm * SUBLANES, LANES), lambda i: (i, 0))] * 2
    return pl.pallas_call(
        functools.partial(_prep_kernel, kind=kind, rope=rope),
        grid=(m // tm,),
        in_specs=in_specs,
        out_specs=out_specs,
        out_shape=out_shapes,
        compiler_params=_params("parallel"),
        name="attn_prep_" + kind,
    )(*args)


LOG2E = 1.4426950408889634


def _with_ones(v):
    return jnp.concatenate([v, jnp.ones_like(v)], axis=1)


def _softmax_pv(s, mx, v_ones):
    e = jnp.exp2(s - mx).astype(BF16)
    acc = _dot(e, v_ones)
    return acc[:, :LANES], acc[:, LANES:LANES + 1]


def _pick_half(o_a, o_b, upper):
    low = lax.broadcasted_iota(jnp.int32, o_a.shape, 1) < SWA_HD
    if upper:
        return jnp.where(low, pltpu.roll(o_a, SWA_HD, 1), o_b)
    return jnp.where(low, o_a, pltpu.roll(o_b, SWA_HD, 1))


def _attn_kernel(*refs, mode, lam_init):
    if mode == "gqa":
        q_ref, k_ref, v_ref, o_ref = refs
    elif mode == "diff":
        q_ref, k_ref, v_ref, lam_ref, g_ref, o_ref = refs
    else:
        q_ref, k_ref, v_ref, sink_ref, o_ref = refs
    v = _with_ones(v_ref[...])
    if mode == "gqa":
        k = k_ref[...]
        for j in range(GQA_HEADS // GQA_KV):
            s = _dot_nt(q_ref[:, j * LANES:(j + 1) * LANES], k)
            num, den = _softmax_pv(s, jnp.max(s, axis=-1, keepdims=True), v)
            o_ref[:, j * LANES:(j + 1) * LANES] = (num / den).astype(BF16)
    elif mode == "diff":
        k = k_ref[...]
        q = q_ref[...]
        low = lax.broadcasted_iota(jnp.int32, q.shape, 1) < DIFF_DH
        lv = lam_ref[...]
        lam = (jnp.exp(jnp.sum(lv[0:1] * lv[1:2], axis=-1, keepdims=True))
               - jnp.exp(jnp.sum(lv[2:3] * lv[3:4], axis=-1, keepdims=True)) + lam_init)
        s0 = _dot_nt(jnp.where(low, q, 0.0).astype(BF16), k)
        s1 = _dot_nt(jnp.where(low, 0.0, q).astype(BF16), k)
        n0, d0 = _softmax_pv(s0, jnp.max(s0, axis=-1, keepdims=True), v)
        n1, d1 = _softmax_pv(s1, jnp.max(s1, axis=-1, keepdims=True), v)
        o = n0 / d0 - n1 * (lam / d1)
        o_ref[...] = (_rms(o, g_ref[...]) * (1.0 - lam_init)).astype(BF16)
    else:
        heads = SWA_HEADS // (SWA_KV // 2)
        group = SWA_HEADS // SWA_KV
        outs = []
        for j in range(heads):
            a = j // group
            s = _dot_nt(q_ref[:, j * LANES:(j + 1) * LANES], k_ref[:, a * LANES:(a + 1) * LANES])
            sink = sink_ref[0, :, j:j + 1] * LOG2E
            mx = jnp.maximum(jnp.max(s, axis=-1, keepdims=True), sink)
            num, den = _softmax_pv(s, mx, v)
            outs.append(num / (den + jnp.exp2(sink - mx)))
        for jj in range(heads // 2):
            a = (2 * jj) // group
            o_ref[:, jj * LANES:(jj + 1) * LANES] = _pick_half(outs[2 * jj], outs[2 * jj + 1], a == 1).astype(BF16)


def _attention(q, k, v, mode, batch, seq, n_keys, extra=(), lam_init=0.0, tq=512):
    tq = min(tq, seq)
    nq = seq // tq
    if mode == "gqa":
        groups, wq, wk, wv, wo = GQA_KV, (GQA_HEADS // GQA_KV) * LANES, LANES, LANES, (GQA_HEADS // GQA_KV) * LANES
    elif mode == "diff":
        groups, wq, wk, wv, wo = DIFF_HEADS, LANES, LANES, LANES, LANES
    else:
        groups, wq, wk, wv, wo = SWA_KV // 2, SWA_HEADS // 2 * LANES, 2 * LANES, LANES, 4 * LANES
    in_specs = [pl.BlockSpec((tq, wq), lambda b, g, i: (b * nq + i, g)),
                pl.BlockSpec((n_keys, wk), lambda b, g, i: (b, g)),
                pl.BlockSpec((n_keys, wv), lambda b, g, i: (b, g))]
    args = [q, k, v]
    if mode == "diff":
        lam_vecs, subln_g = extra
        in_specs += [pl.BlockSpec((4, DIFF_DH), lambda b, g, i: (0, 0)),
                     pl.BlockSpec((1, LANES), lambda b, g, i: (0, 0))]
        args += [lam_vecs, subln_g.reshape(1, LANES)]
    elif mode == "swa":
        (sinks,) = extra
        in_specs.append(pl.BlockSpec((1, 1, SWA_HEADS // 2), lambda b, g, i: (g, 0, 0)))
        args.append(sinks.reshape(2, 1, SWA_HEADS // 2))
    return pl.pallas_call(
        functools.partial(_attn_kernel, mode=mode, lam_init=lam_init),
        grid=(batch, groups, nq),
        in_specs=in_specs,
        out_specs=pl.BlockSpec((tq, wo), lambda b, g, i: (b * nq + i, g)),
        out_shape=jax.ShapeDtypeStruct((batch * seq, D_MODEL), BF16),
        compiler_params=_params("parallel", "parallel", "parallel"),
        name="attn_" + mode,
    )(*args)


def _band_kernel(q_ref, kp_ref, kc_ref, kn_ref, kx_ref, vp_ref, vc_ref, vn_ref, vx_ref, sink_ref, o_ref, *, nqb):
    i = pl.program_id(2)
    qi = lax.broadcasted_iota(jnp.int32, (WINDOW, WINDOW), 0)
    kj = lax.broadcasted_iota(jnp.int32, (WINDOW, WINDOW), 1)
    open_blk = jnp.zeros((WINDOW, WINDOW), F32)
    bias = jnp.concatenate([jnp.where((kj >= qi) & (i > 0), 0.0, -jnp.inf), open_blk,
                            jnp.where((kj <= qi) & (i < nqb - 1), 0.0, -jnp.inf),
                            jnp.zeros((WINDOW, kx_ref.shape[0]), F32)], axis=1)
    keys = jnp.concatenate([kp_ref[...], kc_ref[...], kn_ref[...], kx_ref[...]], axis=0)
    vals = _with_ones(jnp.concatenate([vp_ref[...], vc_ref[...], vn_ref[...], vx_ref[...]], axis=0))
    heads = SWA_HEADS // (SWA_KV // 2)
    group = SWA_HEADS // SWA_KV
    outs = []
    for j in range(heads):
        a = j // group
        s = _dot_nt(q_ref[:, j * LANES:(j + 1) * LANES], keys[:, a * LANES:(a + 1) * LANES]) + bias
        sink = sink_ref[0, :, j:j + 1] * LOG2E
        mx = jnp.maximum(jnp.max(s, axis=-1, keepdims=True), sink)
        num, den = _softmax_pv(s, mx, vals)
        outs.append(num / (den + jnp.exp2(sink - mx)))
    for jj in range(heads // 2):
        a = (2 * jj) // group
        o_ref[:, jj * LANES:(jj + 1) * LANES] = _pick_half(outs[2 * jj], outs[2 * jj + 1], a == 1).astype(BF16)


def _band_attention(q, k, v, sinks, batch, seq, past):
    nqb = seq // WINDOW
    nkb = (seq + past) // WINDOW
    pairs = SWA_KV // 2
    xb = past // WINDOW
    assert past % WINDOW == 0 and seq % past == 0

    def band(shift, width):
        return pl.BlockSpec((WINDOW, width),
                            lambda b, g, i: (b * nkb + jnp.clip(i + shift, 0, nqb - 1), g))

    def ctx(width):
        return pl.BlockSpec((past, width), lambda b, g, i: (b * (nkb // xb) + seq // past, g))

    return pl.pallas_call(
        functools.partial(_band_kernel, nqb=nqb),
        grid=(batch, pairs, nqb),
        in_specs=[pl.BlockSpec((WINDOW, SWA_HEADS // 2 * LANES), lambda b, g, i: (b * nqb + i, g)),
                  band(-1, 2 * LANES), band(0, 2 * LANES), band(1, 2 * LANES), ctx(2 * LANES),
                  band(-1, LANES), band(0, LANES), band(1, LANES), ctx(LANES),
                  pl.BlockSpec((1, 1, SWA_HEADS // 2), lambda b, g, i: (g, 0, 0))],
        out_specs=pl.BlockSpec((WINDOW, 4 * LANES), lambda b, g, i: (b * nqb + i, g)),
        out_shape=jax.ShapeDtypeStruct((batch * seq, D_MODEL), BF16),
        compiler_params=_params("parallel", "parallel", "parallel"),
        name="attn_band",
    )(q, k, k, k, k, v, v, v, v, sinks.reshape(2, 1, SWA_HEADS // 2))


def _gdn_prep_kernel(p_ref, w_ref, o_ref):
    x = p_ref[0]
    t = x.shape[0]
    w = w_ref[...]
    row = lax.broadcasted_iota(jnp.int32, x.shape, 0)
    xm = jnp.where(row == 0, 0.0, pltpu.roll(x, 1, 0))
    xp = jnp.where(row == t - 1, 0.0, pltpu.roll(x, t - 1, 0))
    y = _silu(xm * w[0:1] + x * w[1:2] + xp * w[2:3])
    rs = lax.rsqrt(jnp.sum(y * y, axis=-1, keepdims=True) + NORM_EPS)
    j = pl.program_id(1)
    is_q = (j < GDN_HEADS).astype(F32)
    normed = (j < 2 * GDN_HEADS).astype(F32)
    scale = is_q * (GDN_DK ** -0.5) + (1.0 - is_q)
    o_ref[0] = y * (rs * (scale * normed) + (1.0 - normed))


def _gdn_prep(proj3, conv_w):
    batch, seq, _ = proj3.shape
    nblk = 3 * GDN_HEADS
    return pl.pallas_call(
        _gdn_prep_kernel,
        grid=(batch, nblk),
        in_specs=[pl.BlockSpec((1, seq, LANES), lambda b, j: (b, 0, j)),
                  pl.BlockSpec((3, LANES), lambda b, j: (0, j))],
        out_specs=pl.BlockSpec((1, seq, LANES), lambda b, j: (b, 0, j)),
        out_shape=jax.ShapeDtypeStruct((batch, seq, nblk * LANES), F32),
        compiler_params=_params("parallel", "parallel"),
        name="gdn_conv",
    )(proj3, conv_w)


def _gdn_gate_kernel(p_ref, al_ref, dtb_ref, o_ref, t_ref):
    ab = p_ref[...]
    lane = lax.broadcasted_iota(jnp.int32, ab.shape, 1)
    is_g = (lane < 4 * GDN_HEADS) & ((lane & GDN_HEADS) == 0)
    is_fwd = lane < 2 * GDN_HEADS
    x = ab + dtb_ref[...]
    softplus = jnp.maximum(x, 0.0) + jnp.log(1.0 + jnp.exp(-jnp.abs(x)))
    g = jnp.where(is_g, -jnp.exp(al_ref[...]) * softplus, 0.0)
    beta = 1.0 / (1.0 + jnp.exp(-ab))
    ri = lax.broadcasted_iota(jnp.int32, (GDN_SUPER, GDN_SUPER), 0)
    ci = lax.broadcasted_iota(jnp.int32, (GDN_SUPER, GDN_SUPER), 1)
    same = (ri // GDN_CHUNK) == (ci // GDN_CHUNK)
    lower = jnp.where(same & (ri >= ci), 1.0, 0.0).astype(BF16)
    upper = jnp.where(same & (ri <= ci), 1.0, 0.0).astype(BF16)
    parts = _split_bf16(g, 3)
    pre = _dot(lower, parts[0]) + _dot(lower, parts[1]) + _dot(lower, parts[2])
    suf = _dot(upper, parts[0]) + _dot(upper, parts[1]) + _dot(upper, parts[2])
    o_ref[...] = jnp.where(is_g, jnp.where(is_fwd, pre, suf), beta)
    t_ref[...] = pre + suf - g


def _gdn_gates(proj, a_log, dt_bias):
    m, width = proj.shape
    col_blk = (width - LANES) // LANES
    pad = jnp.zeros((2, GDN_HEADS), F32)

    def row(v):
        return jnp.pad(jnp.stack([v.astype(F32), pad], axis=1).reshape(1, 4 * GDN_HEADS),
                       ((0, 0), (0, LANES - 4 * GDN_HEADS)))

    spec = pl.BlockSpec((GDN_SUPER, LANES), lambda i: (i, 0))
    vec = pl.BlockSpec((1, LANES), lambda i: (0, 0))
    return pl.pallas_call(
        _gdn_gate_kernel,
        grid=(m // GDN_SUPER,),
        in_specs=[pl.BlockSpec((GDN_SUPER, LANES), lambda i: (i, col_blk)), vec, vec],
        out_specs=[spec, spec],
        out_shape=[jax.ShapeDtypeStruct((m, LANES), F32), jax.ShapeDtypeStruct((m, LANES), F32)],
        compiler_params=_params("parallel"),
        name="gdn_gates",
    )(proj, row(a_log), row(dt_bias))


def _gdn_kernel(q_ref, k_ref, v_ref, z_ref, col_ref, row_ref, s0_ref, ng_ref, o_ref, sf_ref, of_s, ob_s, *, nsc, hps):
    n = GDN_SUPER
    ri = lax.broadcasted_iota(jnp.int32, (n, n), 0)
    ci = lax.broadcasted_iota(jnp.int32, (n, n), 1)
    b64 = (ri >> 6) == (ci >> 6)
    b32 = (ri >> 5) == (ci >> 5)
    b16 = (ri >> 4) == (ci >> 4)
    eye = jnp.where(ri == ci, 1.0, 0.0)
    causal = (b64 & (ri >= ci), b64 & (ri <= ci))
    strict = (b64 & (ri > ci), b64 & (ri < ci))
    nchunk = n // GDN_CHUNK

    def mm(a, b):
        return _dot(a.astype(BF16), b.astype(BF16))

    def lockstep(chains, states):
        cs = range(len(chains))
        dirs = [d for (_, d, _) in chains]
        lanes = [slice(hh * LANES, (hh + 1) * LANES) for (hh, _, _) in chains]
        q = [q_ref[0, sc, :, lanes[c]] for c, (_, _, sc) in enumerate(chains)]
        k = [k_ref[0, sc, :, lanes[c]] for c, (_, _, sc) in enumerate(chains)]
        v = [v_ref[0, sc, :, lanes[c]] for c, (_, _, sc) in enumerate(chains)]
        col = [col_ref[hh, sc] for (hh, _, sc) in chains]
        row = [row_ref[hh, sc] for (hh, _, sc) in chains]
        gcol = [col[c][:, 3 * dirs[c]:3 * dirs[c] + 1] for c in cs]
        bcol = [col[c][:, 3 * dirs[c] + 1:3 * dirs[c] + 2] for c in cs]
        tcol = [col[c][:, 3 * dirs[c] + 2:3 * dirs[c] + 3] for c in cs]
        grow = [row[c][3 * dirs[c]:3 * dirs[c] + 1, :] for c in cs]
        trow = [row[c][3 * dirs[c] + 2:3 * dirs[c] + 3, :] for c in cs]
        decay = [jnp.exp(jnp.where(causal[dirs[c]], gcol[c] - grow[c], -jnp.inf)) for c in cs]
        k16 = [k[c].astype(BF16) for c in cs]
        kb = [k[c] * bcol[c] for c in cs]
        kk = [_dot_nt(kb[c].astype(BF16), k16[c]) for c in cs]
        qk = [_dot_nt(q[c].astype(BF16), k16[c]) for c in cs]
        lmat = [jnp.where(strict[dirs[c]], kk[c] * decay[c], 0.0) for c in cs]
        intra = [(qk[c] * decay[c]).astype(BF16) for c in cs]
        l0 = [jnp.where(b16, lmat[c], 0.0) for c in cs]
        inv = [eye - l0[c] for c in cs]
        pw = [mm(l0[c], l0[c]) for c in cs]
        for rep in range(3):
            inv = [inv[c] + mm(inv[c], pw[c]) for c in cs]
            if rep < 2:
                pw = [mm(pw[c], pw[c]) for c in cs]
        for keep in (b32 & (~b16), ~b32):
            cm = [jnp.where(keep, lmat[c], 0.0) for c in cs]
            t = [mm(inv[c], cm[c]) for c in cs]
            inv = [inv[c] - mm(t[c], inv[c]) for c in cs]
        eg = [jnp.exp(gcol[c]) for c in cs]
        uw = [mm(inv[c], jnp.concatenate([v[c] * bcol[c], kb[c] * eg[c]], axis=1)) for c in cs]
        u = [uw[c][:, :LANES] for c in cs]
        w = [uw[c][:, LANES:].astype(BF16) for c in cs]
        qd = [(q[c] * eg[c]).astype(BF16) for c in cs]
        kdt = [jnp.transpose(k[c] * jnp.exp(tcol[c] - gcol[c])).astype(BF16) for c in cs]
        outs = [[None] * nchunk for _ in cs]
        states = list(states)
        for step in range(nchunk):
            js = [step if dirs[c] == 0 else nchunk - 1 - step for c in cs]
            rs = [slice(j * GDN_CHUNK, (j + 1) * GDN_CHUNK) for j in js]
            s16 = [states[c].astype(BF16) for c in cs]
            vn16 = [(u[c][rs[c]] - _dot(w[c][rs[c]], s16[c])).astype(BF16) for c in cs]
            qs = [_dot(qd[c][rs[c]], s16[c]) for c in cs]
            for c in cs:
                outs[c][js[c]] = qs[c] + _dot(intra[c][rs[c], rs[c]], vn16[c])
            states = [states[c] * jnp.exp(trow[c][:, js[c] * GDN_CHUNK:js[c] * GDN_CHUNK + 1])
                      + _dot(kdt[c][:, rs[c]], vn16[c]) for c in cs]
        return [jnp.concatenate(outs[c], axis=0) for c in cs], states

    def body(i, carry):
        chains = [(hh, d, i if d == 0 else nsc - 1 - i) for hh in range(hps) for d in range(2)]
        outs, states = lockstep(chains, carry)
        for c, (hh, d, sc) in enumerate(chains):
            if d == 0:
                of_s[hh, sc] = outs[c]
            else:
                ob_s[hh, sc] = outs[c]
        return tuple(states)

    init = tuple(s0_ref[0, d, hh] for hh in range(hps) for d in range(2))
    fin = body(0, init) if nsc == 1 else lax.fori_loop(0, nsc, body, init)
    for hh in range(hps):
        sf_ref[0, 0, hh] = fin[2 * hh]
        sf_ref[0, 1, hh] = fin[2 * hh + 1]
        lanes = slice(hh * LANES, (hh + 1) * LANES)
        for sc in range(nsc):
            o = _rms(of_s[hh, sc] + ob_s[hh, sc], ng_ref[...])
            o_ref[0, sc, :, lanes] = (o * _silu(z_ref[0, sc, :, lanes])).astype(BF16)


def _gdn_scan(qkv, proj, cols, rows, s0, norm_g, batch, seq):
    nsc = seq // GDN_SUPER
    hps = 4 if nsc == 1 else 2
    h = GDN_HEADS
    qkv4 = qkv.reshape(batch, nsc, GDN_SUPER, 3 * h * LANES)
    proj4 = proj.reshape(batch, nsc, GDN_SUPER, proj.shape[-1])

    def tile(off):
        return pl.BlockSpec((1, nsc, GDN_SUPER, hps * LANES), lambda b, j: (b, 0, 0, off // hps + j))

    st = pl.BlockSpec((1, 2, hps, GDN_DK, LANES), lambda b, j: (b, 0, j, 0, 0))
    return pl.pallas_call(
        functools.partial(_gdn_kernel, nsc=nsc, hps=hps),
        grid=(batch, h // hps),
        in_specs=[tile(0), tile(h), tile(2 * h), tile(3 * h),
                  pl.BlockSpec((hps, nsc, GDN_SUPER, 8), lambda b, j: (j, b, 0, 0)),
                  pl.BlockSpec((hps, nsc, 8, GDN_SUPER), lambda b, j: (j, b, 0, 0)),
                  st, pl.BlockSpec((1, LANES), lambda b, j: (0, 0))],
        out_specs=[tile(0), st],
        out_shape=[jax.ShapeDtypeStruct((batch, nsc, GDN_SUPER, h * LANES), BF16),
                   jax.ShapeDtypeStruct((batch, 2, h, GDN_DK, LANES), F32)],
        scratch_shapes=[pltpu.VMEM((hps, nsc, GDN_SUPER, LANES), F32),
                        pltpu.VMEM((hps, nsc, GDN_SUPER, LANES), F32)],
        compiler_params=_params("parallel", "parallel"),
        name="gdn_scan",
    )(qkv4, qkv4, qkv4, proj4, cols, rows, s0, norm_g.reshape(1, LANES))


def _gdn_mixer(x, g, mod, rows_per_mod, batch, seq, s0, w_cat, conv_w, a_log, dt_bias, norm_g):
    proj = _nmm(x, g, mod, w_cat, rows_per_mod, 0, 1)
    qkv = _gdn_prep(proj.reshape(batch, seq, proj.shape[-1]), conv_w)
    gates, tot = _gdn_gates(proj, a_log, dt_bias)
    h = GDN_HEADS
    n = batch * seq
    per = jnp.stack([gates[:, 0:h], gates[:, h:2 * h], tot[:, 0:h],
                     gates[:, 2 * h:3 * h], gates[:, 3 * h:4 * h], tot[:, 2 * h:3 * h],
                     jnp.zeros((n, h), F32), jnp.zeros((n, h), F32)], axis=-1)
    per = per.reshape(n // GDN_SUPER, GDN_SUPER, h, 8)
    cols = jnp.transpose(per, (2, 0, 1, 3))
    rows = jnp.transpose(per, (2, 0, 3, 1))
    o, s_fin = _gdn_scan(qkv, proj, cols, rows, s0, norm_g, batch, seq)
    return o.reshape(n, D_MODEL), s_fin


SUBLANES = 8


def _store_token_tiles(ref, x):
    rows = x.shape[0]
    for j in range(SUBLANES):
        ref[pl.ds(j, rows, stride=SUBLANES), :] = x[:, j * LANES:(j + 1) * LANES]


def _load_token_tiles(ref, rows):
    return jnp.concatenate([ref[pl.ds(j, rows, stride=SUBLANES), :] for j in range(SUBLANES)], axis=1)


def _moe_pre_kernel(x_ref, g_ref, mod_ref, wr_ref, h_ref, aff_ref):
    h = _norm_mod(x_ref[...], g_ref[...], mod_ref, 3, 4)
    _store_token_tiles(h_ref, h)
    h1, h2 = _split_bf16(h, 2)
    logits = _dot_nt(wr_ref[0], h1) + (_dot_nt(wr_ref[1], h1) + _dot_nt(wr_ref[0], h2))
    e = jnp.exp(logits - jnp.max(logits, axis=0, keepdims=True))
    aff_ref[...] = e / jnp.sum(e, axis=0, keepdims=True)


def _moe_pre(x, g, mod, rows_per_mod, w_router, tm=512):
    m = x.shape[0]
    w_router = w_router.T
    w1 = w_router.astype(BF16)
    w2 = (w_router - w1.astype(F32)).astype(BF16)
    return pl.pallas_call(
        _moe_pre_kernel,
        grid=(m // tm,),
        in_specs=[pl.BlockSpec((tm, D_MODEL), lambda i: (i, 0)),
                  pl.BlockSpec((1, D_MODEL), lambda i: (0, 0)),
                  pl.BlockSpec((1, 6, D_MODEL), lambda i: ((i * tm) // rows_per_mod, 0, 0)),
                  pl.BlockSpec((2, N_EXPERTS, D_MODEL), lambda i: (0, 0, 0))],
        out_specs=[pl.BlockSpec((tm * SUBLANES, LANES), lambda i: (i, 0)),
                   pl.BlockSpec((N_EXPERTS, tm), lambda i: (0, i))],
        out_shape=[jax.ShapeDtypeStruct((m * SUBLANES, LANES), F32), jax.ShapeDtypeStruct((N_EXPERTS, m), F32)],
        compiler_params=_params("parallel"),
        name="moe_router",
    )(x, g.reshape(1, D_MODEL), mod, jnp.stack([w1, w2]))


def _route_kernel(aff_ref, gat_ref, sct_ref, gsel_ref, cnt_ref, pos_ref, mask_s, rank_s, *, cap):
    e = pl.program_id(1)
    nb = aff_ref.shape[1]

    def count(m):
        return jnp.sum(jnp.sum(jnp.where(m, 1.0, 0.0), axis=1, keepdims=True), axis=0, keepdims=True)

    li = lax.broadcasted_iota(jnp.int32, (LANES, LANES), 0)
    lj = lax.broadcasted_iota(jnp.int32, (LANES, LANES), 1)
    upper = jnp.where(li <= lj, 1.0, 0.0).astype(BF16)
    bi = lax.broadcasted_iota(jnp.int32, (nb, nb), 0)
    bj = lax.broadcasted_iota(jnp.int32, (nb, nb), 1)
    before = jnp.where(bj < bi, 1.0, 0.0).astype(BF16)

    def prefix(m):
        inside = _dot(m.astype(BF16), upper)
        total = jnp.broadcast_to(inside[:, LANES - 1:LANES], inside.shape)
        return inside, _dot(before, total.astype(BF16))

    def split64(x):
        hi = jnp.floor(x * (1.0 / 64.0))
        return hi, x - 64.0 * hi

    @pl.when(pl.program_id(0) == 0)
    def _():
        bits = pltpu.bitcast(aff_ref[0], jnp.int32)

        def bisect(_, lohi):
            lo, hi = lohi
            mid = lo + ((hi - lo) >> 1)
            ok = count(bits >= mid) >= cap
            return jnp.where(ok, mid, lo), jnp.where(ok, hi, mid)

        lo, _ = lax.fori_loop(0, 31, bisect,
                              (jnp.zeros((1, 1), jnp.int32), jnp.full((1, 1), 0x7F800000, jnp.int32)))
        above = bits > lo
        tied = bits == lo
        need = cap - count(above)
        tin, tbase = prefix(jnp.where(tied, 1.0, 0.0))
        mask = jnp.where(above | (tied & (tin + tbase <= need)), 1.0, 0.0)
        mask_s[e] = mask

        @pl.when(e == 0)
        def _():
            rank_s[...] = jnp.zeros_like(rank_s)

        rank_s[...] += mask

    @pl.when(pl.program_id(0) == 1)
    def _():
        @pl.when(e == 0)
        def _():
            cnt = rank_s[...]
            inside = _dot(cnt.astype(BF16), upper)
            hi, lo = split64(jnp.broadcast_to(inside[:, LANES - 1:LANES], inside.shape))
            earlier = 64.0 * _dot(before, hi.astype(BF16)) + _dot(before, lo.astype(BF16))
            cnt_ref[...] = cnt.astype(jnp.int32)
            pos_ref[...] = (inside + earlier - cnt).astype(jnp.int32)
            rank_s[...] = jnp.zeros_like(rank_s)

        a = aff_ref[0]
        mask = mask_s[e]
        rank = rank_s[...]
        pos = pos_ref[...].astype(F32)
        inside, base = prefix(mask)
        block_end = jnp.transpose(base + jnp.broadcast_to(inside[:, LANES - 1:LANES], base.shape))[0:1, :]
        slot = lax.broadcasted_iota(jnp.int32, (cap, 1), 0).astype(F32)
        blk = jnp.sum(jnp.where(block_end <= slot, 1.0, 0.0), axis=1, keepdims=True)
        pick = jnp.where(lax.broadcasted_iota(jnp.int32, (cap, nb), 1).astype(F32) == blk, 1.0, 0.0).astype(BF16)
        base_hi, base_lo = split64(base)
        pos_hi = jnp.floor(pos * (1.0 / 4096.0))
        pos_mid, pos_lo = split64(pos - 4096.0 * pos_hi)
        a1, a2, a3 = _split_bf16(a, 3)
        table = jnp.concatenate([inside.astype(BF16), base_hi.astype(BF16), base_lo.astype(BF16), rank.astype(BF16),
                                 a1, a2, a3, pos_hi.astype(BF16), pos_mid.astype(BF16), pos_lo.astype(BF16)],
                                axis=1)
        got = _dot(pick, table)

        def part(j):
            return got[:, j * LANES:(j + 1) * LANES]

        target = slot - (64.0 * part(1)[:, 0:1] + part(2)[:, 0:1])
        lane = jnp.sum(jnp.where(part(0) <= target, 1.0, 0.0), axis=1, keepdims=True)
        here = lax.broadcasted_iota(jnp.int32, (cap, LANES), 1).astype(F32) == lane

        def at_token(x):
            return jnp.sum(jnp.where(here, x, 0.0), axis=1, keepdims=True)

        dest = at_token(part(3)) + at_token(4096.0 * part(7) + 64.0 * part(8) + part(9))
        gat_ref[0] = ((LANES * blk + lane) * SUBLANES).astype(jnp.int32)
        sct_ref[0] = (dest * SUBLANES).astype(jnp.int32)
        gsel_ref[0] = at_token((part(4) + part(5)) + part(6))
        rank_s[...] = rank + mask


def _route(aff3, cap):
    n_exp, nb, _ = aff3.shape
    col = pl.BlockSpec((1, cap, 1), lambda p, e: (p * e, 0, 0))
    tok = pl.BlockSpec((nb, LANES), lambda p, e: (0, 0))
    return pl.pallas_call(
        functools.partial(_route_kernel, cap=cap),
        grid=(2, n_exp),
        in_specs=[pl.BlockSpec((1, nb, LANES), lambda p, e: (e, 0, 0))],
        out_specs=[col, col, col, tok, tok],
        out_shape=[jax.ShapeDtypeStruct((n_exp, cap, 1), jnp.int32), jax.ShapeDtypeStruct((n_exp, cap, 1), jnp.int32),
                   jax.ShapeDtypeStruct((n_exp, cap, 1), F32),
                   jax.ShapeDtypeStruct((nb, LANES), jnp.int32), jax.ShapeDtypeStruct((nb, LANES), jnp.int32)],
        scratch_shapes=[pltpu.VMEM((n_exp, nb, LANES), F32), pltpu.VMEM((nb, LANES), F32)],
        compiler_params=_params("arbitrary", "arbitrary"),
        name="moe_route",
    )(aff3)


def _ffn_kernel(g0_ref, gat_ref, sct_ref, gate_ref, wg_ref, wu_ref, wd_ref, h_hbm, z_hbm,
                xb0, xb1, yb0, yb1, wg_s, wu_s, wd_s, g_s, s_s, gsem, ssem, psem, *, tc, pairs_per_e, npairs):
    m = pl.program_id(0)
    hb = (m % 2) * (3 * tc)
    nb = 3 * tc - hb
    last = npairs - 1

    def tile_in(off, buf, i, sem):
        return pltpu.make_async_copy(h_hbm.at[pl.ds(pl.multiple_of(off, SUBLANES), SUBLANES)],
                                     buf.at[pl.ds(i * SUBLANES, SUBLANES)], sem)

    def tile_out(off, buf, i, sem):
        return pltpu.make_async_copy(buf.at[pl.ds(i * SUBLANES, SUBLANES)],
                                     z_hbm.at[pl.ds(pl.multiple_of(off, SUBLANES), SUBLANES)], sem)

    def tables(row, base):
        return (pltpu.make_async_copy(gat_ref.at[row], g_s.at[pl.ds(base, 2 * tc)], psem.at[0]),
                pltpu.make_async_copy(sct_ref.at[row], s_s.at[pl.ds(base, 3 * tc)], psem.at[1]))

    def wait_tiles(copy):
        for _ in range(tc):
            copy.wait()

    @pl.when(m == 0)
    def _():
        yb1[...] = jnp.zeros_like(yb1)
        first = pltpu.make_async_copy(g0_ref.at[0], g_s.at[pl.ds(0, tc)], psem.at[0])
        first.start()
        first.wait()
        for i in range(tc):
            tile_in(g_s[i], xb0, i, gsem.at[0]).start()
        for cp in tables(0, 0):
            cp.start()

    for cp in tables(m, hb):
        cp.wait()
    for cp in tables(jnp.minimum(m + 1, last), nb):
        cp.start()

    @pl.when(m % pairs_per_e == 0)
    def _():
        wg_s[...] = wg_ref[0, 0].astype(BF16)
        wu_s[...] = wu_ref[0, 0].astype(BF16)
        wd_s[...] = wd_ref[0, 0].astype(BF16)

    def ffn(xb, gate):
        x = _load_token_tiles(xb, tc).astype(BF16)
        half = wg_s.shape[1] // 2
        acc = None
        for c0 in (0, half):
            a = _dot(x, wg_s[:, c0:c0 + half])
            b = _dot(x, wu_s[:, c0:c0 + half])
            part = _dot((_silu(a) * b).astype(BF16), wd_s[c0:c0 + half, :])
            acc = part if acc is None else acc + part
        return acc * gate

    wait_tiles(tile_in(0, xb0, 0, gsem.at[0]))
    for i in range(tc):
        tile_in(g_s[hb + i], xb1, i, gsem.at[1]).start()
        tile_out(s_s[hb + i], yb1, i, ssem.at[1]).start()
    y = ffn(xb0, gate_ref[0, 0:tc])

    @pl.when(m > 0)
    def _():
        wait_tiles(tile_out(0, yb0, 0, ssem.at[0]))

    _store_token_tiles(yb0, y)

    wait_tiles(tile_in(0, xb1, 0, gsem.at[1]))
    for i in range(tc):
        tile_in(g_s[hb + tc + i], xb0, i, gsem.at[0]).start()
        tile_out(s_s[hb + tc + i], yb0, i, ssem.at[0]).start()
    y = ffn(xb1, gate_ref[0, tc:2 * tc])
    wait_tiles(tile_out(0, yb1, 0, ssem.at[1]))
    _store_token_tiles(yb1, y)

    @pl.when(m == last)
    def _():
        for i in range(tc):
            tile_out(s_s[hb + 2 * tc + i], yb1, i, ssem.at[1]).start()
        wait_tiles(tile_out(0, yb1, 0, ssem.at[1]))
        wait_tiles(tile_out(0, yb0, 0, ssem.at[0]))
        wait_tiles(tile_in(0, xb0, 0, gsem.at[0]))
        for cp in tables(last, nb):
            cp.wait()


def _expert_ffn(h_tiles, gat, sct, gate, w_gate, w_up, w_down, layer, z_rows, tc=512):
    n_exp, cap = gat.shape
    ff = w_gate.shape[-1]
    tc = min(tc, cap // 2)
    pairs_per_e = cap // (2 * tc)
    npairs = n_exp * pairs_per_e
    nsteps = 2 * npairs
    gat = gat.reshape(nsteps, tc)
    sct = sct.reshape(nsteps, tc)
    spare = ((z_rows - tc + jnp.arange(tc, dtype=jnp.int32)) * SUBLANES)[None]
    gat_next = jnp.concatenate([gat[1:], gat[:1]], axis=0).reshape(npairs, 2 * tc)
    sct_prev = jnp.concatenate([spare, sct[:-1]], axis=0).reshape(npairs, 2 * tc)
    sct3 = jnp.concatenate([sct_prev, sct[1::2]], axis=1)
    tile_buf = pltpu.VMEM((tc * SUBLANES, LANES), F32)
    wspec_in = pl.BlockSpec((1, 1, D_MODEL, ff), lambda m: (layer, m // pairs_per_e, 0, 0))
    return pl.pallas_call(
        functools.partial(_ffn_kernel, tc=tc, pairs_per_e=pairs_per_e, npairs=npairs),
        grid=(npairs,),
        in_specs=[pl.BlockSpec((1, tc), lambda m: (0, 0)),
                  pl.BlockSpec((npairs, 2 * tc), lambda m: (0, 0)),
                  pl.BlockSpec((npairs, 3 * tc), lambda m: (0, 0)),
                  pl.BlockSpec((1, 2 * tc, 1), lambda m: (m, 0, 0)),
                  wspec_in, wspec_in,
                  pl.BlockSpec((1, 1, ff, D_MODEL), lambda m: (layer, m // pairs_per_e, 0, 0)),
                  pl.BlockSpec(memory_space=pl.ANY)],
        out_specs=pl.BlockSpec(memory_space=pl.ANY),
        out_shape=jax.ShapeDtypeStruct((z_rows * SUBLANES, LANES), F32),
        scratch_shapes=[tile_buf, tile_buf, tile_buf, tile_buf,
                        pltpu.VMEM((D_MODEL, ff), BF16), pltpu.VMEM((D_MODEL, ff), BF16),
                        pltpu.VMEM((ff, D_MODEL), BF16),
                        pltpu.SMEM((6 * tc,), jnp.int32), pltpu.SMEM((6 * tc,), jnp.int32),
                        pltpu.SemaphoreType.DMA((2,)), pltpu.SemaphoreType.DMA((2,)),
                        pltpu.SemaphoreType.DMA((2,))],
        compiler_params=_params("arbitrary"),
        name="expert_ffn",
    )(gat[:1], gat_next, sct3, gate.reshape(npairs, 2 * tc, 1), w_gate, w_up, w_down, h_tiles)


COMBINE_BUFFERS = 3


def _combine_kernel(st_ref, x_ref, cnt_ref, pos_ref, mod_ref, fg_ref, y_hbm, o_ref, ybuf, acc, sem,
                    *, gi, final, tt, rc):
    i = pl.program_id(0)
    first = st_ref[i]
    nchunk = (st_ref[i + 1] - first + (rc - 1)) // rc

    def chunk(c, sl):
        row0 = pl.multiple_of((first + c * rc) * SUBLANES, SUBLANES)
        return pltpu.make_async_copy(y_hbm.at[pl.ds(row0, rc * SUBLANES)], ybuf.at[sl], sem.at[sl])

    for c0 in range(COMBINE_BUFFERS - 1):
        @pl.when(c0 < nchunk)
        def _():
            chunk(c0, c0).start()

    acc[...] = jnp.zeros_like(acc)
    begin = pos_ref[...] - first
    end = begin + cnt_ref[...]
    row = lax.broadcasted_iota(jnp.int32, (tt, rc), 1)

    def body(c, carry):
        sl = c % COMBINE_BUFFERS
        ahead = c + (COMBINE_BUFFERS - 1)

        @pl.when(ahead < nchunk)
        def _():
            chunk(ahead, ahead % COMBINE_BUFFERS).start()

        chunk(c, sl).wait()
        r = row + c * rc
        seg = jnp.where((r >= begin) & (r < end), 1.0, 0.0).astype(BF16)
        hi, lo = _split_bf16(_load_token_tiles(ybuf.at[sl], rc), 2)
        acc[...] += _dot(seg, hi) + _dot(seg, lo)
        return carry

    lax.fori_loop(0, nchunk, body, 0)
    x = x_ref[...] + mod_ref[0, gi:gi + 1, :] * acc[...]
    if final:
        x = _rms(x, fg_ref[...])
    o_ref[...] = x


def _combine(x, y, cnt, pos, mod, rows_per_mod, gi, final_g, final, tt, rc):
    m = x.shape[0]
    starts = jnp.concatenate([pos[::tt], (pos[-1:] + cnt[-1:])])
    col = pl.BlockSpec((tt, 1), lambda i, st: (i, 0))
    return pl.pallas_call(
        functools.partial(_combine_kernel, gi=gi, final=final, tt=tt, rc=rc),
        grid_spec=pltpu.PrefetchScalarGridSpec(
            num_scalar_prefetch=1,
            grid=(m // tt,),
            in_specs=[pl.BlockSpec((tt, D_MODEL), lambda i, st: (i, 0)), col, col,
                      pl.BlockSpec((1, 6, D_MODEL), lambda i, st: ((i * tt) // rows_per_mod, 0, 0)),
                      pl.BlockSpec((1, D_MODEL), lambda i, st: (0, 0)),
                      pl.BlockSpec(memory_space=pl.ANY)],
            out_specs=pl.BlockSpec((tt, D_MODEL), lambda i, st: (i, 0)),
            scratch_shapes=[pltpu.VMEM((COMBINE_BUFFERS, rc * SUBLANES, LANES), F32), pltpu.VMEM((tt, D_MODEL), F32),
                            pltpu.SemaphoreType.DMA((COMBINE_BUFFERS,))]),
        out_shape=jax.ShapeDtypeStruct((m, D_MODEL), F32),
        compiler_params=_params("arbitrary"),
        name="moe_combine",
    )(starts, x, cnt.reshape(m, 1), pos.reshape(m, 1), mod, final_g.reshape(1, D_MODEL), y)


def _expert_choice(x, g, mod, rows_per_mod, w_router, w_gate, w_up, w_down, layer, gi, final_g, final):
    n = x.shape[0]
    cap = EC_CAPACITY_FACTOR * n // N_EXPERTS
    tt = min(512, n)
    h, aff = _moe_pre(x, g, mod, rows_per_mod, w_router)
    gat, sct, gate, cnt, pos = _route(aff.reshape(N_EXPERTS, n // LANES, LANES), cap)
    gat, sct, gate = (a.reshape(N_EXPERTS, cap) for a in (gat, sct, gate))
    tc = min(512, cap // 2)
    y = _expert_ffn(h, gat, sct, gate, w_gate, w_up, w_down, layer, N_EXPERTS * cap + tc, tc)
    return _combine(x, y, cnt.reshape(n), pos.reshape(n), mod, rows_per_mod, gi, final_g, final, tt, tc)


def _trunk(x, mods, rows_per_mod, batch, seq, caches, p):
    depth = p["mod_w"].shape[0]
    n = batch * seq
    new = {}
    for layer in range(depth):
        kind, slot = layer % 4, layer // 4
        mod = mods[layer]
        g_mix = p["norm_mix_g"][layer]
        if kind == 0:
            w_ab = jnp.pad(p["gdn_w_ab"][slot], ((0, 0), (0, LANES - 4 * GDN_HEADS)))
            w_cat = jnp.concatenate([p["gdn_w_in"][slot], w_ab], axis=1).astype(BF16)
            if caches is None:
                s0 = jnp.zeros((batch, 2, GDN_HEADS, GDN_DK, LANES), F32)
            else:
                s0 = caches["state_gdn"][:, slot]
            o, s_fin = _gdn_mixer(x, g_mix, mod, rows_per_mod, batch, seq, s0, w_cat, p["gdn_conv_w"][slot],
                                  p["gdn_a_log"][slot], p["gdn_dt_bias"][slot], p["gdn_norm_g"][slot])
            new.setdefault("gdn", []).append(s_fin)
            w_out = p["gdn_w_out"][slot]
        elif kind == 1:
            proj = _nmm(x, g_mix, mod, p["diff_w_in"][slot].astype(BF16), rows_per_mod, 0, 1)
            nk = seq
            if caches is None:
                assert DIFF_HEADS == SUBLANES
                q, k, v, kc, vc = _attn_prep(proj, "diff", seq, False)
                new.setdefault("diff_k", []).append(kc.reshape(batch, seq, DIFF_HEADS, LANES))
                new.setdefault("diff_v", []).append(vc.reshape(batch, seq, DIFF_HEADS, LANES))
            else:
                q, k, v = _attn_prep(proj, "diff", seq, True)
                kc = caches["cache_diff_k"][:, slot].reshape(batch, -1, DIFF_HEADS * LANES).astype(BF16)
                vc = caches["cache_diff_v"][:, slot].reshape(batch, -1, DIFF_HEADS * LANES).astype(BF16)
                nk = seq + kc.shape[1]
                k = jnp.concatenate([k.reshape(batch, seq, -1), kc], axis=1).reshape(batch * nk, -1)
                v = jnp.concatenate([v.reshape(batch, seq, -1), vc], axis=1).reshape(batch * nk, -1)
            lam_init = 0.8 - 0.6 * math.exp(-0.3 * layer)
            o = _attention(q, k, v, "diff", batch, seq, nk,
                           (p["diff_lambda"][slot], p["diff_subln_g"][slot]), lam_init)
            w_out = p["diff_w_out"][slot]
        elif kind == 2:
            proj = _nmm(x, g_mix, mod, p["gqa_w_in"][slot].astype(BF16), rows_per_mod, 0, 1)
            q, k, v, kf = _attn_prep(proj, "gqa", seq, caches is not None,
                                     (p["gqa_q_norm_g"][slot], p["gqa_k_norm_g"][slot]))
            nk = seq
            if caches is None:
                v0 = (GQA_HEADS + GQA_KV) * LANES
                new.setdefault("gqa_k", []).append(kf.reshape(batch, seq, GQA_KV, GQA_HD))
                new.setdefault("gqa_v", []).append(proj[:, v0:].reshape(batch, seq, GQA_KV, GQA_HD))
            else:
                kc = caches["cache_gqa_k"][:, slot].reshape(batch, -1, GQA_KV * GQA_HD).astype(BF16)
                vc = caches["cache_gqa_v"][:, slot].reshape(batch, -1, GQA_KV * GQA_HD).astype(BF16)
                nk = seq + kc.shape[1]
                k = jnp.concatenate([k.reshape(batch, seq, -1), kc], axis=1).reshape(batch * nk, -1)
                v = jnp.concatenate([v.reshape(batch, seq, -1), vc], axis=1).reshape(batch * nk, -1)
            o = _attention(q, k, v, "gqa", batch, seq, nk)
            w_out = p["gqa_w_out"][slot]
        else:
            proj = _nmm(x, g_mix, mod, p["swa_w_in"][slot].astype(BF16), rows_per_mod, 0, 1)
            q, k, v = _attn_prep(proj, "swa", seq, caches is not None)
            sinks = p["swa_sinks"][slot].astype(F32)
            if caches is None:
                k0 = SWA_HEADS * SWA_HD
                v0 = k0 + SWA_KV * SWA_HD
                new.setdefault("swa_k", []).append(proj[:, k0:v0].reshape(batch, seq, SWA_KV, SWA_HD))
                new.setdefault("swa_v", []).append(proj[:, v0:].reshape(batch, seq, SWA_KV, SWA_HD))
                o = _attention(q, k, v, "swa", batch, seq, seq, (sinks,))
            else:
                kc = jnp.pad(caches["cache_swa_k"][:, slot], ((0, 0), (0, 0), (0, 0), (0, LANES - SWA_HD)))
                past = kc.shape[1]
                kc = kc.reshape(batch, past, SWA_KV * LANES).astype(BF16)
                vc = caches["cache_swa_v"][:, slot].reshape(batch, past, SWA_KV * SWA_HD).astype(BF16)
                nk = seq + past
                k = jnp.concatenate([k.reshape(batch, seq, -1), kc], axis=1).reshape(batch * nk, -1)
                v = jnp.concatenate([v.reshape(batch, seq, -1), vc], axis=1).reshape(batch * nk, -1)
                o = _band_attention(q, k, v, sinks, batch, seq, past)
            w_out = p["swa_w_out"][slot]
        x = _mm_resid(o, w_out.astype(BF16), x, mod, rows_per_mod, 2)
        x = _expert_choice(x, p["norm_ffn_g"][layer], mod, rows_per_mod, p["moe_router"][layer],
                           p["moe_w_gate"], p["moe_w_up"], p["moe_w_down"], layer,
                           5, p["final_g"], layer == depth - 1)
    return x, new


def kernel(x_prompt, x_sample, state_gdn, cache_diff_k, cache_diff_v, cache_gqa_k, cache_gqa_v, cache_swa_k, cache_swa_v, c, c_ctx, mod_w, mod_b, norm_mix_g, norm_ffn_g, final_g, gdn_w_in, gdn_conv_w, gdn_w_ab, gdn_a_log, gdn_dt_bias, gdn_norm_g, gdn_w_out, diff_w_in, diff_lambda, diff_subln_g, diff_w_out, gqa_w_in, gqa_q_norm_g, gqa_k_norm_g, gqa_w_out, swa_w_in, swa_sinks, swa_w_out, moe_router, moe_w_gate, moe_w_up, moe_w_down):
    p = dict(mod_w=mod_w, mod_b=mod_b, norm_mix_g=norm_mix_g, norm_ffn_g=norm_ffn_g, final_g=final_g,
             gdn_w_in=gdn_w_in, gdn_conv_w=gdn_conv_w, gdn_w_ab=gdn_w_ab, gdn_a_log=gdn_a_log,
             gdn_dt_bias=gdn_dt_bias, gdn_norm_g=gdn_norm_g, gdn_w_out=gdn_w_out,
             diff_w_in=diff_w_in, diff_lambda=diff_lambda, diff_subln_g=diff_subln_g, diff_w_out=diff_w_out,
             gqa_w_in=gqa_w_in, gqa_q_norm_g=gqa_q_norm_g, gqa_k_norm_g=gqa_k_norm_g, gqa_w_out=gqa_w_out,
             swa_w_in=swa_w_in, swa_sinks=swa_sinks, swa_w_out=swa_w_out,
             moe_router=moe_router, moe_w_gate=moe_w_gate, moe_w_up=moe_w_up, moe_w_down=moe_w_down)
    batch, seq, _ = x_prompt.shape
    dec_batch, dec_seq, _ = x_sample.shape
    depth = mod_w.shape[0]
    rows = 8 * ((1 + dec_batch + 7) // 8)
    cond = jnp.zeros((rows, D_MODEL), F32).at[0].set(c_ctx).at[1:1 + dec_batch].set(c)
    mod_all = _mod_proj(cond, mod_w, mod_b).reshape(depth, rows, 6, D_MODEL)
    mods_ctx = [mod_all[l, 0:1] for l in range(depth)]
    mods_lat = [mod_all[l, 1:1 + dec_batch] for l in range(depth)]

    y_prompt, st = _trunk(x_prompt.reshape(batch * seq, D_MODEL), mods_ctx, batch * seq, batch, seq, None, p)
    caches = dict(state_gdn=state_gdn, cache_diff_k=cache_diff_k, cache_diff_v=cache_diff_v,
                  cache_gqa_k=cache_gqa_k, cache_gqa_v=cache_gqa_v,
                  cache_swa_k=cache_swa_k, cache_swa_v=cache_swa_v)
    y_sample, _ = _trunk(x_sample.reshape(dec_batch * dec_seq, D_MODEL), mods_lat, dec_seq, dec_batch, dec_seq,
                         caches, p)

    def stack(xs):
        return jnp.stack(xs, axis=1)

    return (y_prompt.reshape(batch, seq, D_MODEL), y_sample.reshape(dec_batch, dec_seq, D_MODEL),
            stack(st["gdn"]), stack(st["diff_k"]), stack(st["diff_v"]),
            stack(st["gqa_k"]), stack(st["gqa_v"]), stack(st["swa_k"]), stack(st["swa_v"]))
```

```python
import functools
import math

import jax
import jax.numpy as jnp
from jax import lax
from jax.experimental import pallas as pl
from jax.experimental.pallas import tpu as pltpu

F32 = jnp.float32
BF16 = jnp.bfloat16

D_MODEL = 1024
GRID_W = 64
ROPE_THETA = 10000.0
NORM_EPS = 1e-6
LANES = 128
GDN_HEADS = 8
GDN_DK = 128
GDN_CHUNK = 64
GDN_SUPER = 256
DIFF_HEADS = 8
DIFF_DH = 64
GQA_HEADS = 8
GQA_KV = 2
GQA_HD = 128
SWA_HEADS = 16
SWA_KV = 4
SWA_HD = 64
WINDOW = 128
N_EXPERTS = 16
EC_CAPACITY_FACTOR = 2
VMEM_LIMIT_BYTES = 56 * 1024 * 1024


def _params(*sem):
    return pltpu.CompilerParams(dimension_semantics=sem, vmem_limit_bytes=VMEM_LIMIT_BYTES)


def _silu(x):
    return x / (1.0 + jnp.exp(-x))


def _rms(x, g):
    return x * lax.rsqrt(jnp.mean(x * x, axis=-1, keepdims=True) + NORM_EPS) * g


def _dot(a, b):
    return jnp.dot(a, b, preferred_element_type=F32)


def _dot_nt(a, b):
    return lax.dot_general(a, b, (((1,), (1,)), ((), ())), preferred_element_type=F32)


def _split_bf16(x, parts):
    out = []
    for _ in range(parts - 1):
        hi = x.astype(BF16)
        out.append(hi)
        x = x - hi.astype(F32)
    out.append(x.astype(BF16))
    return out


def _mod_kernel(c_ref, w_ref, b_ref, o_ref):
    a = _silu(c_ref[...]).astype(BF16)
    o_ref[0] = _dot(a, w_ref[0].astype(BF16)) + b_ref[0]


def _mod_proj(cond, mod_w, mod_b):
    depth, _, width = mod_w.shape
    rows = cond.shape[0]
    tn = 1536
    return pl.pallas_call(
        _mod_kernel,
        grid=(depth, width // tn),
        in_specs=[pl.BlockSpec((rows, D_MODEL), lambda l, j: (0, 0)),
                  pl.BlockSpec((1, D_MODEL, tn), lambda l, j: (l, 0, j)),
                  pl.BlockSpec((1, 1, tn), lambda l, j: (l, 0, j))],
        out_specs=pl.BlockSpec((1, rows, tn), lambda l, j: (l, 0, j)),
        out_shape=jax.ShapeDtypeStruct((depth, rows, width), F32),
        compiler_params=_params("parallel", "parallel"),
        name="mod_proj",
    )(cond, mod_w, mod_b.reshape(depth, 1, width))


def _norm_mod(x, g, mod_ref, sh, sc):
    return _rms(x, g) * (1.0 + mod_ref[0, sc:sc + 1, :]) + mod_ref[0, sh:sh + 1, :]


def _nmm_kernel(x_ref, g_ref, mod_ref, w_ref, o_ref, *, sh, sc, chunk):
    h = _norm_mod(x_ref[...], g_ref[...], mod_ref, sh, sc).astype(BF16)
    width = w_ref.shape[1]
    for c0 in range(0, width, chunk):
        c1 = min(c0 + chunk, width)
        o_ref[:, c0:c1] = _dot(h, w_ref[:, c0:c1])


def _nmm(x, g, mod, w, rows_per_mod, sh, sc, tm=256):
    m = x.shape[0]
    width = w.shape[1]
    return pl.pallas_call(
        functools.partial(_nmm_kernel, sh=sh, sc=sc, chunk=512),
        grid=(m // tm,),
        in_specs=[pl.BlockSpec((tm, D_MODEL), lambda i: (i, 0)),
                  pl.BlockSpec((1, D_MODEL), lambda i: (0, 0)),
                  pl.BlockSpec((1, 6, D_MODEL), lambda i: ((i * tm) // rows_per_mod, 0, 0)),
                  pl.BlockSpec((D_MODEL, width), lambda i: (0, 0))],
        out_specs=pl.BlockSpec((tm, width), lambda i: (i, 0)),
        out_shape=jax.ShapeDtypeStruct((m, width), F32),
        compiler_params=_params("parallel"),
        name="norm_mod_matmul",
    )(x, g.reshape(1, D_MODEL), mod, w)


def _mmr_kernel(a_ref, w_ref, r_ref, mod_ref, o_ref, *, gi):
    acc = _dot(a_ref[...].astype(BF16), w_ref[...])
    o_ref[...] = r_ref[...] + mod_ref[0, gi:gi + 1, :] * acc


def _mm_resid(a, w, res, mod, rows_per_mod, gi, tm=512):
    m = a.shape[0]
    return pl.pallas_call(
        functools.partial(_mmr_kernel, gi=gi),
        grid=(m // tm,),
        in_specs=[pl.BlockSpec((tm, D_MODEL), lambda i: (i, 0)),
                  pl.BlockSpec((D_MODEL, D_MODEL), lambda i: (0, 0)),
                  pl.BlockSpec((tm, D_MODEL), lambda i: (i, 0)),
                  pl.BlockSpec((1, 6, D_MODEL), lambda i: ((i * tm) // rows_per_mod, 0, 0))],
        out_specs=pl.BlockSpec((tm, D_MODEL), lambda i: (i, 0)),
        out_shape=jax.ShapeDtypeStruct((m, D_MODEL), F32),
        compiler_params=_params("parallel"),
        name="out_proj_resid",
    )(a, w, res, mod)


def _rope_tables(n_tokens, dim):
    rows = n_tokens // GRID_W
    row = jnp.broadcast_to(jnp.arange(rows, dtype=F32)[:, None], (rows, GRID_W)).reshape(-1)
    col = jnp.broadcast_to(jnp.arange(GRID_W, dtype=F32)[None, :], (rows, GRID_W)).reshape(-1)
    n_freq = dim // 4
    inv = ROPE_THETA ** (-jnp.arange(n_freq, dtype=F32) / n_freq)
    ang = jnp.concatenate([row[:, None] * inv, col[:, None] * inv], axis=-1)
    cos, sin = jnp.cos(ang), jnp.sin(ang)
    zero = jnp.zeros_like(sin)
    reps = LANES // dim
    cos_t = jnp.tile(jnp.concatenate([cos, cos], axis=-1), (1, reps))
    sin_a = jnp.tile(jnp.concatenate([-sin, zero], axis=-1), (1, reps))
    sin_b = jnp.tile(jnp.concatenate([zero, sin], axis=-1), (1, reps))
    return cos_t, sin_a, sin_b


def _rope(x, cos, sin_a, sin_b, dim):
    half = dim // 2
    return x * cos + pltpu.roll(x, LANES - half, 1) * sin_a + pltpu.roll(x, half, 1) * sin_b


def _prep_kernel(*refs, kind, rope):
    if kind == "gqa":
        p_ref, cos_ref, sa_ref, sb_ref, qg_ref, kg_ref, q_out, k_out, v_out, kf_out = refs
    elif kind == "diff" and not rope:
        p_ref, cos_ref, sa_ref, sb_ref, q_out, k_out, v_out, kc_out, vc_out = refs
        hw = DIFF_HEADS * LANES
        _store_token_tiles(kc_out, p_ref[:, hw:2 * hw])
        _store_token_tiles(vc_out, p_ref[:, 2 * hw:3 * hw])
    else:
        p_ref, cos_ref, sa_ref, sb_ref, q_out, k_out, v_out = refs
    cos, sa, sb = cos_ref[...], sa_ref[...], sb_ref[...]

    def blk(j):
        return p_ref[:, j * LANES:(j + 1) * LANES]

    if kind == "gqa":
        scale = GQA_HD ** -0.5 * LOG2E
        for j in range(GQA_HEADS):
            y = _rms(blk(j), qg_ref[...])
            if rope:
                y = _rope(y, cos, sa, sb, GQA_HD)
            q_out[:, j * LANES:(j + 1) * LANES] = (y * scale).astype(BF16)
        for j in range(GQA_KV):
            y = _rms(blk(GQA_HEADS + j), kg_ref[...])
            kf_out[:, j * LANES:(j + 1) * LANES] = y
            if rope:
                y = _rope(y, cos, sa, sb, GQA_HD)
            k_out[:, j * LANES:(j + 1) * LANES] = y.astype(BF16)
        v0 = (GQA_HEADS + GQA_KV) * LANES
        v_out[...] = p_ref[:, v0:v0 + GQA_KV * LANES].astype(BF16)
    elif kind == "diff":
        scale = DIFF_DH ** -0.5 * LOG2E
        for j in range(DIFF_HEADS):
            y = blk(j)
            if rope:
                y = _rope(y, cos, sa, sb, DIFF_DH)
            q_out[:, j * LANES:(j + 1) * LANES] = (y * scale).astype(BF16)
            y = blk(DIFF_HEADS + j)
            if rope:
                y = _rope(y, cos, sa, sb, DIFF_DH)
            k_out[:, j * LANES:(j + 1) * LANES] = y.astype(BF16)
        v_out[...] = p_ref[:, 2 * DIFF_HEADS * LANES:3 * DIFF_HEADS * LANES].astype(BF16)
    else:
        scale = SWA_HD ** -0.5 * LOG2E
        low = lax.broadcasted_iota(jnp.int32, cos.shape, 1) < SWA_HD

        def pad_pair(y, out, j):
            out[:, (2 * j) * LANES:(2 * j + 1) * LANES] = jnp.where(low, y, 0.0).astype(BF16)
            out[:, (2 * j + 1) * LANES:(2 * j + 2) * LANES] = jnp.where(
                low, pltpu.roll(y, SWA_HD, 1), 0.0).astype(BF16)

        nq = SWA_HEADS // 2
        for j in range(nq):
            y = blk(j)
            if rope:
                y = _rope(y, cos, sa, sb, SWA_HD)
            pad_pair(y * scale, q_out, j)
        for j in range(SWA_KV // 2):
            y = blk(nq + j)
            if rope:
                y = _rope(y, cos, sa, sb, SWA_HD)
            pad_pair(y, k_out, j)
        v0 = (nq + SWA_KV // 2) * LANES
        v_out[...] = p_ref[:, v0:v0 + (SWA_KV // 2) * LANES].astype(BF16)


def _attn_prep(proj, kind, seq, rope, gains=None, tm=256):
    m, width = proj.shape
    dim = GQA_HD if kind == "gqa" else DIFF_DH
    cos, sa, sb = _rope_tables(seq, dim)
    nt = seq // tm
    if kind == "gqa":
        wq, wk, wv = GQA_HEADS * LANES, GQA_KV * LANES, GQA_KV * LANES
    elif kind == "diff":
        wq, wk, wv = DIFF_HEADS * LANES, DIFF_HEADS * LANES, DIFF_HEADS * LANES
    else:
        wq, wk, wv = SWA_HEADS * LANES, SWA_KV * LANES, (SWA_KV // 2) * LANES
    tab = pl.BlockSpec((tm, LANES), lambda i: (i % nt, 0))
    in_specs = [pl.BlockSpec((tm, width), lambda i: (i, 0)), tab, tab, tab]
    args = [proj, cos, sa, sb]
    out_shapes = [jax.ShapeDtypeStruct((m, wq), BF16), jax.ShapeDtypeStruct((m, wk), BF16),
                  jax.ShapeDtypeStruct((m, wv), BF16)]
    out_specs = [pl.BlockSpec((tm, wq), lambda i: (i, 0)), pl.BlockSpec((tm, wk), lambda i: (i, 0)),
                 pl.BlockSpec((tm, wv), lambda i: (i, 0))]
    if kind == "gqa":
        gspec = pl.BlockSpec((1, LANES), lambda i: (0, 0))
        in_specs += [gspec, gspec]
        args += [gains[0].reshape(1, LANES), gains[1].reshape(1, LANES)]
        out_shapes.append(jax.ShapeDtypeStruct((m, wk), F32))
        out_specs.append(pl.BlockSpec((tm, wk), lambda i: (i, 0)))
    elif kind == "diff" and not rope:
        out_shapes += [jax.ShapeDtypeStruct((m * SUBLANES, LANES), F32)] * 2
        out_specs += [pl.BlockSpec((tm * SUBLANES, LANES), lambda i: (i, 0))] * 2
    return pl.pallas_call(
        functools.partial(_prep_kernel, kind=kind, rope=rope),
        grid=(m // tm,),
        in_specs=in_specs,
        out_specs=out_specs,
        out_shape=out_shapes,
        compiler_params=_params("parallel"),
        name="attn_prep_" + kind,
    )(*args)


LOG2E = 1.4426950408889634


def _with_ones(v):
    return jnp.concatenate([v, jnp.ones_like(v)], axis=1)


def _softmax_pv(s, mx, v_ones):
    e = jnp.exp2(s - mx).astype(BF16)
    acc = _dot(e, v_ones)
    return acc[:, :LANES], acc[:, LANES:LANES + 1]


def _pick_half(o_a, o_b, upper):
    low = lax.broadcasted_iota(jnp.int32, o_a.shape, 1) < SWA_HD
    if upper:
        return jnp.where(low, pltpu.roll(o_a, SWA_HD, 1), o_b)
    return jnp.where(low, o_a, pltpu.roll(o_b, SWA_HD, 1))


def _attn_kernel(*refs, mode, lam_init):
    if mode == "gqa":
        q_ref, k_ref, v_ref, o_ref = refs
    elif mode == "diff":
        q_ref, k_ref, v_ref, lam_ref, g_ref, o_ref = refs
    else:
        q_ref, k_ref, v_ref, sink_ref, o_ref = refs
    v = _with_ones(v_ref[...])
    if mode == "gqa":
        k = k_ref[...]
        for j in range(GQA_HEADS // GQA_KV):
            s = _dot_nt(q_ref[:, j * LANES:(j + 1) * LANES], k)
            num, den = _softmax_pv(s, jnp.max(s, axis=-1, keepdims=True), v)
            o_ref[:, j * LANES:(j + 1) * LANES] = (num / den).astype(BF16)
    elif mode == "diff":
        k = k_ref[...]
        q = q_ref[...]
        low = lax.broadcasted_iota(jnp.int32, q.shape, 1) < DIFF_DH
        lv = lam_ref[...]
        lam = (jnp.exp(jnp.sum(lv[0:1] * lv[1:2], axis=-1, keepdims=True))
               - jnp.exp(jnp.sum(lv[2:3] * lv[3:4], axis=-1, keepdims=True)) + lam_init)
        s0 = _dot_nt(jnp.where(low, q, 0.0).astype(BF16), k)
        s1 = _dot_nt(jnp.where(low, 0.0, q).astype(BF16), k)
        n0, d0 = _softmax_pv(s0, jnp.max(s0, axis=-1, keepdims=True), v)
        n1, d1 = _softmax_pv(s1, jnp.max(s1, axis=-1, keepdims=True), v)
        o = n0 / d0 - n1 * (lam / d1)
        o_ref[...] = (_rms(o, g_ref[...]) * (1.0 - lam_init)).astype(BF16)
    else:
        heads = SWA_HEADS // (SWA_KV // 2)
        group = SWA_HEADS // SWA_KV
        outs = []
        for j in range(heads):
            a = j // group
            s = _dot_nt(q_ref[:, j * LANES:(j + 1) * LANES], k_ref[:, a * LANES:(a + 1) * LANES])
            sink = sink_ref[0, :, j:j + 1] * LOG2E
            mx = jnp.maximum(jnp.max(s, axis=-1, keepdims=True), sink)
            num, den = _softmax_pv(s, mx, v)
            outs.append(num / (den + jnp.exp2(sink - mx)))
        for jj in range(heads // 2):
            a = (2 * jj) // group
            o_ref[:, jj * LANES:(jj + 1) * LANES] = _pick_half(outs[2 * jj], outs[2 * jj + 1], a == 1).astype(BF16)


def _attention(q, k, v, mode, batch, seq, n_keys, extra=(), lam_init=0.0, tq=512):
    tq = min(tq, seq)
    nq = seq // tq
    if mode == "gqa":
        groups, wq, wk, wv, wo = GQA_KV, (GQA_HEADS // GQA_KV) * LANES, LANES, LANES, (GQA_HEADS // GQA_KV) * LANES
    elif mode == "diff":
        groups, wq, wk, wv, wo = DIFF_HEADS, LANES, LANES, LANES, LANES
    else:
        groups, wq, wk, wv, wo = SWA_KV // 2, SWA_HEADS // 2 * LANES, 2 * LANES, LANES, 4 * LANES
    in_specs = [pl.BlockSpec((tq, wq), lambda b, g, i: (b * nq + i, g)),
                pl.BlockSpec((n_keys, wk), lambda b, g, i: (b, g)),
                pl.BlockSpec((n_keys, wv), lambda b, g, i: (b, g))]
    args = [q, k, v]
    if mode == "diff":
        lam_vecs, subln_g = extra
        in_specs += [pl.BlockSpec((4, DIFF_DH), lambda b, g, i: (0, 0)),
                     pl.BlockSpec((1, LANES), lambda b, g, i: (0, 0))]
        args += [lam_vecs, subln_g.reshape(1, LANES)]
    elif mode == "swa":
        (sinks,) = extra
        in_specs.append(pl.BlockSpec((1, 1, SWA_HEADS // 2), lambda b, g, i: (g, 0, 0)))
        args.append(sinks.reshape(2, 1, SWA_HEADS // 2))
    return pl.pallas_call(
        functools.partial(_attn_kernel, mode=mode, lam_init=lam_init),
        grid=(batch, groups, nq),
        in_specs=in_specs,
        out_specs=pl.BlockSpec((tq, wo), lambda b, g, i: (b * nq + i, g)),
        out_shape=jax.ShapeDtypeStruct((batch * seq, D_MODEL), BF16),
        compiler_params=_params("parallel", "parallel", "parallel"),
        name="attn_" + mode,
    )(*args)


def _band_kernel(q_ref, kp_ref, kc_ref, kn_ref, kx_ref, vp_ref, vc_ref, vn_ref, vx_ref, sink_ref, o_ref, *, nqb):
    i = pl.program_id(2)
    qi = lax.broadcasted_iota(jnp.int32, (WINDOW, WINDOW), 0)
    kj = lax.broadcasted_iota(jnp.int32, (WINDOW, WINDOW), 1)
    open_blk = jnp.zeros((WINDOW, WINDOW), F32)
    bias = jnp.concatenate([jnp.where((kj >= qi) & (i > 0), 0.0, -jnp.inf), open_blk,
                            jnp.where((kj <= qi) & (i < nqb - 1), 0.0, -jnp.inf),
                            jnp.zeros((WINDOW, kx_ref.shape[0]), F32)], axis=1)
    keys = jnp.concatenate([kp_ref[...], kc_ref[...], kn_ref[...], kx_ref[...]], axis=0)
    vals = _with_ones(jnp.concatenate([vp_ref[...], vc_ref[...], vn_ref[...], vx_ref[...]], axis=0))
    heads = SWA_HEADS // (SWA_KV // 2)
    group = SWA_HEADS // SWA_KV
    outs = []
    for j in range(heads):
        a = j // group
        s = _dot_nt(q_ref[:, j * LANES:(j + 1) * LANES], keys[:, a * LANES:(a + 1) * LANES]) + bias
        sink = sink_ref[0, :, j:j + 1] * LOG2E
        mx = jnp.maximum(jnp.max(s, axis=-1, keepdims=True), sink)
        num, den = _softmax_pv(s, mx, vals)
        outs.append(num / (den + jnp.exp2(sink - mx)))
    for jj in range(heads // 2):
        a = (2 * jj) // group
        o_ref[:, jj * LANES:(jj + 1) * LANES] = _pick_half(outs[2 * jj], outs[2 * jj + 1], a == 1).astype(BF16)


def _band_attention(q, k, v, sinks, batch, seq, past):
    nqb = seq // WINDOW
    nkb = (seq + past) // WINDOW
    pairs = SWA_KV // 2
    xb = past // WINDOW
    assert past % WINDOW == 0 and seq % past == 0

    def band(shift, width):
        return pl.BlockSpec((WINDOW, width),
                            lambda b, g, i: (b * nkb + jnp.clip(i + shift, 0, nqb - 1), g))

    def ctx(width):
        return pl.BlockSpec((past, width), lambda b, g, i: (b * (nkb // xb) + seq // past, g))

    return pl.pallas_call(
        functools.partial(_band_kernel, nqb=nqb),
        grid=(batch, pairs, nqb),
        in_specs=[pl.BlockSpec((WINDOW, SWA_HEADS // 2 * LANES), lambda b, g, i: (b * nqb + i, g)),
                  band(-1, 2 * LANES), band(0, 2 * LANES), band(1, 2 * LANES), ctx(2 * LANES),
                  band(-1, LANES), band(0, LANES), band(1, LANES), ctx(LANES),
                  pl.BlockSpec((1, 1, SWA_HEADS // 2), lambda b, g, i: (g, 0, 0))],
        out_specs=pl.BlockSpec((WINDOW, 4 * LANES), lambda b, g, i: (b * nqb + i, g)),
        out_shape=jax.ShapeDtypeStruct((batch * seq, D_MODEL), BF16),
        compiler_params=_params("parallel", "parallel", "parallel"),
        name="attn_band",
    )(q, k, k, k, k, v, v, v, v, sinks.reshape(2, 1, SWA_HEADS // 2))


def _gdn_prep_kernel(p_ref, w_ref, o_ref):
    x = p_ref[0]
    t = x.shape[0]
    w = w_ref[...]
    row = lax.broadcasted_iota(jnp.int32, x.shape, 0)
    xm = jnp.where(row == 0, 0.0, pltpu.roll(x, 1, 0))
    xp = jnp.where(row == t - 1, 0.0, pltpu.roll(x, t - 1, 0))
    y = _silu(xm * w[0:1] + x * w[1:2] + xp * w[2:3])
    rs = lax.rsqrt(jnp.sum(y * y, axis=-1, keepdims=True) + NORM_EPS)
    j = pl.program_id(1)
    is_q = (j < GDN_HEADS).astype(F32)
    normed = (j < 2 * GDN_HEADS).astype(F32)
    scale = is_q * (GDN_DK ** -0.5) + (1.0 - is_q)
    o_ref[0] = y * (rs * (scale * normed) + (1.0 - normed))


def _gdn_prep(proj3, conv_w):
    batch, seq, _ = proj3.shape
    nblk = 3 * GDN_HEADS
    return pl.pallas_call(
        _gdn_prep_kernel,
        grid=(batch, nblk),
        in_specs=[pl.BlockSpec((1, seq, LANES), lambda b, j: (b, 0, j)),
                  pl.BlockSpec((3, LANES), lambda b, j: (0, j))],
        out_specs=pl.BlockSpec((1, seq, LANES), lambda b, j: (b, 0, j)),
        out_shape=jax.ShapeDtypeStruct((batch, seq, nblk * LANES), F32),
        compiler_params=_params("parallel", "parallel"),
        name="gdn_conv",
    )(proj3, conv_w)


def _gdn_gate_kernel(p_ref, al_ref, dtb_ref, o_ref, t_ref):
    ab = p_ref[...]
    lane = lax.broadcasted_iota(jnp.int32, ab.shape, 1)
    is_g = (lane < 4 * GDN_HEADS) & ((lane & GDN_HEADS) == 0)
    is_fwd = lane < 2 * GDN_HEADS
    x = ab + dtb_ref[...]
    softplus = jnp.maximum(x, 0.0) + jnp.log(1.0 + jnp.exp(-jnp.abs(x)))
    g = jnp.where(is_g, -jnp.exp(al_ref[...]) * softplus, 0.0)
    beta = 1.0 / (1.0 + jnp.exp(-ab))
    ri = lax.broadcasted_iota(jnp.int32, (GDN_SUPER, GDN_SUPER), 0)
    ci = lax.broadcasted_iota(jnp.int32, (GDN_SUPER, GDN_SUPER), 1)
    same = (ri // GDN_CHUNK) == (ci // GDN_CHUNK)
    lower = jnp.where(same & (ri >= ci), 1.0, 0.0).astype(BF16)
    upper = jnp.where(same & (ri <= ci), 1.0, 0.0).astype(BF16)
    parts = _split_bf16(g, 3)
    pre = _dot(lower, parts[0]) + _dot(lower, parts[1]) + _dot(lower, parts[2])
    suf = _dot(upper, parts[0]) + _dot(upper, parts[1]) + _dot(upper, parts[2])
    o_ref[...] = jnp.where(is_g, jnp.where(is_fwd, pre, suf), beta)
    t_ref[...] = pre + suf - g


def _gdn_gates(proj, a_log, dt_bias):
    m, width = proj.shape
    col_blk = (width - LANES) // LANES
    pad = jnp.zeros((2, GDN_HEADS), F32)

    def row(v):
        return jnp.pad(jnp.stack([v.astype(F32), pad], axis=1).reshape(1, 4 * GDN_HEADS),
                       ((0, 0), (0, LANES - 4 * GDN_HEADS)))

    spec = pl.BlockSpec((GDN_SUPER, LANES), lambda i: (i, 0))
    vec = pl.BlockSpec((1, LANES), lambda i: (0, 0))
    return pl.pallas_call(
        _gdn_gate_kernel,
        grid=(m // GDN_SUPER,),
        in_specs=[pl.BlockSpec((GDN_SUPER, LANES), lambda i: (i, col_blk)), vec, vec],
        out_specs=[spec, spec],
        out_shape=[jax.ShapeDtypeStruct((m, LANES), F32), jax.ShapeDtypeStruct((m, LANES), F32)],
        compiler_params=_params("parallel"),
        name="gdn_gates",
    )(proj, row(a_log), row(dt_bias))


def _gdn_kernel(q_ref, k_ref, v_ref, z_ref, col_ref, row_ref, s0_ref, ng_ref, o_ref, sf_ref, of_s, ob_s, *, nsc, hps):
    n = GDN_SUPER
    ri = lax.broadcasted_iota(jnp.int32, (n, n), 0)
    ci = lax.broadcasted_iota(jnp.int32, (n, n), 1)
    b64 = (ri >> 6) == (ci >> 6)
    b32 = (ri >> 5) == (ci >> 5)
    b16 = (ri >> 4) == (ci >> 4)
    eye = jnp.where(ri == ci, 1.0, 0.0)
    causal = (b64 & (ri >= ci), b64 & (ri <= ci))
    strict = (b64 & (ri > ci), b64 & (ri < ci))
    nchunk = n // GDN_CHUNK

    def mm(a, b):
        return _dot(a.astype(BF16), b.astype(BF16))

    def lockstep(chains, states):
        cs = range(len(chains))
        dirs = [d for (_, d, _) in chains]
        lanes = [slice(hh * LANES, (hh + 1) * LANES) for (hh, _, _) in chains]
        q = [q_ref[0, sc, :, lanes[c]] for c, (_, _, sc) in enumerate(chains)]
        k = [k_ref[0, sc, :, lanes[c]] for c, (_, _, sc) in enumerate(chains)]
        v = [v_ref[0, sc, :, lanes[c]] for c, (_, _, sc) in enumerate(chains)]
        col = [col_ref[hh, sc] for (hh, _, sc) in chains]
        row = [row_ref[hh, sc] for (hh, _, sc) in chains]
        gcol = [col[c][:, 3 * dirs[c]:3 * dirs[c] + 1] for c in cs]
        bcol = [col[c][:, 3 * dirs[c] + 1:3 * dirs[c] + 2] for c in cs]
        tcol = [col[c][:, 3 * dirs[c] + 2:3 * dirs[c] + 3] for c in cs]
        grow = [row[c][3 * dirs[c]:3 * dirs[c] + 1, :] for c in cs]
        trow = [row[c][3 * dirs[c] + 2:3 * dirs[c] + 3, :] for c in cs]
        decay = [jnp.exp(jnp.where(causal[dirs[c]], gcol[c] - grow[c], -jnp.inf)) for c in cs]
        k16 = [k[c].astype(BF16) for c in cs]
        kb = [k[c] * bcol[c] for c in cs]
        kk = [_dot_nt(kb[c].astype(BF16), k16[c]) for c in cs]
        qk = [_dot_nt(q[c].astype(BF16), k16[c]) for c in cs]
        lmat = [jnp.where(strict[dirs[c]], kk[c] * decay[c], 0.0) for c in cs]
        intra = [(qk[c] * decay[c]).astype(BF16) for c in cs]
        l0 = [jnp.where(b16, lmat[c], 0.0) for c in cs]
        inv = [eye - l0[c] for c in cs]
        pw = [mm(l0[c], l0[c]) for c in cs]
        for rep in range(3):
            inv = [inv[c] + mm(inv[c], pw[c]) for c in cs]
            if rep < 2:
                pw = [mm(pw[c], pw[c]) for c in cs]
        for keep in (b32 & (~b16), ~b32):
            cm = [jnp.where(keep, lmat[c], 0.0) for c in cs]
            t = [mm(inv[c], cm[c]) for c in cs]
            inv = [inv[c] - mm(t[c], inv[c]) for c in cs]
        eg = [jnp.exp(gcol[c]) for c in cs]
        uw = [mm(inv[c], jnp.concatenate([v[c] * bcol[c], kb[c] * eg[c]], axis=1)) for c in cs]
        u = [uw[c][:, :LANES] for c in cs]
        w = [uw[c][:, LANES:].astype(BF16) for c in cs]
        qd = [(q[c] * eg[c]).astype(BF16) for c in cs]
        kdt = [jnp.transpose(k[c] * jnp.exp(tcol[c] - gcol[c])).astype(BF16) for c in cs]
        outs = [[None] * nchunk for _ in cs]
        states = list(states)
        for step in range(nchunk):
            js = [step if dirs[c] == 0 else nchunk - 1 - step for c in cs]
            rs = [slice(j * GDN_CHUNK, (j + 1) * GDN_CHUNK) for j in js]
            s16 = [states[c].astype(BF16) for c in cs]
            vn16 = [(u[c][rs[c]] - _dot(w[c][rs[c]], s16[c])).astype(BF16) for c in cs]
            qs = [_dot(qd[c][rs[c]], s16[c]) for c in cs]
            for c in cs:
                outs[c][js[c]] = qs[c] + _dot(intra[c][rs[c], rs[c]], vn16[c])
            states = [states[c] * jnp.exp(trow[c][:, js[c] * GDN_CHUNK:js[c] * GDN_CHUNK + 1])
                      + _dot(kdt[c][:, rs[c]], vn16[c]) for c in cs]
        return [jnp.concatenate(outs[c], axis=0) for c in cs], states

    def body(i, carry):
        chains = [(hh, d, i if d == 0 else nsc - 1 - i) for hh in range(hps) for d in range(2)]
        outs, states = lockstep(chains, carry)
        for c, (hh, d, sc) in enumerate(chains):
            if d == 0:
                of_s[hh, sc] = outs[c]
            else:
                ob_s[hh, sc] = outs[c]
        return tuple(states)

    init = tuple(s0_ref[0, d, hh] for hh in range(hps) for d in range(2))
    fin = body(0, init) if nsc == 1 else lax.fori_loop(0, nsc, body, init)
    for hh in range(hps):
        sf_ref[0, 0, hh] = fin[2 * hh]
        sf_ref[0, 1, hh] = fin[2 * hh + 1]
        lanes = slice(hh * LANES, (hh + 1) * LANES)
        for sc in range(nsc):
            o = _rms(of_s[hh, sc] + ob_s[hh, sc], ng_ref[...])
            o_ref[0, sc, :, lanes] = (o * _silu(z_ref[0, sc, :, lanes])).astype(BF16)


def _gdn_scan(qkv, proj, cols, rows, s0, norm_g, batch, seq):
    nsc = seq // GDN_SUPER
    hps = 4 if nsc == 1 else 2
    h = GDN_HEADS
    qkv4 = qkv.reshape(batch, nsc, GDN_SUPER, 3 * h * LANES)
    proj4 = proj.reshape(batch, nsc, GDN_SUPER, proj.shape[-1])

    def tile(off):
        return pl.BlockSpec((1, nsc, GDN_SUPER, hps * LANES), lambda b, j: (b, 0, 0, off // hps + j))

    st = pl.BlockSpec((1, 2, hps, GDN_DK, LANES), lambda b, j: (b, 0, j, 0, 0))
    return pl.pallas_call(
        functools.partial(_gdn_kernel, nsc=nsc, hps=hps),
        grid=(batch, h // hps),
        in_specs=[tile(0), tile(h), tile(2 * h), tile(3 * h),
                  pl.BlockSpec((hps, nsc, GDN_SUPER, 8), lambda b, j: (j, b, 0, 0)),
                  pl.BlockSpec((hps, nsc, 8, GDN_SUPER), lambda b, j: (j, b, 0, 0)),
                  st, pl.BlockSpec((1, LANES), lambda b, j: (0, 0))],
        out_specs=[tile(0), st],
        out_shape=[jax.ShapeDtypeStruct((batch, nsc, GDN_SUPER, h * LANES), BF16),
                   jax.ShapeDtypeStruct((batch, 2, h, GDN_DK, LANES), F32)],
        scratch_shapes=[pltpu.VMEM((hps, nsc, GDN_SUPER, LANES), F32),
                        pltpu.VMEM((hps, nsc, GDN_SUPER, LANES), F32)],
        compiler_params=_params("parallel", "parallel"),
        name="gdn_scan",
    )(qkv4, qkv4, qkv4, proj4, cols, rows, s0, norm_g.reshape(1, LANES))


def _gdn_mixer(x, g, mod, rows_per_mod, batch, seq, s0, w_cat, conv_w, a_log, dt_bias, norm_g):
    proj = _nmm(x, g, mod, w_cat, rows_per_mod, 0, 1)
    qkv = _gdn_prep(proj.reshape(batch, seq, proj.shape[-1]), conv_w)
    gates, tot = _gdn_gates(proj, a_log, dt_bias)
    h = GDN_HEADS
    n = batch * seq
    per = jnp.stack([gates[:, 0:h], gates[:, h:2 * h], tot[:, 0:h],
                     gates[:, 2 * h:3 * h], gates[:, 3 * h:4 * h], tot[:, 2 * h:3 * h],
                     jnp.zeros((n, h), F32), jnp.zeros((n, h), F32)], axis=-1)
    per = per.reshape(n // GDN_SUPER, GDN_SUPER, h, 8)
    cols = jnp.transpose(per, (2, 0, 1, 3))
    rows = jnp.transpose(per, (2, 0, 3, 1))
    o, s_fin = _gdn_scan(qkv, proj, cols, rows, s0, norm_g, batch, seq)
    return o.reshape(n, D_MODEL), s_fin


SUBLANES = 8


TILE_PITCH = 9


def _store_token_tiles(ref, x, pitch=SUBLANES):
    rows = x.shape[0]
    for j in range(SUBLANES):
        ref[pl.ds(j, rows, stride=pitch), :] = x[:, j * LANES:(j + 1) * LANES]


def _load_token_tiles(ref, rows, pitch=SUBLANES):
    return jnp.concatenate([ref[pl.ds(j, rows, stride=pitch), :] for j in range(SUBLANES)], axis=1)


def _moe_pre_kernel(x_ref, g_ref, mod_ref, wr_ref, h_ref, aff_ref):
    h = _norm_mod(x_ref[...], g_ref[...], mod_ref, 3, 4)
    _store_token_tiles(h_ref, h)
    h1, h2 = _split_bf16(h, 2)
    logits = _dot_nt(wr_ref[0], h1) + (_dot_nt(wr_ref[1], h1) + _dot_nt(wr_ref[0], h2))
    e = jnp.exp(logits - jnp.max(logits, axis=0, keepdims=True))
    aff_ref[...] = e / jnp.sum(e, axis=0, keepdims=True)


def _moe_pre(x, g, mod, rows_per_mod, w_router, tm=512):
    m = x.shape[0]
    w_router = w_router.T
    w1 = w_router.astype(BF16)
    w2 = (w_router - w1.astype(F32)).astype(BF16)
    return pl.pallas_call(
        _moe_pre_kernel,
        grid=(m // tm,),
        in_specs=[pl.BlockSpec((tm, D_MODEL), lambda i: (i, 0)),
                  pl.BlockSpec((1, D_MODEL), lambda i: (0, 0)),
                  pl.BlockSpec((1, 6, D_MODEL), lambda i: ((i * tm) // rows_per_mod, 0, 0)),
                  pl.BlockSpec((2, N_EXPERTS, D_MODEL), lambda i: (0, 0, 0))],
        out_specs=[pl.BlockSpec((tm * SUBLANES, LANES), lambda i: (i, 0)),
                   pl.BlockSpec((N_EXPERTS, tm), lambda i: (0, i))],
        out_shape=[jax.ShapeDtypeStruct((m * SUBLANES, LANES), F32), jax.ShapeDtypeStruct((N_EXPERTS, m), F32)],
        compiler_params=_params("parallel"),
        name="moe_router",
    )(x, g.reshape(1, D_MODEL), mod, jnp.stack([w1, w2]))


def _route_kernel(aff_ref, gat_ref, sct_ref, gsel_ref, cnt_ref, pos_ref, mask_s, rank_s, *, cap):
    e = pl.program_id(1)
    nb = aff_ref.shape[1]

    def count(m):
        return jnp.sum(jnp.sum(jnp.where(m, 1.0, 0.0), axis=1, keepdims=True), axis=0, keepdims=True)

    li = lax.broadcasted_iota(jnp.int32, (LANES, LANES), 0)
    lj = lax.broadcasted_iota(jnp.int32, (LANES, LANES), 1)
    upper = jnp.where(li <= lj, 1.0, 0.0).astype(BF16)
    bi = lax.broadcasted_iota(jnp.int32, (nb, nb), 0)
    bj = lax.broadcasted_iota(jnp.int32, (nb, nb), 1)
    before = jnp.where(bj < bi, 1.0, 0.0).astype(BF16)

    def prefix(m):
        inside = _dot(m.astype(BF16), upper)
        total = jnp.broadcast_to(inside[:, LANES - 1:LANES], inside.shape)
        return inside, _dot(before, total.astype(BF16))

    def split64(x):
        hi = jnp.floor(x * (1.0 / 64.0))
        return hi, x - 64.0 * hi

    @pl.when(pl.program_id(0) == 0)
    def _():
        bits = pltpu.bitcast(aff_ref[0], jnp.int32)

        def bisect(_, lohi):
            lo, hi = lohi
            mid = lo + ((hi - lo) >> 1)
            ok = count(bits >= mid) >= cap
            return jnp.where(ok, mid, lo), jnp.where(ok, hi, mid)

        lo, _ = lax.fori_loop(0, 31, bisect,
                              (jnp.zeros((1, 1), jnp.int32), jnp.full((1, 1), 0x7F800000, jnp.int32)))
        above = bits > lo
        tied = bits == lo
        need = cap - count(above)
        tin, tbase = prefix(jnp.where(tied, 1.0, 0.0))
        mask = jnp.where(above | (tied & (tin + tbase <= need)), 1.0, 0.0)
        mask_s[e] = mask

        @pl.when(e == 0)
        def _():
            rank_s[...] = jnp.zeros_like(rank_s)

        rank_s[...] += mask

    @pl.when(pl.program_id(0) == 1)
    def _():
        @pl.when(e == 0)
        def _():
            cnt = rank_s[...]
            inside = _dot(cnt.astype(BF16), upper)
            hi, lo = split64(jnp.broadcast_to(inside[:, LANES - 1:LANES], inside.shape))
            earlier = 64.0 * _dot(before, hi.astype(BF16)) + _dot(before, lo.astype(BF16))
            cnt_ref[...] = cnt.astype(jnp.int32)
            pos_ref[...] = (inside + earlier - cnt).astype(jnp.int32)
            rank_s[...] = jnp.zeros_like(rank_s)

        a = aff_ref[0]
        mask = mask_s[e]
        rank = rank_s[...]
        pos = pos_ref[...].astype(F32)
        inside, base = prefix(mask)
        block_end = jnp.transpose(base + jnp.broadcast_to(inside[:, LANES - 1:LANES], base.shape))[0:1, :]
        slot = lax.broadcasted_iota(jnp.int32, (cap, 1), 0).astype(F32)
        blk = jnp.sum(jnp.where(block_end <= slot, 1.0, 0.0), axis=1, keepdims=True)
        pick = jnp.where(lax.broadcasted_iota(jnp.int32, (cap, nb), 1).astype(F32) == blk, 1.0, 0.0).astype(BF16)
        base_hi, base_lo = split64(base)
        pos_hi = jnp.floor(pos * (1.0 / 4096.0))
        pos_mid, pos_lo = split64(pos - 4096.0 * pos_hi)
        a1, a2, a3 = _split_bf16(a, 3)
        table = jnp.concatenate([inside.astype(BF16), base_hi.astype(BF16), base_lo.astype(BF16), rank.astype(BF16),
                                 a1, a2, a3, pos_hi.astype(BF16), pos_mid.astype(BF16), pos_lo.astype(BF16)],
                                axis=1)
        got = _dot(pick, table)

        def part(j):
            return got[:, j * LANES:(j + 1) * LANES]

        target = slot - (64.0 * part(1)[:, 0:1] + part(2)[:, 0:1])
        lane = jnp.sum(jnp.where(part(0) <= target, 1.0, 0.0), axis=1, keepdims=True)
        here = lax.broadcasted_iota(jnp.int32, (cap, LANES), 1).astype(F32) == lane

        def at_token(x):
            return jnp.sum(jnp.where(here, x, 0.0), axis=1, keepdims=True)

        dest = at_token(part(3)) + at_token(4096.0 * part(7) + 64.0 * part(8) + part(9))
        gat_ref[0] = ((LANES * blk + lane) * SUBLANES).astype(jnp.int32)
        sct_ref[0] = (dest * SUBLANES).astype(jnp.int32)
        gsel_ref[0] = at_token((part(4) + part(5)) + part(6))
        rank_s[...] = rank + mask


def _route(aff3, cap):
    n_exp, nb, _ = aff3.shape
    col = pl.BlockSpec((1, cap, 1), lambda p, e: (p * e, 0, 0))
    tok = pl.BlockSpec((nb, LANES), lambda p, e: (0, 0))
    return pl.pallas_call(
        functools.partial(_route_kernel, cap=cap),
        grid=(2, n_exp),
        in_specs=[pl.BlockSpec((1, nb, LANES), lambda p, e: (e, 0, 0))],
        out_specs=[col, col, col, tok, tok],
        out_shape=[jax.ShapeDtypeStruct((n_exp, cap, 1), jnp.int32), jax.ShapeDtypeStruct((n_exp, cap, 1), jnp.int32),
                   jax.ShapeDtypeStruct((n_exp, cap, 1), F32),
                   jax.ShapeDtypeStruct((nb, LANES), jnp.int32), jax.ShapeDtypeStruct((nb, LANES), jnp.int32)],
        scratch_shapes=[pltpu.VMEM((n_exp, nb, LANES), F32), pltpu.VMEM((nb, LANES), F32)],
        compiler_params=_params("arbitrary", "arbitrary"),
        name="moe_route",
    )(aff3)


def _ffn_kernel(g0_ref, gat_ref, sct_ref, gate_ref, wg_ref, wu_ref, wd_ref, h_hbm, z_hbm,
                xb0, xb1, yb0, yb1, wg_s, wu_s, wd_s, g_s, s_s, gsem, ssem, psem, *, tc, pairs_per_e, npairs):
    m = pl.program_id(0)
    hb = (m % 2) * (3 * tc)
    nb = 3 * tc - hb
    last = npairs - 1

    def tile_in(off, buf, i, sem):
        return pltpu.make_async_copy(h_hbm.at[pl.ds(pl.multiple_of(off, SUBLANES), SUBLANES)],
                                     buf.at[pl.ds(i * TILE_PITCH, SUBLANES)], sem)

    def tile_out(off, buf, i, sem):
        return pltpu.make_async_copy(buf.at[pl.ds(i * TILE_PITCH, SUBLANES)],
                                     z_hbm.at[pl.ds(pl.multiple_of(off, SUBLANES), SUBLANES)], sem)

    def tables(row, base):
        return (pltpu.make_async_copy(gat_ref.at[row], g_s.at[pl.ds(base, 2 * tc)], psem.at[0]),
                pltpu.make_async_copy(sct_ref.at[row], s_s.at[pl.ds(base, 3 * tc)], psem.at[1]))

    def wait_tiles(copy):
        for _ in range(tc):
            copy.wait()

    @pl.when(m == 0)
    def _():
        yb1[...] = jnp.zeros_like(yb1)
        first = pltpu.make_async_copy(g0_ref.at[0], g_s.at[pl.ds(0, tc)], psem.at[0])
        first.start()
        first.wait()
        for i in range(tc):
            tile_in(g_s[i], xb0, i, gsem.at[0]).start()
        for cp in tables(0, 0):
            cp.start()

    for cp in tables(m, hb):
        cp.wait()
    for cp in tables(jnp.minimum(m + 1, last), nb):
        cp.start()

    @pl.when(m % pairs_per_e == 0)
    def _():
        wg_s[...] = wg_ref[0, 0].astype(BF16)
        wu_s[...] = wu_ref[0, 0].astype(BF16)
        wd_s[...] = wd_ref[0, 0].astype(BF16)

    def ffn(xb, gate):
        x = _load_token_tiles(xb, tc, TILE_PITCH).astype(BF16)
        half = wg_s.shape[1] // 2
        acc = None
        for c0 in (0, half):
            a = _dot(x, wg_s[:, c0:c0 + half])
            b = _dot(x, wu_s[:, c0:c0 + half])
            part = _dot((_silu(a) * b).astype(BF16), wd_s[c0:c0 + half, :])
            acc = part if acc is None else acc + part
        return acc * gate

    wait_tiles(tile_in(0, xb0, 0, gsem.at[0]))
    for i in range(tc):
        tile_in(g_s[hb + i], xb1, i, gsem.at[1]).start()
        tile_out(s_s[hb + i], yb1, i, ssem.at[1]).start()
    y = ffn(xb0, gate_ref[0, 0:tc])

    @pl.when(m > 0)
    def _():
        wait_tiles(tile_out(0, yb0, 0, ssem.at[0]))

    _store_token_tiles(yb0, y, TILE_PITCH)

    wait_tiles(tile_in(0, xb1, 0, gsem.at[1]))
    for i in range(tc):
        tile_in(g_s[hb + tc + i], xb0, i, gsem.at[0]).start()
        tile_out(s_s[hb + tc + i], yb0, i, ssem.at[0]).start()
    y = ffn(xb1, gate_ref[0, tc:2 * tc])
    wait_tiles(tile_out(0, yb1, 0, ssem.at[1]))
    _store_token_tiles(yb1, y, TILE_PITCH)

    @pl.when(m == last)
    def _():
        for i in range(tc):
            tile_out(s_s[hb + 2 * tc + i], yb1, i, ssem.at[1]).start()
        wait_tiles(tile_out(0, yb1, 0, ssem.at[1]))
        wait_tiles(tile_out(0, yb0, 0, ssem.at[0]))
        wait_tiles(tile_in(0, xb0, 0, gsem.at[0]))
        for cp in tables(last, nb):
            cp.wait()


def _expert_ffn(h_tiles, gat, sct, gate, w_gate, w_up, w_down, layer, z_rows, tc=512):
    n_exp, cap = gat.shape
    ff = w_gate.shape[-1]
    tc = min(tc, cap // 2)
    pairs_per_e = cap // (2 * tc)
    npairs = n_exp * pairs_per_e
    nsteps = 2 * npairs
    gat = gat.reshape(nsteps, tc)
    sct = sct.reshape(nsteps, tc)
    spare = ((z_rows - tc + jnp.arange(tc, dtype=jnp.int32)) * SUBLANES)[None]
    gat_next = jnp.concatenate([gat[1:], gat[:1]], axis=0).reshape(npairs, 2 * tc)
    sct_prev = jnp.concatenate([spare, sct[:-1]], axis=0).reshape(npairs, 2 * tc)
    sct3 = jnp.concatenate([sct_prev, sct[1::2]], axis=1)
    tile_buf = pltpu.VMEM((tc * TILE_PITCH, LANES), F32)
    wspec_in = pl.BlockSpec((1, 1, D_MODEL, ff), lambda m: (layer, m // pairs_per_e, 0, 0))
    return pl.pallas_call(
        functools.partial(_ffn_kernel, tc=tc, pairs_per_e=pairs_per_e, npairs=npairs),
        grid=(npairs,),
        in_specs=[pl.BlockSpec((1, tc), lambda m: (0, 0)),
                  pl.BlockSpec((npairs, 2 * tc), lambda m: (0, 0)),
                  pl.BlockSpec((npairs, 3 * tc), lambda m: (0, 0)),
                  pl.BlockSpec((1, 2 * tc, 1), lambda m: (m, 0, 0)),
                  wspec_in, wspec_in,
                  pl.BlockSpec((1, 1, ff, D_MODEL), lambda m: (layer, m // pairs_per_e, 0, 0)),
                  pl.BlockSpec(memory_space=pl.ANY)],
        out_specs=pl.BlockSpec(memory_space=pl.ANY),
        out_shape=jax.ShapeDtypeStruct((z_rows * SUBLANES, LANES), F32),
        scratch_shapes=[tile_buf, tile_buf, tile_buf, tile_buf,
                        pltpu.VMEM((D_MODEL, ff), BF16), pltpu.VMEM((D_MODEL, ff), BF16),
                        pltpu.VMEM((ff, D_MODEL), BF16),
                        pltpu.SMEM((6 * tc,), jnp.int32), pltpu.SMEM((6 * tc,), jnp.int32),
                        pltpu.SemaphoreType.DMA((2,)), pltpu.SemaphoreType.DMA((2,)),
                        pltpu.SemaphoreType.DMA((2,))],
        compiler_params=_params("arbitrary"),
        name="expert_ffn",
    )(gat[:1], gat_next, sct3, gate.reshape(npairs, 2 * tc, 1), w_gate, w_up, w_down, h_tiles)


COMBINE_BUFFERS = 3


def _combine_kernel(st_ref, x_ref, cnt_ref, pos_ref, mod_ref, fg_ref, y_hbm, o_ref, ybuf, acc, sem,
                    *, gi, final, tt, rc):
    i = pl.program_id(0)
    first = st_ref[i]
    nchunk = (st_ref[i + 1] - first + (rc - 1)) // rc

    def chunk(c, sl):
        row0 = pl.multiple_of((first + c * rc) * SUBLANES, SUBLANES)
        return pltpu.make_async_copy(y_hbm.at[pl.ds(row0, rc * SUBLANES)], ybuf.at[sl], sem.at[sl])

    for c0 in range(COMBINE_BUFFERS - 1):
        @pl.when(c0 < nchunk)
        def _():
            chunk(c0, c0).start()

    acc[...] = jnp.zeros_like(acc)
    begin = pos_ref[...] - first
    end = begin + cnt_ref[...]
    row = lax.broadcasted_iota(jnp.int32, (tt, rc), 1)

    def body(c, carry):
        sl = c % COMBINE_BUFFERS
        ahead = c + (COMBINE_BUFFERS - 1)

        @pl.when(ahead < nchunk)
        def _():
            chunk(ahead, ahead % COMBINE_BUFFERS).start()

        chunk(c, sl).wait()
        r = row + c * rc
        seg = jnp.where((r >= begin) & (r < end), 1.0, 0.0).astype(BF16)
        hi, lo = _split_bf16(_load_token_tiles(ybuf.at[sl], rc), 2)
        acc[...] += _dot(seg, hi) + _dot(seg, lo)
        return carry

    lax.fori_loop(0, nchunk, body, 0)
    x = x_ref[...] + mod_ref[0, gi:gi + 1, :] * acc[...]
    if final:
        x = _rms(x, fg_ref[...])
    o_ref[...] = x


def _combine(x, y, cnt, pos, mod, rows_per_mod, gi, final_g, final, tt, rc):
    m = x.shape[0]
    starts = jnp.concatenate([pos[::tt], (pos[-1:] + cnt[-1:])])
    col = pl.BlockSpec((tt, 1), lambda i, st: (i, 0))
    return pl.pallas_call(
        functools.partial(_combine_kernel, gi=gi, final=final, tt=tt, rc=rc),
        grid_spec=pltpu.PrefetchScalarGridSpec(
            num_scalar_prefetch=1,
            grid=(m // tt,),
            in_specs=[pl.BlockSpec((tt, D_MODEL), lambda i, st: (i, 0)), col, col,
                      pl.BlockSpec((1, 6, D_MODEL), lambda i, st: ((i * tt) // rows_per_mod, 0, 0)),
                      pl.BlockSpec((1, D_MODEL), lambda i, st: (0, 0)),
                      pl.BlockSpec(memory_space=pl.ANY)],
            out_specs=pl.BlockSpec((tt, D_MODEL), lambda i, st: (i, 0)),
            scratch_shapes=[pltpu.VMEM((COMBINE_BUFFERS, rc * SUBLANES, LANES), F32), pltpu.VMEM((tt, D_MODEL), F32),
                            pltpu.SemaphoreType.DMA((COMBINE_BUFFERS,))]),
        out_shape=jax.ShapeDtypeStruct((m, D_MODEL), F32),
        compiler_params=_params("arbitrary"),
        name="moe_combine",
    )(starts, x, cnt.reshape(m, 1), pos.reshape(m, 1), mod, final_g.reshape(1, D_MODEL), y)


def _expert_choice(x, g, mod, rows_per_mod, w_router, w_gate, w_up, w_down, layer, gi, final_g, final):
    n = x.shape[0]
    cap = EC_CAPACITY_FACTOR * n // N_EXPERTS
    tt = min(512, n)
    h, aff = _moe_pre(x, g, mod, rows_per_mod, w_router)
    gat, sct, gate, cnt, pos = _route(aff.reshape(N_EXPERTS, n // LANES, LANES), cap)
    gat, sct, gate = (a.reshape(N_EXPERTS, cap) for a in (gat, sct, gate))
    tc = min(512, cap // 2)
    y = _expert_ffn(h, gat, sct, gate, w_gate, w_up, w_down, layer, N_EXPERTS * cap + tc, tc)
    return _combine(x, y, cnt.reshape(n), pos.reshape(n), mod, rows_per_mod, gi, final_g, final, tt, tc)


def _trunk(x, mods, rows_per_mod, batch, seq, caches, p):
    depth = p["mod_w"].shape[0]
    n = batch * seq
    new = {}
    for layer in range(depth):
        kind, slot = layer % 4, layer // 4
        mod = mods[layer]
        g_mix = p["norm_mix_g"][layer]
        if kind == 0:
            w_ab = jnp.pad(p["gdn_w_ab"][slot], ((0, 0), (0, LANES - 4 * GDN_HEADS)))
            w_cat = jnp.concatenate([p["gdn_w_in"][slot], w_ab], axis=1).astype(BF16)
            if caches is None:
                s0 = jnp.zeros((batch, 2, GDN_HEADS, GDN_DK, LANES), F32)
            else:
                s0 = caches["state_gdn"][:, slot]
            o, s_fin = _gdn_mixer(x, g_mix, mod, rows_per_mod, batch, seq, s0, w_cat, p["gdn_conv_w"][slot],
                                  p["gdn_a_log"][slot], p["gdn_dt_bias"][slot], p["gdn_norm_g"][slot])
            new.setdefault("gdn", []).append(s_fin)
            w_out = p["gdn_w_out"][slot]
        elif kind == 1:
            proj = _nmm(x, g_mix, mod, p["diff_w_in"][slot].astype(BF16), rows_per_mod, 0, 1)
            nk = seq
            if caches is None:
                assert DIFF_HEADS == SUBLANES
                q, k, v, kc, vc = _attn_prep(proj, "diff", seq, False)
                new.setdefault("diff_k", []).append(kc.reshape(batch, seq, DIFF_HEADS, LANES))
                new.setdefault("diff_v", []).append(vc.reshape(batch, seq, DIFF_HEADS, LANES))
            else:
                q, k, v = _attn_prep(proj, "diff", seq, True)
                kc = caches["cache_diff_k"][:, slot].reshape(batch, -1, DIFF_HEADS * LANES).astype(BF16)
                vc = caches["cache_diff_v"][:, slot].reshape(batch, -1, DIFF_HEADS * LANES).astype(BF16)
                nk = seq + kc.shape[1]
                k = jnp.concatenate([k.reshape(batch, seq, -1), kc], axis=1).reshape(batch * nk, -1)
                v = jnp.concatenate([v.reshape(batch, seq, -1), vc], axis=1).reshape(batch * nk, -1)
            lam_init = 0.8 - 0.6 * math.exp(-0.3 * layer)
            o = _attention(q, k, v, "diff", batch, seq, nk,
                           (p["diff_lambda"][slot], p["diff_subln_g"][slot]), lam_init)
            w_out = p["diff_w_out"][slot]
        elif kind == 2:
            proj = _nmm(x, g_mix, mod, p["gqa_w_in"][slot].astype(BF16), rows_per_mod, 0, 1)
            q, k, v, kf = _attn_prep(proj, "gqa", seq, caches is not None,
                                     (p["gqa_q_norm_g"][slot], p["gqa_k_norm_g"][slot]))
            nk = seq
            if caches is None:
                v0 = (GQA_HEADS + GQA_KV) * LANES
                new.setdefault("gqa_k", []).append(kf.reshape(batch, seq, GQA_KV, GQA_HD))
                new.setdefault("gqa_v", []).append(proj[:, v0:].reshape(batch, seq, GQA_KV, GQA_HD))
            else:
                kc = caches["cache_gqa_k"][:, slot].reshape(batch, -1, GQA_KV * GQA_HD).astype(BF16)
                vc = caches["cache_gqa_v"][:, slot].reshape(batch, -1, GQA_KV * GQA_HD).astype(BF16)
                nk = seq + kc.shape[1]
                k = jnp.concatenate([k.reshape(batch, seq, -1), kc], axis=1).reshape(batch * nk, -1)
                v = jnp.concatenate([v.reshape(batch, seq, -1), vc], axis=1).reshape(batch * nk, -1)
            o = _attention(q, k, v, "gqa", batch, seq, nk)
            w_out = p["gqa_w_out"][slot]
        else:
            proj = _nmm(x, g_mix, mod, p["swa_w_in"][slot].astype(BF16), rows_per_mod, 0, 1)
            q, k, v = _attn_prep(proj, "swa", seq, caches is not None)
            sinks = p["swa_sinks"][slot].astype(F32)
            if caches is None:
                k0 = SWA_HEADS * SWA_HD
                v0 = k0 + SWA_KV * SWA_HD
                new.setdefault("swa_k", []).append(proj[:, k0:v0].reshape(batch, seq, SWA_KV, SWA_HD))
                new.setdefault("swa_v", []).append(proj[:, v0:].reshape(batch, seq, SWA_KV, SWA_HD))
                o = _attention(q, k, v, "swa", batch, seq, seq, (sinks,))
            else:
                kc = jnp.pad(caches["cache_swa_k"][:, slot], ((0, 0), (0, 0), (0, 0), (0, LANES - SWA_HD)))
                past = kc.shape[1]
                kc = kc.reshape(batch, past, SWA_KV * LANES).astype(BF16)
                vc = caches["cache_swa_v"][:, slot].reshape(batch, past, SWA_KV * SWA_HD).astype(BF16)
                nk = seq + past
                k = jnp.concatenate([k.reshape(batch, seq, -1), kc], axis=1).reshape(batch * nk, -1)
                v = jnp.concatenate([v.reshape(batch, seq, -1), vc], axis=1).reshape(batch * nk, -1)
                o = _band_attention(q, k, v, sinks, batch, seq, past)
            w_out = p["swa_w_out"][slot]
        x = _mm_resid(o, w_out.astype(BF16), x, mod, rows_per_mod, 2)
        x = _expert_choice(x, p["norm_ffn_g"][layer], mod, rows_per_mod, p["moe_router"][layer],
                           p["moe_w_gate"], p["moe_w_up"], p["moe_w_down"], layer,
                           5, p["final_g"], layer == depth - 1)
    return x, new


def kernel(x_prompt, x_sample, state_gdn, cache_diff_k, cache_diff_v, cache_gqa_k, cache_gqa_v, cache_swa_k, cache_swa_v, c, c_ctx, mod_w, mod_b, norm_mix_g, norm_ffn_g, final_g, gdn_w_in, gdn_conv_w, gdn_w_ab, gdn_a_log, gdn_dt_bias, gdn_norm_g, gdn_w_out, diff_w_in, diff_lambda, diff_subln_g, diff_w_out, gqa_w_in, gqa_q_norm_g, gqa_k_norm_g, gqa_w_out, swa_w_in, swa_sinks, swa_w_out, moe_router, moe_w_gate, moe_w_up, moe_w_down):
    p = dict(mod_w=mod_w, mod_b=mod_b, norm_mix_g=norm_mix_g, norm_ffn_g=norm_ffn_g, final_g=final_g,
             gdn_w_in=gdn_w_in, gdn_conv_w=gdn_conv_w, gdn_w_ab=gdn_w_ab, gdn_a_log=gdn_a_log,
             gdn_dt_bias=gdn_dt_bias, gdn_norm_g=gdn_norm_g, gdn_w_out=gdn_w_out,
             diff_w_in=diff_w_in, diff_lambda=diff_lambda, diff_subln_g=diff_subln_g, diff_w_out=diff_w_out,
             gqa_w_in=gqa_w_in, gqa_q_norm_g=gqa_q_norm_g, gqa_k_norm_g=gqa_k_norm_g, gqa_w_out=gqa_w_out,
             swa_w_in=swa_w_in, swa_sinks=swa_sinks, swa_w_out=swa_w_out,
             moe_router=moe_router, moe_w_gate=moe_w_gate, moe_w_up=moe_w_up, moe_w_down=moe_w_down)
    batch, seq, _ = x_prompt.shape
    dec_batch, dec_seq, _ = x_sample.shape
    depth = mod_w.shape[0]
    rows = 8 * ((1 + dec_batch + 7) // 8)
    cond = jnp.zeros((rows, D_MODEL), F32).at[0].set(c_ctx).at[1:1 + dec_batch].set(c)
    mod_all = _mod_proj(cond, mod_w, mod_b).reshape(depth, rows, 6, D_MODEL)
    mods_ctx = [mod_all[l, 0:1] for l in range(depth)]
    mods_lat = [mod_all[l, 1:1 + dec_batch] for l in range(depth)]

    y_prompt, st = _trunk(x_prompt.reshape(batch * seq, D_MODEL), mods_ctx, batch * seq, batch, seq, None, p)
    caches = dict(state_gdn=state_gdn, cache_diff_k=cache_diff_k, cache_diff_v=cache_diff_v,
                  cache_gqa_k=cache_gqa_k, cache_gqa_v=cache_gqa_v,
                  cache_swa_k=cache_swa_k, cache_swa_v=cache_swa_v)
    y_sample, _ = _trunk(x_sample.reshape(dec_batch * dec_seq, D_MODEL), mods_lat, dec_seq, dec_batch, dec_seq,
                         caches, p)

    def stack(xs):
        return jnp.stack(xs, axis=1)

    return (y_prompt.reshape(batch, seq, D_MODEL), y_sample.reshape(dec_batch, dec_seq, D_MODEL),
            stack(st["gdn"]), stack(st["diff_k"]), stack(st["diff_v"]),
            stack(st["gqa_k"]), stack(st["gqa_v"]), stack(st["swa_k"]), stack(st["swa_v"]))
```

```python
import functools
import math

import jax
import jax.numpy as jnp
from jax import lax
from jax.experimental import pallas as pl
from jax.experimental.pallas import tpu as pltpu

F32 = jnp.float32
BF16 = jnp.bfloat16

D_MODEL = 1024
GRID_W = 64
ROPE_THETA = 10000.0
NORM_EPS = 1e-6
LANES = 128
GDN_HEADS = 8
GDN_DK = 128
GDN_CHUNK = 64
GDN_SUPER = 256
DIFF_HEADS = 8
DIFF_DH = 64
GQA_HEADS = 8
GQA_KV = 2
GQA_HD = 128
SWA_HEADS = 16
SWA_KV = 4
SWA_HD = 64
WINDOW = 128
N_EXPERTS = 16
EC_CAPACITY_FACTOR = 2
VMEM_LIMIT_BYTES = 56 * 1024 * 1024


def _params(*sem):
    return pltpu.CompilerParams(dimension_semantics=sem, vmem_limit_bytes=VMEM_LIMIT_BYTES)


def _silu(x):
    return x / (1.0 + jnp.exp(-x))


def _rms(x, g):
    return x * lax.rsqrt(jnp.mean(x * x, axis=-1, keepdims=True) + NORM_EPS) * g


def _dot(a, b):
    return jnp.dot(a, b, preferred_element_type=F32)


def _dot_nt(a, b):
    return lax.dot_general(a, b, (((1,), (1,)), ((), ())), preferred_element_type=F32)


def _split_bf16(x, parts):
    out = []
    for _ in range(parts - 1):
        hi = x.astype(BF16)
        out.append(hi)
        x = x - hi.astype(F32)
    out.append(x.astype(BF16))
    return out


def _mod_kernel(c_ref, w_ref, b_ref, o_ref):
    a = _silu(c_ref[...]).astype(BF16)
    o_ref[0] = _dot(a, w_ref[0].astype(BF16)) + b_ref[0]


def _mod_proj(cond, mod_w, mod_b):
    depth, _, width = mod_w.shape
    rows = cond.shape[0]
    tn = 1536
    return pl.pallas_call(
        _mod_kernel,
        grid=(depth, width // tn),
        in_specs=[pl.BlockSpec((rows, D_MODEL), lambda l, j: (0, 0)),
                  pl.BlockSpec((1, D_MODEL, tn), lambda l, j: (l, 0, j)),
                  pl.BlockSpec((1, 1, tn), lambda l, j: (l, 0, j))],
        out_specs=pl.BlockSpec((1, rows, tn), lambda l, j: (l, 0, j)),
        out_shape=jax.ShapeDtypeStruct((depth, rows, width), F32),
        compiler_params=_params("parallel", "parallel"),
        name="mod_proj",
    )(cond, mod_w, mod_b.reshape(depth, 1, width))


def _norm_mod(x, g, mod_ref, sh, sc):
    return _rms(x, g) * (1.0 + mod_ref[0, sc:sc + 1, :]) + mod_ref[0, sh:sh + 1, :]


def _nmm_kernel(x_ref, g_ref, mod_ref, w_ref, o_ref, *, sh, sc, chunk):
    h = _norm_mod(x_ref[...], g_ref[...], mod_ref, sh, sc).astype(BF16)
    width = w_ref.shape[1]
    for c0 in range(0, width, chunk):
        c1 = min(c0 + chunk, width)
        o_ref[:, c0:c1] = _dot(h, w_ref[:, c0:c1])


def _nmm(x, g, mod, w, rows_per_mod, sh, sc, tm=256):
    m = x.shape[0]
    width = w.shape[1]
    return pl.pallas_call(
        functools.partial(_nmm_kernel, sh=sh, sc=sc, chunk=512),
        grid=(m // tm,),
        in_specs=[pl.BlockSpec((tm, D_MODEL), lambda i: (i, 0)),
                  pl.BlockSpec((1, D_MODEL), lambda i: (0, 0)),
                  pl.BlockSpec((1, 6, D_MODEL), lambda i: ((i * tm) // rows_per_mod, 0, 0)),
                  pl.BlockSpec((D_MODEL, width), lambda i: (0, 0))],
        out_specs=pl.BlockSpec((tm, width), lambda i: (i, 0)),
        out_shape=jax.ShapeDtypeStruct((m, width), F32),
        compiler_params=_params("parallel"),
        name="norm_mod_matmul",
    )(x, g.reshape(1, D_MODEL), mod, w)


def _mmr_kernel(a_ref, w_ref, r_ref, mod_ref, o_ref, *, gi):
    acc = _dot(a_ref[...].astype(BF16), w_ref[...])
    o_ref[...] = r_ref[...] + mod_ref[0, gi:gi + 1, :] * acc


def _mm_resid(a, w, res, mod, rows_per_mod, gi, tm=512):
    m = a.shape[0]
    return pl.pallas_call(
        functools.partial(_mmr_kernel, gi=gi),
        grid=(m // tm,),
        in_specs=[pl.BlockSpec((tm, D_MODEL), lambda i: (i, 0)),
                  pl.BlockSpec((D_MODEL, D_MODEL), lambda i: (0, 0)),
                  pl.BlockSpec((tm, D_MODEL), lambda i: (i, 0)),
                  pl.BlockSpec((1, 6, D_MODEL), lambda i: ((i * tm) // rows_per_mod, 0, 0))],
        out_specs=pl.BlockSpec((tm, D_MODEL), lambda i: (i, 0)),
        out_shape=jax.ShapeDtypeStruct((m, D_MODEL), F32),
        compiler_params=_params("parallel"),
        name="out_proj_resid",
    )(a, w, res, mod)


def _rope_tables(n_tokens, dim):
    rows = n_tokens // GRID_W
    row = jnp.broadcast_to(jnp.arange(rows, dtype=F32)[:, None], (rows, GRID_W)).reshape(-1)
    col = jnp.broadcast_to(jnp.arange(GRID_W, dtype=F32)[None, :], (rows, GRID_W)).reshape(-1)
    n_freq = dim // 4
    inv = ROPE_THETA ** (-jnp.arange(n_freq, dtype=F32) / n_freq)
    ang = jnp.concatenate([row[:, None] * inv, col[:, None] * inv], axis=-1)
    cos, sin = jnp.cos(ang), jnp.sin(ang)
    zero = jnp.zeros_like(sin)
    reps = LANES // dim
    cos_t = jnp.tile(jnp.concatenate([cos, cos], axis=-1), (1, reps))
    sin_a = jnp.tile(jnp.concatenate([-sin, zero], axis=-1), (1, reps))
    sin_b = jnp.tile(jnp.concatenate([zero, sin], axis=-1), (1, reps))
    return cos_t, sin_a, sin_b


def _rope(x, cos, sin_a, sin_b, dim):
    half = dim // 2
    return x * cos + pltpu.roll(x, LANES - half, 1) * sin_a + pltpu.roll(x, half, 1) * sin_b


def _prep_kernel(*refs, kind, rope):
    if kind == "gqa":
        p_ref, cos_ref, sa_ref, sb_ref, qg_ref, kg_ref, q_out, k_out, v_out, kf_out = refs
    elif kind == "diff" and not rope:
        p_ref, cos_ref, sa_ref, sb_ref, q_out, k_out, v_out, kc_out, vc_out = refs
        hw = DIFF_HEADS * LANES
        _store_token_tiles(kc_out, p_ref[:, hw:2 * hw])
        _store_token_tiles(vc_out, p_ref[:, 2 * hw:3 * hw])
    else:
        p_ref, cos_ref, sa_ref, sb_ref, q_out, k_out, v_out = refs
    cos, sa, sb = cos_ref[...], sa_ref[...], sb_ref[...]

    def blk(j):
        return p_ref[:, j * LANES:(j + 1) * LANES]

    if kind == "gqa":
        scale = GQA_HD ** -0.5 * LOG2E
        for j in range(GQA_HEADS):
            y = _rms(blk(j), qg_ref[...])
            if rope:
                y = _rope(y, cos, sa, sb, GQA_HD)
            q_out[:, j * LANES:(j + 1) * LANES] = (y * scale).astype(BF16)
        for j in range(GQA_KV):
            y = _rms(blk(GQA_HEADS + j), kg_ref[...])
            kf_out[:, j * LANES:(j + 1) * LANES] = y
            if rope:
                y = _rope(y, cos, sa, sb, GQA_HD)
            k_out[:, j * LANES:(j + 1) * LANES] = y.astype(BF16)
        v0 = (GQA_HEADS + GQA_KV) * LANES
        v_out[...] = p_ref[:, v0:v0 + GQA_KV * LANES].astype(BF16)
    elif kind == "diff":
        scale = DIFF_DH ** -0.5 * LOG2E
        for j in range(DIFF_HEADS):
            y = blk(j)
            if rope:
                y = _rope(y, cos, sa, sb, DIFF_DH)
            q_out[:, j * LANES:(j + 1) * LANES] = (y * scale).astype(BF16)
            y = blk(DIFF_HEADS + j)
            if rope:
                y = _rope(y, cos, sa, sb, DIFF_DH)
            k_out[:, j * LANES:(j + 1) * LANES] = y.astype(BF16)
        v_out[...] = p_ref[:, 2 * DIFF_HEADS * LANES:3 * DIFF_HEADS * LANES].astype(BF16)
    else:
        scale = SWA_HD ** -0.5 * LOG2E
        low = lax.broadcasted_iota(jnp.int32, cos.shape, 1) < SWA_HD

        def pad_pair(y, out, j):
            out[:, (2 * j) * LANES:(2 * j + 1) * LANES] = jnp.where(low, y, 0.0).astype(BF16)
            out[:, (2 * j + 1) * LANES:(2 * j + 2) * LANES] = jnp.where(
                low, pltpu.roll(y, SWA_HD, 1), 0.0).astype(BF16)

        nq = SWA_HEADS // 2
        for j in range(nq):
            y = blk(j)
            if rope:
                y = _rope(y, cos, sa, sb, SWA_HD)
            pad_pair(y * scale, q_out, j)
        for j in range(SWA_KV // 2):
            y = blk(nq + j)
            if rope:
                y = _rope(y, cos, sa, sb, SWA_HD)
            pad_pair(y, k_out, j)
        v0 = (nq + SWA_KV // 2) * LANES
        v_out[...] = p_ref[:, v0:v0 + (SWA_KV // 2) * LANES].astype(BF16)


def _attn_prep(proj, kind, seq, rope, gains=None, tm=256):
    m, width = proj.shape
    dim = GQA_HD if kind == "gqa" else DIFF_DH
    cos, sa, sb = _rope_tables(seq, dim)
    nt = seq // tm
    if kind == "gqa":
        wq, wk, wv = GQA_HEADS * LANES, GQA_KV * LANES, GQA_KV * LANES
    elif kind == "diff":
        wq, wk, wv = DIFF_HEADS * LANES, DIFF_HEADS * LANES, DIFF_HEADS * LANES
    else:
        wq, wk, wv = SWA_HEADS * LANES, SWA_KV * LANES, (SWA_KV // 2) * LANES
    tab = pl.BlockSpec((tm, LANES), lambda i: (i % nt, 0))
    in_specs = [pl.BlockSpec((tm, width), lambda i: (i, 0)), tab, tab, tab]
    args = [proj, cos, sa, sb]
    out_shapes = [jax.ShapeDtypeStruct((m, wq), BF16), jax.ShapeDtypeStruct((m, wk), BF16),
                  jax.ShapeDtypeStruct((m, wv), BF16)]
    out_specs = [pl.BlockSpec((tm, wq), lambda i: (i, 0)), pl.BlockSpec((tm, wk), lambda i: (i, 0)),
                 pl.BlockSpec((tm, wv), lambda i: (i, 0))]
    if kind == "gqa":
        gspec = pl.BlockSpec((1, LANES), lambda i: (0, 0))
        in_specs += [gspec, gspec]
        args += [gains[0].reshape(1, LANES), gains[1].reshape(1, LANES)]
        out_shapes.append(jax.ShapeDtypeStruct((m, wk), F32))
        out_specs.append(pl.BlockSpec((tm, wk), lambda i: (i, 0)))
    elif kind == "diff" and not rope:
        out_shapes += [jax.ShapeDtypeStruct((m * SUBLANES, LANES), F32)] * 2
        out_specs += [pl.BlockSpec((tm * SUBLANES, LANES), lambda i: (i, 0))] * 2
    return pl.pallas_call(
        functools.partial(_prep_kernel, kind=kind, rope=rope),
        grid=(m // tm,),
        in_specs=in_specs,
        out_specs=out_specs,
        out_shape=out_shapes,
        compiler_params=_params("parallel"),
        name="attn_prep_" + kind,
    )(*args)


LOG2E = 1.4426950408889634


def _with_ones(v):
    return jnp.concatenate([v, jnp.ones_like(v)], axis=1)


def _softmax_pv(s, mx, v_ones):
    e = jnp.exp2(s - mx).astype(BF16)
    acc = _dot(e, v_ones)
    return acc[:, :LANES], acc[:, LANES:LANES + 1]


def _pick_half(o_a, o_b, upper):
    low = lax.broadcasted_iota(jnp.int32, o_a.shape, 1) < SWA_HD
    if upper:
        return jnp.where(low, pltpu.roll(o_a, SWA_HD, 1), o_b)
    return jnp.where(low, o_a, pltpu.roll(o_b, SWA_HD, 1))


def _attn_kernel(*refs, mode, lam_init):
    if mode == "gqa":
        q_ref, k_ref, v_ref, o_ref = refs
    elif mode == "diff":
        q_ref, k_ref, v_ref, lam_ref, g_ref, o_ref = refs
    else:
        q_ref, k_ref, v_ref, sink_ref, o_ref = refs
    v = _with_ones(v_ref[...])
    if mode == "gqa":
        k = k_ref[...]
        for j in range(GQA_HEADS // GQA_KV):
            s = _dot_nt(q_ref[:, j * LANES:(j + 1) * LANES], k)
            num, den = _softmax_pv(s, jnp.max(s, axis=-1, keepdims=True), v)
            o_ref[:, j * LANES:(j + 1) * LANES] = (num / den).astype(BF16)
    elif mode == "diff":
        k = k_ref[...]
        q = q_ref[...]
        low = lax.broadcasted_iota(jnp.int32, q.shape, 1) < DIFF_DH
        lv = lam_ref[...]
        lam = (jnp.exp(jnp.sum(lv[0:1] * lv[1:2], axis=-1, keepdims=True))
               - jnp.exp(jnp.sum(lv[2:3] * lv[3:4], axis=-1, keepdims=True)) + lam_init)
        s0 = _dot_nt(jnp.where(low, q, 0.0).astype(BF16), k)
        s1 = _dot_nt(jnp.where(low, 0.0, q).astype(BF16), k)
        n0, d0 = _softmax_pv(s0, jnp.max(s0, axis=-1, keepdims=True), v)
        n1, d1 = _softmax_pv(s1, jnp.max(s1, axis=-1, keepdims=True), v)
        o = n0 / d0 - n1 * (lam / d1)
        o_ref[...] = (_rms(o, g_ref[...]) * (1.0 - lam_init)).astype(BF16)
    else:
        heads = SWA_HEADS // (SWA_KV // 2)
        group = SWA_HEADS // SWA_KV
        outs = []
        for j in range(heads):
            a = j // group
            s = _dot_nt(q_ref[:, j * LANES:(j + 1) * LANES], k_ref[:, a * LANES:(a + 1) * LANES])
            sink = sink_ref[0, :, j:j + 1] * LOG2E
            mx = jnp.maximum(jnp.max(s, axis=-1, keepdims=True), sink)
            num, den = _softmax_pv(s, mx, v)
            outs.append(num / (den + jnp.exp2(sink - mx)))
        for jj in range(heads // 2):
            a = (2 * jj) // group
            o_ref[:, jj * LANES:(jj + 1) * LANES] = _pick_half(outs[2 * jj], outs[2 * jj + 1], a == 1).astype(BF16)


def _attention(q, k, v, mode, batch, seq, n_keys, extra=(), lam_init=0.0, tq=512):
    tq = min(tq, seq)
    nq = seq // tq
    if mode == "gqa":
        groups, wq, wk, wv, wo = GQA_KV, (GQA_HEADS // GQA_KV) * LANES, LANES, LANES, (GQA_HEADS // GQA_KV) * LANES
    elif mode == "diff":
        groups, wq, wk, wv, wo = DIFF_HEADS, LANES, LANES, LANES, LANES
    else:
        groups, wq, wk, wv, wo = SWA_KV // 2, SWA_HEADS // 2 * LANES, 2 * LANES, LANES, 4 * LANES
    in_specs = [pl.BlockSpec((tq, wq), lambda b, g, i: (b * nq + i, g)),
                pl.BlockSpec((n_keys, wk), lambda b, g, i: (b, g)),
                pl.BlockSpec((n_keys, wv), lambda b, g, i: (b, g))]
    args = [q, k, v]
    if mode == "diff":
        lam_vecs, subln_g = extra
        in_specs += [pl.BlockSpec((4, DIFF_DH), lambda b, g, i: (0, 0)),
                     pl.BlockSpec((1, LANES), lambda b, g, i: (0, 0))]
        args += [lam_vecs, subln_g.reshape(1, LANES)]
    elif mode == "swa":
        (sinks,) = extra
        in_specs.append(pl.BlockSpec((1, 1, SWA_HEADS // 2), lambda b, g, i: (g, 0, 0)))
        args.append(sinks.reshape(2, 1, SWA_HEADS // 2))
    return pl.pallas_call(
        functools.partial(_attn_kernel, mode=mode, lam_init=lam_init),
        grid=(batch, groups, nq),
        in_specs=in_specs,
        out_specs=pl.BlockSpec((tq, wo), lambda b, g, i: (b * nq + i, g)),
        out_shape=jax.ShapeDtypeStruct((batch * seq, D_MODEL), BF16),
        compiler_params=_params("parallel", "parallel", "parallel"),
        name="attn_" + mode,
    )(*args)


def _band_kernel(q_ref, kp_ref, kc_ref, kn_ref, kx_ref, vp_ref, vc_ref, vn_ref, vx_ref, sink_ref, o_ref, *, nqb):
    i = pl.program_id(2)
    qi = lax.broadcasted_iota(jnp.int32, (WINDOW, WINDOW), 0)
    kj = lax.broadcasted_iota(jnp.int32, (WINDOW, WINDOW), 1)
    open_blk = jnp.zeros((WINDOW, WINDOW), F32)
    bias = jnp.concatenate([jnp.where((kj >= qi) & (i > 0), 0.0, -jnp.inf), open_blk,
                            jnp.where((kj <= qi) & (i < nqb - 1), 0.0, -jnp.inf),
                            jnp.zeros((WINDOW, kx_ref.shape[0]), F32)], axis=1)
    keys = jnp.concatenate([kp_ref[...], kc_ref[...], kn_ref[...], kx_ref[...]], axis=0)
    vals = _with_ones(jnp.concatenate([vp_ref[...], vc_ref[...], vn_ref[...], vx_ref[...]], axis=0))
    heads = SWA_HEADS // (SWA_KV // 2)
    group = SWA_HEADS // SWA_KV
    outs = []
    for j in range(heads):
        a = j // group
        s = _dot_nt(q_ref[:, j * LANES:(j + 1) * LANES], keys[:, a * LANES:(a + 1) * LANES]) + bias
        sink = sink_ref[0, :, j:j + 1] * LOG2E
        mx = jnp.maximum(jnp.max(s, axis=-1, keepdims=True), sink)
        num, den = _softmax_pv(s, mx, vals)
        outs.append(num / (den + jnp.exp2(sink - mx)))
    for jj in range(heads // 2):
        a = (2 * jj) // group
        o_ref[:, jj * LANES:(jj + 1) * LANES] = _pick_half(outs[2 * jj], outs[2 * jj + 1], a == 1).astype(BF16)


def _band_attention(q, k, v, sinks, batch, seq, past):
    nqb = seq // WINDOW
    nkb = (seq + past) // WINDOW
    pairs = SWA_KV // 2
    xb = past // WINDOW
    assert past % WINDOW == 0 and seq % past == 0

    def band(shift, width):
        return pl.BlockSpec((WINDOW, width),
                            lambda b, g, i: (b * nkb + jnp.clip(i + shift, 0, nqb - 1), g))

    def ctx(width):
        return pl.BlockSpec((past, width), lambda b, g, i: (b * (nkb // xb) + seq // past, g))

    return pl.pallas_call(
        functools.partial(_band_kernel, nqb=nqb),
        grid=(batch, pairs, nqb),
        in_specs=[pl.BlockSpec((WINDOW, SWA_HEADS // 2 * LANES), lambda b, g, i: (b * nqb + i, g)),
                  band(-1, 2 * LANES), band(0, 2 * LANES), band(1, 2 * LANES), ctx(2 * LANES),
                  band(-1, LANES), band(0, LANES), band(1, LANES), ctx(LANES),
                  pl.BlockSpec((1, 1, SWA_HEADS // 2), lambda b, g, i: (g, 0, 0))],
        out_specs=pl.BlockSpec((WINDOW, 4 * LANES), lambda b, g, i: (b * nqb + i, g)),
        out_shape=jax.ShapeDtypeStruct((batch * seq, D_MODEL), BF16),
        compiler_params=_params("parallel", "parallel", "parallel"),
        name="attn_band",
    )(q, k, k, k, k, v, v, v, v, sinks.reshape(2, 1, SWA_HEADS // 2))


def _gdn_prep_kernel(p_ref, w_ref, o_ref):
    x = p_ref[0]
    t = x.shape[0]
    w = w_ref[...]
    row = lax.broadcasted_iota(jnp.int32, x.shape, 0)
    xm = jnp.where(row == 0, 0.0, pltpu.roll(x, 1, 0))
    xp = jnp.where(row == t - 1, 0.0, pltpu.roll(x, t - 1, 0))
    y = _silu(xm * w[0:1] + x * w[1:2] + xp * w[2:3])
    rs = lax.rsqrt(jnp.sum(y * y, axis=-1, keepdims=True) + NORM_EPS)
    j = pl.program_id(1)
    is_q = (j < GDN_HEADS).astype(F32)
    normed = (j < 2 * GDN_HEADS).astype(F32)
    scale = is_q * (GDN_DK ** -0.5) + (1.0 - is_q)
    o_ref[0] = y * (rs * (scale * normed) + (1.0 - normed))


def _gdn_prep(proj3, conv_w):
    batch, seq, _ = proj3.shape
    nblk = 3 * GDN_HEADS
    return pl.pallas_call(
        _gdn_prep_kernel,
        grid=(batch, nblk),
        in_specs=[pl.BlockSpec((1, seq, LANES), lambda b, j: (b, 0, j)),
                  pl.BlockSpec((3, LANES), lambda b, j: (0, j))],
        out_specs=pl.BlockSpec((1, seq, LANES), lambda b, j: (b, 0, j)),
        out_shape=jax.ShapeDtypeStruct((batch, seq, nblk * LANES), F32),
        compiler_params=_params("parallel", "parallel"),
        name="gdn_conv",
    )(proj3, conv_w)


def _gdn_gate_kernel(p_ref, al_ref, dtb_ref, o_ref, t_ref):
    ab = p_ref[...]
    lane = lax.broadcasted_iota(jnp.int32, ab.shape, 1)
    is_g = (lane < 4 * GDN_HEADS) & ((lane & GDN_HEADS) == 0)
    is_fwd = lane < 2 * GDN_HEADS
    x = ab + dtb_ref[...]
    softplus = jnp.maximum(x, 0.0) + jnp.log(1.0 + jnp.exp(-jnp.abs(x)))
    g = jnp.where(is_g, -jnp.exp(al_ref[...]) * softplus, 0.0)
    beta = 1.0 / (1.0 + jnp.exp(-ab))
    ri = lax.broadcasted_iota(jnp.int32, (GDN_SUPER, GDN_SUPER), 0)
    ci = lax.broadcasted_iota(jnp.int32, (GDN_SUPER, GDN_SUPER), 1)
    same = (ri // GDN_CHUNK) == (ci // GDN_CHUNK)
    lower = jnp.where(same & (ri >= ci), 1.0, 0.0).astype(BF16)
    upper = jnp.where(same & (ri <= ci), 1.0, 0.0).astype(BF16)
    parts = _split_bf16(g, 3)
    pre = _dot(lower, parts[0]) + _dot(lower, parts[1]) + _dot(lower, parts[2])
    suf = _dot(upper, parts[0]) + _dot(upper, parts[1]) + _dot(upper, parts[2])
    o_ref[...] = jnp.where(is_g, jnp.where(is_fwd, pre, suf), beta)
    t_ref[...] = pre + suf - g


def _gdn_gates(proj, a_log, dt_bias):
    m, width = proj.shape
    col_blk = (width - LANES) // LANES
    pad = jnp.zeros((2, GDN_HEADS), F32)

    def row(v):
        return jnp.pad(jnp.stack([v.astype(F32), pad], axis=1).reshape(1, 4 * GDN_HEADS),
                       ((0, 0), (0, LANES - 4 * GDN_HEADS)))

    spec = pl.BlockSpec((GDN_SUPER, LANES), lambda i: (i, 0))
    vec = pl.BlockSpec((1, LANES), lambda i: (0, 0))
    return pl.pallas_call(
        _gdn_gate_kernel,
        grid=(m // GDN_SUPER,),
        in_specs=[pl.BlockSpec((GDN_SUPER, LANES), lambda i: (i, col_blk)), vec, vec],
        out_specs=[spec, spec],
        out_shape=[jax.ShapeDtypeStruct((m, LANES), F32), jax.ShapeDtypeStruct((m, LANES), F32)],
        compiler_params=_params("parallel"),
        name="gdn_gates",
    )(proj, row(a_log), row(dt_bias))


def _gdn_kernel(q_ref, k_ref, v_ref, z_ref, col_ref, row_ref, s0_ref, ng_ref, o_ref, sf_ref, of_s, ob_s, *, nsc, hps):
    n = GDN_SUPER
    ri = lax.broadcasted_iota(jnp.int32, (n, n), 0)
    ci = lax.broadcasted_iota(jnp.int32, (n, n), 1)
    b64 = (ri >> 6) == (ci >> 6)
    b32 = (ri >> 5) == (ci >> 5)
    b16 = (ri >> 4) == (ci >> 4)
    eye = jnp.where(ri == ci, 1.0, 0.0)
    causal = (b64 & (ri >= ci), b64 & (ri <= ci))
    strict = (b64 & (ri > ci), b64 & (ri < ci))
    nchunk = n // GDN_CHUNK

    def mm(a, b):
        return _dot(a.astype(BF16), b.astype(BF16))

    def lockstep(chains, states):
        cs = range(len(chains))
        dirs = [d for (_, d, _) in chains]
        lanes = [slice(hh * LANES, (hh + 1) * LANES) for (hh, _, _) in chains]
        q = [q_ref[0, sc, :, lanes[c]] for c, (_, _, sc) in enumerate(chains)]
        k = [k_ref[0, sc, :, lanes[c]] for c, (_, _, sc) in enumerate(chains)]
        v = [v_ref[0, sc, :, lanes[c]] for c, (_, _, sc) in enumerate(chains)]
        col = [col_ref[hh, sc] for (hh, _, sc) in chains]
        row = [row_ref[hh, sc] for (hh, _, sc) in chains]
        gcol = [col[c][:, 3 * dirs[c]:3 * dirs[c] + 1] for c in cs]
        bcol = [col[c][:, 3 * dirs[c] + 1:3 * dirs[c] + 2] for c in cs]
        tcol = [col[c][:, 3 * dirs[c] + 2:3 * dirs[c] + 3] for c in cs]
        grow = [row[c][3 * dirs[c]:3 * dirs[c] + 1, :] for c in cs]
        trow = [row[c][3 * dirs[c] + 2:3 * dirs[c] + 3, :] for c in cs]
        decay = [jnp.exp(jnp.where(causal[dirs[c]], gcol[c] - grow[c], -jnp.inf)) for c in cs]
        k16 = [k[c].astype(BF16) for c in cs]
        kb = [k[c] * bcol[c] for c in cs]
        kk = [_dot_nt(kb[c].astype(BF16), k16[c]) for c in cs]
        qk = [_dot_nt(q[c].astype(BF16), k16[c]) for c in cs]
        lmat = [jnp.where(strict[dirs[c]], kk[c] * decay[c], 0.0) for c in cs]
        intra = [(qk[c] * decay[c]).astype(BF16) for c in cs]
        l0 = [jnp.where(b16, lmat[c], 0.0) for c in cs]
        inv = [eye - l0[c] for c in cs]
        pw = [mm(l0[c], l0[c]) for c in cs]
        for rep in range(3):
            inv = [inv[c] + mm(inv[c], pw[c]) for c in cs]
            if rep < 2:
                pw = [mm(pw[c], pw[c]) for c in cs]
        for keep in (b32 & (~b16), ~b32):
            cm = [jnp.where(keep, lmat[c], 0.0) for c in cs]
            t = [mm(inv[c], cm[c]) for c in cs]
            inv = [inv[c] - mm(t[c], inv[c]) for c in cs]
        eg = [jnp.exp(gcol[c]) for c in cs]
        uw = [mm(inv[c], jnp.concatenate([v[c] * bcol[c], kb[c] * eg[c]], axis=1)) for c in cs]
        u = [uw[c][:, :LANES] for c in cs]
        w = [uw[c][:, LANES:].astype(BF16) for c in cs]
        qd = [(q[c] * eg[c]).astype(BF16) for c in cs]
        kdt = [jnp.transpose(k[c] * jnp.exp(tcol[c] - gcol[c])).astype(BF16) for c in cs]
        outs = [[None] * nchunk for _ in cs]
        states = list(states)
        for step in range(nchunk):
            js = [step if dirs[c] == 0 else nchunk - 1 - step for c in cs]
            rs = [slice(j * GDN_CHUNK, (j + 1) * GDN_CHUNK) for j in js]
            s16 = [states[c].astype(BF16) for c in cs]
            vn16 = [(u[c][rs[c]] - _dot(w[c][rs[c]], s16[c])).astype(BF16) for c in cs]
            qs = [_dot(qd[c][rs[c]], s16[c]) for c in cs]
            for c in cs:
                outs[c][js[c]] = qs[c] + _dot(intra[c][rs[c], rs[c]], vn16[c])
            states = [states[c] * jnp.exp(trow[c][:, js[c] * GDN_CHUNK:js[c] * GDN_CHUNK + 1])
                      + _dot(kdt[c][:, rs[c]], vn16[c]) for c in cs]
        return [jnp.concatenate(outs[c], axis=0) for c in cs], states

    def body(i, carry):
        chains = [(hh, d, i if d == 0 else nsc - 1 - i) for hh in range(hps) for d in range(2)]
        outs, states = lockstep(chains, carry)
        for c, (hh, d, sc) in enumerate(chains):
            if d == 0:
                of_s[hh, sc] = outs[c]
            else:
                ob_s[hh, sc] = outs[c]
        return tuple(states)

    init = tuple(s0_ref[0, d, hh] for hh in range(hps) for d in range(2))
    fin = body(0, init) if nsc == 1 else lax.fori_loop(0, nsc, body, init)
    for hh in range(hps):
        sf_ref[0, 0, hh] = fin[2 * hh]
        sf_ref[0, 1, hh] = fin[2 * hh + 1]
        lanes = slice(hh * LANES, (hh + 1) * LANES)
        for sc in range(nsc):
            o = _rms(of_s[hh, sc] + ob_s[hh, sc], ng_ref[...])
            o_ref[0, sc, :, lanes] = (o * _silu(z_ref[0, sc, :, lanes])).astype(BF16)


def _gdn_scan(qkv, proj, cols, rows, s0, norm_g, batch, seq):
    nsc = seq // GDN_SUPER
    hps = 4 if nsc == 1 else 2
    h = GDN_HEADS
    qkv4 = qkv.reshape(batch, nsc, GDN_SUPER, 3 * h * LANES)
    proj4 = proj.reshape(batch, nsc, GDN_SUPER, proj.shape[-1])

    def tile(off):
        return pl.BlockSpec((1, nsc, GDN_SUPER, hps * LANES), lambda b, j: (b, 0, 0, off // hps + j))

    st = pl.BlockSpec((1, 2, hps, GDN_DK, LANES), lambda b, j: (b, 0, j, 0, 0))
    return pl.pallas_call(
        functools.partial(_gdn_kernel, nsc=nsc, hps=hps),
        grid=(batch, h // hps),
        in_specs=[tile(0), tile(h), tile(2 * h), tile(3 * h),
                  pl.BlockSpec((hps, nsc, GDN_SUPER, 8), lambda b, j: (j, b, 0, 0)),
                  pl.BlockSpec((hps, nsc, 8, GDN_SUPER), lambda b, j: (j, b, 0, 0)),
                  st, pl.BlockSpec((1, LANES), lambda b, j: (0, 0))],
        out_specs=[tile(0), st],
        out_shape=[jax.ShapeDtypeStruct((batch, nsc, GDN_SUPER, h * LANES), BF16),
                   jax.ShapeDtypeStruct((batch, 2, h, GDN_DK, LANES), F32)],
        scratch_shapes=[pltpu.VMEM((hps, nsc, GDN_SUPER, LANES), F32),
                        pltpu.VMEM((hps, nsc, GDN_SUPER, LANES), F32)],
        compiler_params=_params("parallel", "parallel"),
        name="gdn_scan",
    )(qkv4, qkv4, qkv4, proj4, cols, rows, s0, norm_g.reshape(1, LANES))


def _gdn_mixer(x, g, mod, rows_per_mod, batch, seq, s0, w_cat, conv_w, a_log, dt_bias, norm_g):
    proj = _nmm(x, g, mod, w_cat, rows_per_mod, 0, 1)
    qkv = _gdn_prep(proj.reshape(batch, seq, proj.shape[-1]), conv_w)
    gates, tot = _gdn_gates(proj, a_log, dt_bias)
    h = GDN_HEADS
    n = batch * seq
    per = jnp.stack([gates[:, 0:h], gates[:, h:2 * h], tot[:, 0:h],
                     gates[:, 2 * h:3 * h], gates[:, 3 * h:4 * h], tot[:, 2 * h:3 * h],
                     jnp.zeros((n, h), F32), jnp.zeros((n, h), F32)], axis=-1)
    per = per.reshape(n // GDN_SUPER, GDN_SUPER, h, 8)
    cols = jnp.transpose(per, (2, 0, 1, 3))
    rows = jnp.transpose(per, (2, 0, 3, 1))
    o, s_fin = _gdn_scan(qkv, proj, cols, rows, s0, norm_g, batch, seq)
    return o.reshape(n, D_MODEL), s_fin


SUBLANES = 8


TILE_PITCH = 9


def _store_token_tiles(ref, x, pitch=SUBLANES):
    rows = x.shape[0]
    for j in range(SUBLANES):
        ref[pl.ds(j, rows, stride=pitch), :] = x[:, j * LANES:(j + 1) * LANES]


def _load_token_tiles(ref, rows, pitch=SUBLANES):
    return jnp.concatenate([ref[pl.ds(j, rows, stride=pitch), :] for j in range(SUBLANES)], axis=1)


def _moe_pre_kernel(x_ref, g_ref, mod_ref, wr_ref, h_ref, aff_ref):
    h = _norm_mod(x_ref[...], g_ref[...], mod_ref, 3, 4)
    _store_token_tiles(h_ref, h)
    h1, h2 = _split_bf16(h, 2)
    logits = _dot_nt(wr_ref[0], h1) + (_dot_nt(wr_ref[1], h1) + _dot_nt(wr_ref[0], h2))
    e = jnp.exp(logits - jnp.max(logits, axis=0, keepdims=True))
    aff_ref[...] = e / jnp.sum(e, axis=0, keepdims=True)


def _moe_pre(x, g, mod, rows_per_mod, w_router, tm=512):
    m = x.shape[0]
    w_router = w_router.T
    w1 = w_router.astype(BF16)
    w2 = (w_router - w1.astype(F32)).astype(BF16)
    return pl.pallas_call(
        _moe_pre_kernel,
        grid=(m // tm,),
        in_specs=[pl.BlockSpec((tm, D_MODEL), lambda i: (i, 0)),
                  pl.BlockSpec((1, D_MODEL), lambda i: (0, 0)),
                  pl.BlockSpec((1, 6, D_MODEL), lambda i: ((i * tm) // rows_per_mod, 0, 0)),
                  pl.BlockSpec((2, N_EXPERTS, D_MODEL), lambda i: (0, 0, 0))],
        out_specs=[pl.BlockSpec((tm * SUBLANES, LANES), lambda i: (i, 0)),
                   pl.BlockSpec((N_EXPERTS, tm), lambda i: (0, i))],
        out_shape=[jax.ShapeDtypeStruct((m * SUBLANES, LANES), F32), jax.ShapeDtypeStruct((N_EXPERTS, m), F32)],
        compiler_params=_params("parallel"),
        name="moe_router",
    )(x, g.reshape(1, D_MODEL), mod, jnp.stack([w1, w2]))


def _route_kernel(aff_ref, gat_ref, sct_ref, gsel_ref, cnt_ref, pos_ref, mask_s, rank_s, *, cap):
    e = pl.program_id(1)
    nb = aff_ref.shape[1]

    def count(m):
        return jnp.sum(jnp.sum(jnp.where(m, 1.0, 0.0), axis=1, keepdims=True), axis=0, keepdims=True)

    li = lax.broadcasted_iota(jnp.int32, (LANES, LANES), 0)
    lj = lax.broadcasted_iota(jnp.int32, (LANES, LANES), 1)
    upper = jnp.where(li <= lj, 1.0, 0.0).astype(BF16)
    bi = lax.broadcasted_iota(jnp.int32, (nb, nb), 0)
    bj = lax.broadcasted_iota(jnp.int32, (nb, nb), 1)
    before = jnp.where(bj < bi, 1.0, 0.0).astype(BF16)

    def prefix(m):
        inside = _dot(m.astype(BF16), upper)
        total = jnp.broadcast_to(inside[:, LANES - 1:LANES], inside.shape)
        return inside, _dot(before, total.astype(BF16))

    def split64(x):
        hi = jnp.floor(x * (1.0 / 64.0))
        return hi, x - 64.0 * hi

    @pl.when(pl.program_id(0) == 0)
    def _():
        bits = pltpu.bitcast(aff_ref[0], jnp.int32)

        def bisect(_, lohi):
            lo, hi = lohi
            mid = lo + ((hi - lo) >> 1)
            ok = count(bits >= mid) >= cap
            return jnp.where(ok, mid, lo), jnp.where(ok, hi, mid)

        lo, _ = lax.fori_loop(0, 31, bisect,
                              (jnp.zeros((1, 1), jnp.int32), jnp.full((1, 1), 0x7F800000, jnp.int32)))
        above = bits > lo
        tied = bits == lo
        need = cap - count(above)
        tin, tbase = prefix(jnp.where(tied, 1.0, 0.0))
        mask = jnp.where(above | (tied & (tin + tbase <= need)), 1.0, 0.0)
        mask_s[e] = mask

        @pl.when(e == 0)
        def _():
            rank_s[...] = jnp.zeros_like(rank_s)

        rank_s[...] += mask

    @pl.when(pl.program_id(0) == 1)
    def _():
        @pl.when(e == 0)
        def _():
            cnt = rank_s[...]
            inside = _dot(cnt.astype(BF16), upper)
            hi, lo = split64(jnp.broadcast_to(inside[:, LANES - 1:LANES], inside.shape))
            earlier = 64.0 * _dot(before, hi.astype(BF16)) + _dot(before, lo.astype(BF16))
            cnt_ref[...] = cnt.astype(jnp.int32)
            pos_ref[...] = (inside + earlier - cnt).astype(jnp.int32)
            rank_s[...] = jnp.zeros_like(rank_s)

        a = aff_ref[0]
        mask = mask_s[e]
        rank = rank_s[...]
        pos = pos_ref[...].astype(F32)
        inside, base = prefix(mask)
        block_end = jnp.transpose(base + jnp.broadcast_to(inside[:, LANES - 1:LANES], base.shape))[0:1, :]
        slot = lax.broadcasted_iota(jnp.int32, (cap, 1), 0).astype(F32)
        blk = jnp.sum(jnp.where(block_end <= slot, 1.0, 0.0), axis=1, keepdims=True)
        pick = jnp.where(lax.broadcasted_iota(jnp.int32, (cap, nb), 1).astype(F32) == blk, 1.0, 0.0).astype(BF16)
        base_hi, base_lo = split64(base)
        pos_hi = jnp.floor(pos * (1.0 / 4096.0))
        pos_mid, pos_lo = split64(pos - 4096.0 * pos_hi)
        a1, a2, a3 = _split_bf16(a, 3)
        table = jnp.concatenate([inside.astype(BF16), base_hi.astype(BF16), base_lo.astype(BF16), rank.astype(BF16),
                                 a1, a2, a3, pos_hi.astype(BF16), pos_mid.astype(BF16), pos_lo.astype(BF16)],
                                axis=1)
        got = _dot(pick, table)

        def part(j):
            return got[:, j * LANES:(j + 1) * LANES]

        target = slot - (64.0 * part(1)[:, 0:1] + part(2)[:, 0:1])
        lane = jnp.sum(jnp.where(part(0) <= target, 1.0, 0.0), axis=1, keepdims=True)
        here = lax.broadcasted_iota(jnp.int32, (cap, LANES), 1).astype(F32) == lane

        def at_token(x):
            return jnp.sum(jnp.where(here, x, 0.0), axis=1, keepdims=True)

        dest = at_token(part(3)) + at_token(4096.0 * part(7) + 64.0 * part(8) + part(9))
        gat_ref[0] = ((LANES * blk + lane) * SUBLANES).astype(jnp.int32)
        sct_ref[0] = (dest * SUBLANES).astype(jnp.int32)
        gsel_ref[0] = at_token((part(4) + part(5)) + part(6))
        rank_s[...] = rank + mask


def _route(aff3, cap):
    n_exp, nb, _ = aff3.shape
    col = pl.BlockSpec((1, cap, 1), lambda p, e: (p * e, 0, 0))
    tok = pl.BlockSpec((nb, LANES), lambda p, e: (0, 0))
    return pl.pallas_call(
        functools.partial(_route_kernel, cap=cap),
        grid=(2, n_exp),
        in_specs=[pl.BlockSpec((1, nb, LANES), lambda p, e: (e, 0, 0))],
        out_specs=[col, col, col, tok, tok],
        out_shape=[jax.ShapeDtypeStruct((n_exp, cap, 1), jnp.int32), jax.ShapeDtypeStruct((n_exp, cap, 1), jnp.int32),
                   jax.ShapeDtypeStruct((n_exp, cap, 1), F32),
                   jax.ShapeDtypeStruct((nb, LANES), jnp.int32), jax.ShapeDtypeStruct((nb, LANES), jnp.int32)],
        scratch_shapes=[pltpu.VMEM((n_exp, nb, LANES), F32), pltpu.VMEM((nb, LANES), F32)],
        compiler_params=_params("arbitrary", "arbitrary"),
        name="moe_route",
    )(aff3)


def _ffn_kernel(g0_ref, gat_ref, sct_ref, gate_ref, wg_ref, wu_ref, wd_ref, h_hbm, z_hbm,
                xb0, xb1, yb0, yb1, wg_s, wu_s, wd_s, g_s, s_s, gsem, ssem, psem, *, tc, pairs_per_e, npairs):
    m = pl.program_id(0)
    hb = (m % 2) * (3 * tc)
    nb = 3 * tc - hb
    last = npairs - 1

    def tile_in(off, buf, i, sem):
        return pltpu.make_async_copy(h_hbm.at[pl.ds(pl.multiple_of(off, SUBLANES), SUBLANES)],
                                     buf.at[pl.ds(i * TILE_PITCH, SUBLANES)], sem)

    def tile_out(off, buf, i, sem):
        return pltpu.make_async_copy(buf.at[pl.ds(i * TILE_PITCH, SUBLANES)],
                                     z_hbm.at[pl.ds(pl.multiple_of(off, SUBLANES), SUBLANES)], sem)

    def tables(row, base):
        return (pltpu.make_async_copy(gat_ref.at[row], g_s.at[pl.ds(base, 2 * tc)], psem.at[0]),
                pltpu.make_async_copy(sct_ref.at[row], s_s.at[pl.ds(base, 3 * tc)], psem.at[1]))

    def wait_tiles(copy):
        for _ in range(tc):
            copy.wait()

    @pl.when(m == 0)
    def _():
        yb1[...] = jnp.zeros_like(yb1)
        first = pltpu.make_async_copy(g0_ref.at[0], g_s.at[pl.ds(0, tc)], psem.at[0])
        first.start()
        first.wait()
        for i in range(tc):
            tile_in(g_s[i], xb0, i, gsem.at[0]).start()
        for cp in tables(0, 0):
            cp.start()

    for cp in tables(m, hb):
        cp.wait()
    for cp in tables(jnp.minimum(m + 1, last), nb):
        cp.start()

    @pl.when(m % pairs_per_e == 0)
    def _():
        wg_s[...] = wg_ref[0, 0].astype(BF16)
        wu_s[...] = wu_ref[0, 0].astype(BF16)
        wd_s[...] = wd_ref[0, 0].astype(BF16)

    def ffn(xb, gate):
        x = _load_token_tiles(xb, tc, TILE_PITCH).astype(BF16)
        half = wg_s.shape[1] // 2
        acc = None
        for c0 in (0, half):
            a = _dot(x, wg_s[:, c0:c0 + half])
            b = _dot(x, wu_s[:, c0:c0 + half])
            part = _dot((_silu(a) * b).astype(BF16), wd_s[c0:c0 + half, :])
            acc = part if acc is None else acc + part
        return acc * gate

    wait_tiles(tile_in(0, xb0, 0, gsem.at[0]))
    for i in range(tc):
        tile_in(g_s[hb + i], xb1, i, gsem.at[1]).start(priority=i % 2)
        tile_out(s_s[hb + i], yb1, i, ssem.at[1]).start(priority=i % 2)
    y = ffn(xb0, gate_ref[0, 0:tc])

    @pl.when(m > 0)
    def _():
        wait_tiles(tile_out(0, yb0, 0, ssem.at[0]))

    _store_token_tiles(yb0, y, TILE_PITCH)

    wait_tiles(tile_in(0, xb1, 0, gsem.at[1]))
    for i in range(tc):
        tile_in(g_s[hb + tc + i], xb0, i, gsem.at[0]).start(priority=i % 2)
        tile_out(s_s[hb + tc + i], yb0, i, ssem.at[0]).start(priority=i % 2)
    y = ffn(xb1, gate_ref[0, tc:2 * tc])
    wait_tiles(tile_out(0, yb1, 0, ssem.at[1]))
    _store_token_tiles(yb1, y, TILE_PITCH)

    @pl.when(m == last)
    def _():
        for i in range(tc):
            tile_out(s_s[hb + 2 * tc + i], yb1, i, ssem.at[1]).start()
        wait_tiles(tile_out(0, yb1, 0, ssem.at[1]))
        wait_tiles(tile_out(0, yb0, 0, ssem.at[0]))
        wait_tiles(tile_in(0, xb0, 0, gsem.at[0]))
        for cp in tables(last, nb):
            cp.wait()


def _expert_ffn(h_tiles, gat, sct, gate, w_gate, w_up, w_down, layer, z_rows, tc=512):
    n_exp, cap = gat.shape
    ff = w_gate.shape[-1]
    tc = min(tc, cap // 2)
    pairs_per_e = cap // (2 * tc)
    npairs = n_exp * pairs_per_e
    nsteps = 2 * npairs
    gat = gat.reshape(nsteps, tc)
    sct = sct.reshape(nsteps, tc)
    spare = ((z_rows - tc + jnp.arange(tc, dtype=jnp.int32)) * SUBLANES)[None]
    gat_next = jnp.concatenate([gat[1:], gat[:1]], axis=0).reshape(npairs, 2 * tc)
    sct_prev = jnp.concatenate([spare, sct[:-1]], axis=0).reshape(npairs, 2 * tc)
    sct3 = jnp.concatenate([sct_prev, sct[1::2]], axis=1)
    tile_buf = pltpu.VMEM((tc * TILE_PITCH, LANES), F32)
    wspec_in = pl.BlockSpec((1, 1, D_MODEL, ff), lambda m: (layer, m // pairs_per_e, 0, 0))
    return pl.pallas_call(
        functools.partial(_ffn_kernel, tc=tc, pairs_per_e=pairs_per_e, npairs=npairs),
        grid=(npairs,),
        in_specs=[pl.BlockSpec((1, tc), lambda m: (0, 0)),
                  pl.BlockSpec((npairs, 2 * tc), lambda m: (0, 0)),
                  pl.BlockSpec((npairs, 3 * tc), lambda m: (0, 0)),
                  pl.BlockSpec((1, 2 * tc, 1), lambda m: (m, 0, 0)),
                  wspec_in, wspec_in,
                  pl.BlockSpec((1, 1, ff, D_MODEL), lambda m: (layer, m // pairs_per_e, 0, 0)),
                  pl.BlockSpec(memory_space=pl.ANY)],
        out_specs=pl.BlockSpec(memory_space=pl.ANY),
        out_shape=jax.ShapeDtypeStruct((z_rows * SUBLANES, LANES), F32),
        scratch_shapes=[tile_buf, tile_buf, tile_buf, tile_buf,
                        pltpu.VMEM((D_MODEL, ff), BF16), pltpu.VMEM((D_MODEL, ff), BF16),
                        pltpu.VMEM((ff, D_MODEL), BF16),
                        pltpu.SMEM((6 * tc,), jnp.int32), pltpu.SMEM((6 * tc,), jnp.int32),
                        pltpu.SemaphoreType.DMA((2,)), pltpu.SemaphoreType.DMA((2,)),
                        pltpu.SemaphoreType.DMA((2,))],
        compiler_params=_params("arbitrary"),
        name="expert_ffn",
    )(gat[:1], gat_next, sct3, gate.reshape(npairs, 2 * tc, 1), w_gate, w_up, w_down, h_tiles)


COMBINE_BUFFERS = 3


def _combine_kernel(st_ref, x_ref, cnt_ref, pos_ref, mod_ref, fg_ref, y_hbm, o_ref, ybuf, acc, sem,
                    *, gi, final, tt, rc):
    i = pl.program_id(0)
    first = st_ref[i]
    nchunk = (st_ref[i + 1] - first + (rc - 1)) // rc

    def chunk(c, sl):
        row0 = pl.multiple_of((first + c * rc) * SUBLANES, SUBLANES)
        return pltpu.make_async_copy(y_hbm.at[pl.ds(row0, rc * SUBLANES)], ybuf.at[sl], sem.at[sl])

    for c0 in range(COMBINE_BUFFERS - 1):
        @pl.when(c0 < nchunk)
        def _():
            chunk(c0, c0).start()

    acc[...] = jnp.zeros_like(acc)
    begin = pos_ref[...] - first
    end = begin + cnt_ref[...]
    row = lax.broadcasted_iota(jnp.int32, (tt, rc), 1)

    def body(c, carry):
        sl = c % COMBINE_BUFFERS
        ahead = c + (COMBINE_BUFFERS - 1)

        @pl.when(ahead < nchunk)
        def _():
            chunk(ahead, ahead % COMBINE_BUFFERS).start()

        chunk(c, sl).wait()
        r = row + c * rc
        seg = jnp.where((r >= begin) & (r < end), 1.0, 0.0).astype(BF16)
        hi, lo = _split_bf16(_load_token_tiles(ybuf.at[sl], rc), 2)
        acc[...] += _dot(seg, hi) + _dot(seg, lo)
        return carry

    lax.fori_loop(0, nchunk, body, 0)
    x = x_ref[...] + mod_ref[0, gi:gi + 1, :] * acc[...]
    if final:
        x = _rms(x, fg_ref[...])
    o_ref[...] = x


def _combine(x, y, cnt, pos, mod, rows_per_mod, gi, final_g, final, tt, rc):
    m = x.shape[0]
    starts = jnp.concatenate([pos[::tt], (pos[-1:] + cnt[-1:])])
    col = pl.BlockSpec((tt, 1), lambda i, st: (i, 0))
    return pl.pallas_call(
        functools.partial(_combine_kernel, gi=gi, final=final, tt=tt, rc=rc),
        grid_spec=pltpu.PrefetchScalarGridSpec(
            num_scalar_prefetch=1,
            grid=(m // tt,),
            in_specs=[pl.BlockSpec((tt, D_MODEL), lambda i, st: (i, 0)), col, col,
                      pl.BlockSpec((1, 6, D_MODEL), lambda i, st: ((i * tt) // rows_per_mod, 0, 0)),
                      pl.BlockSpec((1, D_MODEL), lambda i, st: (0, 0)),
                      pl.BlockSpec(memory_space=pl.ANY)],
            out_specs=pl.BlockSpec((tt, D_MODEL), lambda i, st: (i, 0)),
            scratch_shapes=[pltpu.VMEM((COMBINE_BUFFERS, rc * SUBLANES, LANES), F32), pltpu.VMEM((tt, D_MODEL), F32),
                            pltpu.SemaphoreType.DMA((COMBINE_BUFFERS,))]),
        out_shape=jax.ShapeDtypeStruct((m, D_MODEL), F32),
        compiler_params=_params("arbitrary"),
        name="moe_combine",
    )(starts, x, cnt.reshape(m, 1), pos.reshape(m, 1), mod, final_g.reshape(1, D_MODEL), y)


def _expert_choice(x, g, mod, rows_per_mod, w_router, w_gate, w_up, w_down, layer, gi, final_g, final):
    n = x.shape[0]
    cap = EC_CAPACITY_FACTOR * n // N_EXPERTS
    tt = min(512, n)
    h, aff = _moe_pre(x, g, mod, rows_per_mod, w_router)
    gat, sct, gate, cnt, pos = _route(aff.reshape(N_EXPERTS, n // LANES, LANES), cap)
    gat, sct, gate = (a.reshape(N_EXPERTS, cap) for a in (gat, sct, gate))
    tc = min(512, cap // 2)
    y = _expert_ffn(h, gat, sct, gate, w_gate, w_up, w_down, layer, N_EXPERTS * cap + tc, tc)
    return _combine(x, y, cnt.reshape(n), pos.reshape(n), mod, rows_per_mod, gi, final_g, final, tt, tc)


def _trunk(x, mods, rows_per_mod, batch, seq, caches, p):
    depth = p["mod_w"].shape[0]
    n = batch * seq
    new = {}
    for layer in range(depth):
        kind, slot = layer % 4, layer // 4
        mod = mods[layer]
        g_mix = p["norm_mix_g"][layer]
        if kind == 0:
            w_ab = jnp.pad(p["gdn_w_ab"][slot], ((0, 0), (0, LANES - 4 * GDN_HEADS)))
            w_cat = jnp.concatenate([p["gdn_w_in"][slot], w_ab], axis=1).astype(BF16)
            if caches is None:
                s0 = jnp.zeros((batch, 2, GDN_HEADS, GDN_DK, LANES), F32)
            else:
                s0 = caches["state_gdn"][:, slot]
            o, s_fin = _gdn_mixer(x, g_mix, mod, rows_per_mod, batch, seq, s0, w_cat, p["gdn_conv_w"][slot],
                                  p["gdn_a_log"][slot], p["gdn_dt_bias"][slot], p["gdn_norm_g"][slot])
            new.setdefault("gdn", []).append(s_fin)
            w_out = p["gdn_w_out"][slot]
        elif kind == 1:
            proj = _nmm(x, g_mix, mod, p["diff_w_in"][slot].astype(BF16), rows_per_mod, 0, 1)
            nk = seq
            if caches is None:
                assert DIFF_HEADS == SUBLANES
                q, k, v, kc, vc = _attn_prep(proj, "diff", seq, False)
                new.setdefault("diff_k", []).append(kc.reshape(batch, seq, DIFF_HEADS, LANES))
                new.setdefault("diff_v", []).append(vc.reshape(batch, seq, DIFF_HEADS, LANES))
            else:
                q, k, v = _attn_prep(proj, "diff", seq, True)
                kc = caches["cache_diff_k"][:, slot].reshape(batch, -1, DIFF_HEADS * LANES).astype(BF16)
                vc = caches["cache_diff_v"][:, slot].reshape(batch, -1, DIFF_HEADS * LANES).astype(BF16)
                nk = seq + kc.shape[1]
                k = jnp.concatenate([k.reshape(batch, seq, -1), kc], axis=1).reshape(batch * nk, -1)
                v = jnp.concatenate([v.reshape(batch, seq, -1), vc], axis=1).reshape(batch * nk, -1)
            lam_init = 0.8 - 0.6 * math.exp(-0.3 * layer)
            o = _attention(q, k, v, "diff", batch, seq, nk,
                           (p["diff_lambda"][slot], p["diff_subln_g"][slot]), lam_init)
            w_out = p["diff_w_out"][slot]
        elif kind == 2:
            proj = _nmm(x, g_mix, mod, p["gqa_w_in"][slot].astype(BF16), rows_per_mod, 0, 1)
            q, k, v, kf = _attn_prep(proj, "gqa", seq, caches is not None,
                                     (p["gqa_q_norm_g"][slot], p["gqa_k_norm_g"][slot]))
            nk = seq
            if caches is None:
                v0 = (GQA_HEADS + GQA_KV) * LANES
                new.setdefault("gqa_k", []).append(kf.reshape(batch, seq, GQA_KV, GQA_HD))
                new.setdefault("gqa_v", []).append(proj[:, v0:].reshape(batch, seq, GQA_KV, GQA_HD))
            else:
                kc = caches["cache_gqa_k"][:, slot].reshape(batch, -1, GQA_KV * GQA_HD).astype(BF16)
                vc = caches["cache_gqa_v"][:, slot].reshape(batch, -1, GQA_KV * GQA_HD).astype(BF16)
                nk = seq + kc.shape[1]
                k = jnp.concatenate([k.reshape(batch, seq, -1), kc], axis=1).reshape(batch * nk, -1)
                v = jnp.concatenate([v.reshape(batch, seq, -1), vc], axis=1).reshape(batch * nk, -1)
            o = _attention(q, k, v, "gqa", batch, seq, nk)
            w_out = p["gqa_w_out"][slot]
        else:
            proj = _nmm(x, g_mix, mod, p["swa_w_in"][slot].astype(BF16), rows_per_mod, 0, 1)
            q, k, v = _attn_prep(proj, "swa", seq, caches is not None)
            sinks = p["swa_sinks"][slot].astype(F32)
            if caches is None:
                k0 = SWA_HEADS * SWA_HD
                v0 = k0 + SWA_KV * SWA_HD
                new.setdefault("swa_k", []).append(proj[:, k0:v0].reshape(batch, seq, SWA_KV, SWA_HD))
                new.setdefault("swa_v", []).append(proj[:, v0:].reshape(batch, seq, SWA_KV, SWA_HD))
                o = _attention(q, k, v, "swa", batch, seq, seq, (sinks,))
            else:
                kc = jnp.pad(caches["cache_swa_k"][:, slot], ((0, 0), (0, 0), (0, 0), (0, LANES - SWA_HD)))
                past = kc.shape[1]
                kc = kc.reshape(batch, past, SWA_KV * LANES).astype(BF16)
                vc = caches["cache_swa_v"][:, slot].reshape(batch, past, SWA_KV * SWA_HD).astype(BF16)
                nk = seq + past
                k = jnp.concatenate([k.reshape(batch, seq, -1), kc], axis=1).reshape(batch * nk, -1)
                v = jnp.concatenate([v.reshape(batch, seq, -1), vc], axis=1).reshape(batch * nk, -1)
                o = _band_attention(q, k, v, sinks, batch, seq, past)
            w_out = p["swa_w_out"][slot]
        x = _mm_resid(o, w_out.astype(BF16), x, mod, rows_per_mod, 2)
        x = _expert_choice(x, p["norm_ffn_g"][layer], mod, rows_per_mod, p["moe_router"][layer],
                           p["moe_w_gate"], p["moe_w_up"], p["moe_w_down"], layer,
                           5, p["final_g"], layer == depth - 1)
    return x, new


def kernel(x_prompt, x_sample, state_gdn, cache_diff_k, cache_diff_v, cache_gqa_k, cache_gqa_v, cache_swa_k, cache_swa_v, c, c_ctx, mod_w, mod_b, norm_mix_g, norm_ffn_g, final_g, gdn_w_in, gdn_conv_w, gdn_w_ab, gdn_a_log, gdn_dt_bias, gdn_norm_g, gdn_w_out, diff_w_in, diff_lambda, diff_subln_g, diff_w_out, gqa_w_in, gqa_q_norm_g, gqa_k_norm_g, gqa_w_out, swa_w_in, swa_sinks, swa_w_out, moe_router, moe_w_gate, moe_w_up, moe_w_down):
    p = dict(mod_w=mod_w, mod_b=mod_b, norm_mix_g=norm_mix_g, norm_ffn_g=norm_ffn_g, final_g=final_g,
             gdn_w_in=gdn_w_in, gdn_conv_w=gdn_conv_w, gdn_w_ab=gdn_w_ab, gdn_a_log=gdn_a_log,
             gdn_dt_bias=gdn_dt_bias, gdn_norm_g=gdn_norm_g, gdn_w_out=gdn_w_out,
             diff_w_in=diff_w_in, diff_lambda=diff_lambda, diff_subln_g=diff_subln_g, diff_w_out=diff_w_out,
             gqa_w_in=gqa_w_in, gqa_q_norm_g=gqa_q_norm_g, gqa_k_norm_g=gqa_k_norm_g, gqa_w_out=gqa_w_out,
             swa_w_in=swa_w_in, swa_sinks=swa_sinks, swa_w_out=swa_w_out,
             moe_router=moe_router, moe_w_gate=moe_w_gate, moe_w_up=moe_w_up, moe_w_down=moe_w_down)
    batch, seq, _ = x_prompt.shape
    dec_batch, dec_seq, _ = x_sample.shape
    depth = mod_w.shape[0]
    rows = 8 * ((1 + dec_batch + 7) // 8)
    cond = jnp.zeros((rows, D_MODEL), F32).at[0].set(c_ctx).at[1:1 + dec_batch].set(c)
    mod_all = _mod_proj(cond, mod_w, mod_b).reshape(depth, rows, 6, D_MODEL)
    mods_ctx = [mod_all[l, 0:1] for l in range(depth)]
    mods_lat = [mod_all[l, 1:1 + dec_batch] for l in range(depth)]

    y_prompt, st = _trunk(x_prompt.reshape(batch * seq, D_MODEL), mods_ctx, batch * seq, batch, seq, None, p)
    caches = dict(state_gdn=state_gdn, cache_diff_k=cache_diff_k, cache_diff_v=cache_diff_v,
                  cache_gqa_k=cache_gqa_k, cache_gqa_v=cache_gqa_v,
                  cache_swa_k=cache_swa_k, cache_swa_v=cache_swa_v)
    y_sample, _ = _trunk(x_sample.reshape(dec_batch * dec_seq, D_MODEL), mods_lat, dec_seq, dec_batch, dec_seq,
                         caches, p)

    def stack(xs):
        return jnp.stack(xs, axis=1)

    return (y_prompt.reshape(batch, seq, D_MODEL), y_sample.reshape(dec_batch, dec_seq, D_MODEL),
            stack(st["gdn"]), stack(st["diff_k"]), stack(st["diff_v"]),
            stack(st["gqa_k"]), stack(st["gqa_v"]), stack(st["swa_k"]), stack(st["swa_v"]))
```
